```python
import math
import jax, jax.numpy as jnp
from jax import lax
import numpy as np

D_MODEL = 1024
BATCH = 16
SEQ = 256
DEPTH = 2
DEC_BATCH = 8
DEC_SEQ = 2048
PAST_LEN = 256

GRID_W = 64
N_MIXERS = 2
N_DIR = 2
N_HEADS = 8
HEAD_K = 128
HEAD_V = 128
KEY_DIM = N_HEADS * HEAD_K
VAL_DIM = N_HEADS * HEAD_V
QKV_DIM = 2 * KEY_DIM + VAL_DIM
GDN_PROJ = QKV_DIM + VAL_DIM + 2 * N_DIR * N_HEADS
HGRN_PROJ = KEY_DIM + N_DIR * KEY_DIM + 2 * VAL_DIM
CONV_W = 5
GDN_CHUNK = 64
HGRN_CHUNK = 16
D_FF = 4 * D_MODEL
N_GDN = (DEPTH + 1) // 2
N_HGRN = DEPTH // 2
EPS = 1e-6
STATE_SCALE = 0.5

kernel_name = 'hybrid_gdn_hgrn2_flow_step'


def rms_norm(x, g):
    x32 = x.astype(jnp.float32)
    y = x32 * lax.rsqrt(jnp.mean(x32 * x32, axis=-1, keepdims=True) + EPS)
    return (y * g.astype(jnp.float32)).astype(x.dtype)


def l2_normalise(x):
    return x * lax.rsqrt(jnp.sum(x * x, axis=-1, keepdims=True) + EPS)


def centred_conv(x, w):
    pad = CONV_W // 2
    n = x.shape[-2]
    xp = jnp.pad(x, [(0, 0)] * (x.ndim - 2) + [(pad, pad), (0, 0)])
    return sum(xp[..., j:j + n, :] * w[j] for j in range(CONV_W))


def token_conv(x, w, grid):
    if not grid:
        return centred_conv(x, w)
    b, t, ch = x.shape
    rows = t // GRID_W
    return centred_conv(x.reshape(b, rows, GRID_W, ch), w).reshape(b, t, ch)


def to_heads(x, d):
    b, t, _ = x.shape
    return x.reshape(b, t, N_HEADS, d).transpose(0, 2, 1, 3)


def to_chunks(x, c):
    return x.reshape(x.shape[:2] + (x.shape[2] // c, c) + x.shape[3:])


def gated_delta_chunk(q, k, v, g, beta, s0):
    out_dtype = v.dtype
    q, k, v, g, beta, s = (a.astype(jnp.float32) for a in (q, k, v, g, beta, s0))
    b, h, t, dk = q.shape
    c = GDN_CHUNK
    q = to_chunks(q * dk ** -0.5, c)
    k = to_chunks(k, c)
    v = to_chunks(v, c)
    g = jnp.cumsum(to_chunks(g, c), axis=-1)
    beta = to_chunks(beta, c)
    causal = jnp.tril(jnp.ones((c, c), dtype=bool))
    strict = jnp.tril(jnp.ones((c, c), jnp.float32), -1)
    decay = jnp.exp(jnp.where(causal, g[..., :, None] - g[..., None, :], -jnp.inf))
    kb = k * beta[..., None]
    lower = jnp.einsum('bhncd,bhnsd->bhncs', kb, k) * decay * strict
    eye = jnp.eye(c, dtype=jnp.float32)
    tmat = lax.linalg.triangular_solve(eye + lower, jnp.broadcast_to(eye, lower.shape), left_side=True, lower=True)
    u = jnp.einsum('bhncs,bhnse->bhnce', tmat, v * beta[..., None])
    w = jnp.einsum('bhncs,bhnsd->bhncd', tmat, kb * jnp.exp(g)[..., None])
    qk = jnp.einsum('bhncd,bhnsd->bhncs', q, k) * decay
    q_dec = q * jnp.exp(g)[..., None]
    k_tail = k * jnp.exp(g[..., -1:] - g)[..., None]
    g_last = jnp.exp(g[..., -1])

    def step(state, xs):
        w_n, u_n, qd_n, qk_n, kt_n, gl_n = xs
        v_new = u_n - jnp.einsum('bhcd,bhde->bhce', w_n, state)
        o_n = jnp.einsum('bhcd,bhde->bhce', qd_n, state) + jnp.einsum('bhcs,bhse->bhce', qk_n, v_new)
        state = state * gl_n[..., None, None] + jnp.einsum('bhcd,bhce->bhde', kt_n, v_new)
        return state, o_n

    xs = tuple(jnp.moveaxis(a, 2, 0) for a in (w, u, q_dec, qk, k_tail, g_last))
    s, o = lax.scan(step, s, xs)
    o = jnp.moveaxis(o, 0, 2).reshape(b, h, t, v.shape[-1])
    return o.astype(out_dtype), s.astype(s0.dtype)


def hgrn2_chunk(q, k, v, logf, s0):
    out_dtype = v.dtype
    q, k, v, logf, s = (a.astype(jnp.float32) for a in (q, k, v, logf, s0))
    b, h, t, _ = q.shape
    c = HGRN_CHUNK
    q, k, v, logf = (to_chunks(a, c) for a in (q, k, v, logf))
    bcum = jnp.cumsum(logf, axis=-2)
    q_in = q * jnp.exp(bcum)
    k_out = k * jnp.exp(-bcum)
    causal = jnp.tril(jnp.ones((c, c), dtype=bool))
    att = jnp.where(causal, jnp.einsum('bhncd,bhnsd->bhncs', q_in, k_out), 0.0)
    o_intra = jnp.einsum('bhncs,bhnse->bhnce', att, v)
    b_last = bcum[..., -1:, :]
    k_tail = k * jnp.exp(b_last - bcum)
    f_last = jnp.exp(b_last[..., 0, :])

    def step(state, xs):
        qi_n, kt_n, v_n, fl_n = xs
        o_n = jnp.einsum('bhcd,bhde->bhce', qi_n, state)
        state = state * fl_n[..., :, None] + jnp.einsum('bhcd,bhce->bhde', kt_n, v_n)
        return state, o_n

    xs = tuple(jnp.moveaxis(a, 2, 0) for a in (q_in, k_tail, v, f_last))
    s, o_inter = lax.scan(step, s, xs)
    o = o_intra + jnp.moveaxis(o_inter, 0, 2)
    return o.reshape(b, h, t, v.shape[-1]).astype(out_dtype), s.astype(s0.dtype)


def flip_t(a):
    return jnp.flip(a, axis=2)


def gated_deltanet_mixer(h, w_in, conv_w, a_log, dt_bias, onorm_g, w_out, s0, grid):
    b, t, _ = h.shape
    proj = h @ w_in
    qkv, z, braw, araw = jnp.split(proj, [QKV_DIM, QKV_DIM + VAL_DIM, QKV_DIM + VAL_DIM + N_DIR * N_HEADS], axis=-1)
    qkv = jax.nn.silu(token_conv(qkv, conv_w, grid))
    q, k, v = jnp.split(qkv, [KEY_DIM, 2 * KEY_DIM], axis=-1)
    q = l2_normalise(to_heads(q, HEAD_K).astype(jnp.float32))
    k = l2_normalise(to_heads(k, HEAD_K).astype(jnp.float32))
    v = to_heads(v, HEAD_V)
    beta = jax.nn.sigmoid(braw.astype(jnp.float32)).reshape(b, t, N_DIR, N_HEADS)
    g = -jnp.exp(a_log.astype(jnp.float32)) * jax.nn.softplus(
        araw.astype(jnp.float32).reshape(b, t, N_DIR, N_HEADS) + dt_bias.astype(jnp.float32))
    outs, finals = [], []
    for d in range(N_DIR):
        args = (q, k, v, g[:, :, d].transpose(0, 2, 1), beta[:, :, d].transpose(0, 2, 1))
        if d:
            args = tuple(flip_t(a) for a in args)
        o_d, s_d = gated_delta_chunk(*args, s0[:, d])
        outs.append(flip_t(o_d) if d else o_d)
        finals.append(s_d)
    o = rms_norm(outs[0] + outs[1], onorm_g) * jax.nn.silu(to_heads(z, HEAD_V))
    o = o.transpose(0, 2, 1, 3).reshape(b, t, VAL_DIM) @ w_out
    return o, jnp.stack(finals, axis=1)


def hgrn2_mixer(h, w_in, lb, onorm_g, w_out, s0):
    b, t, _ = h.shape
    proj = h @ w_in
    q, f, i, z = jnp.split(proj, [KEY_DIM, 3 * KEY_DIM, 3 * KEY_DIM + VAL_DIM], axis=-1)
    q = to_heads(jax.nn.silu(q), HEAD_K)
    v = to_heads(i, HEAD_V)
    fgate = lb + (1.0 - lb) * jax.nn.sigmoid(f.astype(jnp.float32).reshape(b, t, N_DIR, KEY_DIM))
    outs, finals = [], []
    for d in range(N_DIR):
        fd = to_heads(fgate[:, :, d], HEAD_K)
        args = (q, 1.0 - fd, v, jnp.log(fd))
        if d:
            args = tuple(flip_t(a) for a in args)
        o_d, s_d = hgrn2_chunk(*args, s0[:, d])
        outs.append(flip_t(o_d) if d else o_d)
        finals.append(s_d)
    o = rms_norm(outs[0] + outs[1], onorm_g) * jax.nn.silu(to_heads(z, HEAD_V))
    o = o.transpose(0, 2, 1, 3).reshape(b, t, VAL_DIM) @ w_out
    return o, jnp.stack(finals, axis=1)


def layer_lower_bounds(lb_logits):
    p = jax.nn.softmax(lb_logits.astype(jnp.float32), axis=0)
    return jnp.cumsum(p, axis=0) - p[0]


def run_trunk(x, cond, s_gdn, s_hgrn, grid, weights):
    (w_ada, b_ada, norm_g, gdn_w_in, gdn_conv_w, gdn_a_log, gdn_dt_bias, gdn_onorm_g, gdn_w_out,
     hgrn_w_in, hgrn_lb_logits, hgrn_onorm_g, hgrn_w_out, mlp_w1, mlp_w2) = weights
    lb_all = layer_lower_bounds(hgrn_lb_logits)
    cond_act = jax.nn.silu(cond)
    new_gdn, new_hgrn = [], []
    for layer in range(DEPTH):
        mod = (cond_act @ w_ada[layer] + b_ada[layer])[:, None, :]
        sh1, sc1, gt1, sh2, sc2, gt2 = jnp.split(mod, 6, axis=-1)
        hmod = rms_norm(x, norm_g[layer, 0]) * (1.0 + sc1) + sh1
        j = layer // N_MIXERS
        if layer % N_MIXERS == 0:
            mix, s_fin = gated_deltanet_mixer(hmod, gdn_w_in[j], gdn_conv_w[j], gdn_a_log[j], gdn_dt_bias[j],
                                              gdn_onorm_g[j], gdn_w_out[j], s_gdn[:, j], grid)
            new_gdn.append(s_fin)
        else:
            mix, s_fin = hgrn2_mixer(hmod, hgrn_w_in[j], lb_all[layer], hgrn_onorm_g[j], hgrn_w_out[j], s_hgrn[:, j])
            new_hgrn.append(s_fin)
        x = x + gt1 * rms_norm(mix, norm_g[layer, 1])
        hmod = rms_norm(x, norm_g[layer, 2]) * (1.0 + sc2) + sh2
        ff = jnp.square(jax.nn.relu(hmod @ mlp_w1[layer])) @ mlp_w2[layer]
        x = x + gt2 * rms_norm(ff, norm_g[layer, 3])
    return x, jnp.stack(new_gdn, axis=1), jnp.stack(new_hgrn, axis=1)


def setup_inputs(seed: int = 0) -> dict:
    key = jax.random.key(seed)
    ks = jax.random.split(key, 22)
    f32 = jnp.float32

    def nrm(k, shape, scale):
        return jax.random.normal(k, shape, f32) * scale

    dt = jnp.exp(jax.random.uniform(ks[12], (N_GDN, N_DIR, N_HEADS), f32, math.log(1e-3), math.log(1e-1)))
    return {
        'x_prompt': nrm(ks[0], (BATCH, SEQ, D_MODEL), 1.0),
        'x_sample': nrm(ks[1], (DEC_BATCH, DEC_SEQ, D_MODEL), 1.0),
        'state_gdn': nrm(ks[2], (DEC_BATCH, N_GDN, N_DIR, N_HEADS, HEAD_K, HEAD_V), STATE_SCALE),
        'state_hgrn': nrm(ks[3], (DEC_BATCH, N_HGRN, N_DIR, N_HEADS, HEAD_K, HEAD_V), STATE_SCALE),
        'c': nrm(ks[4], (DEC_BATCH, D_MODEL), 1.0),
        'c_ctx': nrm(ks[5], (D_MODEL,), 1.0),
        'w_ada': nrm(ks[6], (DEPTH, D_MODEL, 6 * D_MODEL), 0.5 * D_MODEL ** -0.5),
        'b_ada': nrm(ks[7], (DEPTH, 6 * D_MODEL), 0.02),
        'norm_g': 1.0 + nrm(ks[8], (DEPTH, 4, D_MODEL), 0.05),
        'gdn_w_in': nrm(ks[9], (N_GDN, D_MODEL, GDN_PROJ), D_MODEL ** -0.5),
        'gdn_conv_w': nrm(ks[10], (N_GDN, CONV_W, QKV_DIM), CONV_W ** -0.5),
        'gdn_a_log': jnp.log(jax.random.uniform(ks[11], (N_GDN, N_DIR, N_HEADS), f32, 1.0, 16.0)),
        'gdn_dt_bias': dt + jnp.log(-jnp.expm1(-dt)),
        'gdn_onorm_g': 1.0 + nrm(ks[13], (N_GDN, HEAD_V), 0.05),
        'gdn_w_out': nrm(ks[14], (N_GDN, VAL_DIM, D_MODEL), VAL_DIM ** -0.5),
        'hgrn_w_in': nrm(ks[15], (N_HGRN, D_MODEL, HGRN_PROJ), D_MODEL ** -0.5),
        'hgrn_lb_logits': nrm(ks[16], (DEPTH, N_DIR, KEY_DIM), 0.5),
        'hgrn_onorm_g': 1.0 + nrm(ks[17], (N_HGRN, HEAD_V), 0.05),
        'hgrn_w_out': nrm(ks[18], (N_HGRN, VAL_DIM, D_MODEL), VAL_DIM ** -0.5),
        'mlp_w1': nrm(ks[19], (DEPTH, D_MODEL, D_FF), D_MODEL ** -0.5),
        'mlp_w2': nrm(ks[20], (DEPTH, D_FF, D_MODEL), D_FF ** -0.5),
    }


def reference(x_prompt, x_sample, state_gdn, state_hgrn, c, c_ctx, w_ada, b_ada, norm_g,
              gdn_w_in, gdn_conv_w, gdn_a_log, gdn_dt_bias, gdn_onorm_g, gdn_w_out,
              hgrn_w_in, hgrn_lb_logits, hgrn_onorm_g, hgrn_w_out, mlp_w1, mlp_w2):
    weights = (w_ada, b_ada, norm_g, gdn_w_in, gdn_conv_w, gdn_a_log, gdn_dt_bias, gdn_onorm_g, gdn_w_out,
               hgrn_w_in, hgrn_lb_logits, hgrn_onorm_g, hgrn_w_out, mlp_w1, mlp_w2)
    nb = x_prompt.shape[0]
    zeros_gdn = jnp.zeros((nb, N_GDN, N_DIR, N_HEADS, HEAD_K, HEAD_V), x_prompt.dtype)
    zeros_hgrn = jnp.zeros((nb, N_HGRN, N_DIR, N_HEADS, HEAD_K, HEAD_V), x_prompt.dtype)
    y_prompt, new_state_gdn, new_state_hgrn = run_trunk(x_prompt, c_ctx[None, :], zeros_gdn, zeros_hgrn, False, weights)
    y_sample, _, _ = run_trunk(x_sample, c, state_gdn, state_hgrn, True, weights)
    return (y_prompt, y_sample, new_state_gdn, new_state_hgrn)
```

```python
import functools

import jax
import jax.numpy as jnp
from jax import lax
from jax.experimental import pallas as pl
from jax.experimental.pallas import tpu as pltpu

D_MODEL = 1024
N_HEADS = 8
HEAD_DIM = 128
KEY_DIM = N_HEADS * HEAD_DIM
CONV_W = 5
GDN_CHUNK = 64
HGRN_CHUNK = 16
GRID_W = 64
D_FF = 4 * D_MODEL
EPS = 1e-6
GDN_MAIN = 4 * KEY_DIM
GATE_LANES = 128
HGRN_PROJ = 5 * KEY_DIM

ROW_TILE = 512
PRE_ROWS = 256
VMEM_LIMIT = 56 * 1024 * 1024

BF16 = jnp.bfloat16
F32 = jnp.float32

_NT = (((1,), (1,)), ((), ()))
_TN = (((0,), (0,)), ((), ()))


def _dot(a, b, dims=None):
    a = a.astype(BF16)
    b = b.astype(BF16)
    if dims is None:
        return jnp.dot(a, b, preferred_element_type=F32)
    return lax.dot_general(a, b, dims, preferred_element_type=F32)


def _segment_sums(x, seg):
    rows = x.shape[0]
    pos = lax.broadcasted_iota(jnp.int32, x.shape, 0) & (seg - 1)
    pre, suf = x, x
    s = 1
    while s < seg:
        pre = pre + jnp.where(pos >= s, pltpu.roll(pre, s, axis=0), 0.0)
        suf = suf + jnp.where(pos + s < seg, pltpu.roll(suf, rows - s, axis=0), 0.0)
        s *= 2
    return pre, suf


def _rms(x, g):
    return x * lax.rsqrt(jnp.mean(x * x, axis=-1, keepdims=True) + EPS) * g


def _silu(x):
    return x * jax.nn.sigmoid(x)


def _softplus(x):
    return jnp.maximum(x, 0.0) + jnp.log1p(jnp.exp(-jnp.abs(x)))


def _const_spec(shape):
    zeros = (0,) * len(shape)
    return pl.BlockSpec(shape, lambda *_: zeros, pipeline_mode=pl.Buffered(1))


def _mod_spec(col, tiles_per_seq):
    if tiles_per_seq is None:
        return pl.BlockSpec((None, 1, D_MODEL), lambda i: (0, 0, col))
    return pl.BlockSpec((None, 1, D_MODEL), lambda i: (i // tiles_per_seq, 0, col))


def _params(n_grid):
    return pltpu.CompilerParams(dimension_semantics=("arbitrary",) * n_grid, vmem_limit_bytes=VMEM_LIMIT)


def _ada_kernel(c_ref, w_ref, b_ref, o_ref):
    o_ref[...] = _dot(_silu(c_ref[...]), w_ref[...]) + b_ref[...]


def _ada_mod(cond, w_ada, b_ada):
    depth, _, n = w_ada.shape
    rows = cond.shape[0]
    tn = 1536
    return pl.pallas_call(
        _ada_kernel,
        grid=(depth, n // tn),
        in_specs=[pl.BlockSpec((rows, D_MODEL), lambda l, j: (0, 0)),
                  pl.BlockSpec((None, D_MODEL, tn), lambda l, j: (l, 0, j)),
                  pl.BlockSpec((None, 1, tn), lambda l, j: (l, 0, j))],
        out_specs=pl.BlockSpec((None, rows, tn), lambda l, j: (l, 0, j)),
        out_shape=jax.ShapeDtypeStruct((depth, rows, n), F32),
        compiler_params=_params(2),
        name="ada_mod",
    )(cond, w_ada, b_ada.reshape(depth, 1, n))


def _modulated(x_ref, g_ref, sh_ref, sc_ref):
    return (_rms(x_ref[...], g_ref[0:1, :]) * (1.0 + sc_ref[...]) + sh_ref[...]).astype(BF16)


def _gdn_inproj_kernel(x_ref, g_ref, sh_ref, sc_ref, w_ref, wg_ref, alog_ref, dt_ref, proj_ref, gate_ref):
    hb = _modulated(x_ref, g_ref, sh_ref, sc_ref)
    for j in range(GDN_MAIN // 512):
        cols = slice(j * 512, (j + 1) * 512)
        proj_ref[:, cols] = jnp.dot(hb, w_ref[:, cols], preferred_element_type=F32)
    raw = jnp.dot(hb, wg_ref[...], preferred_element_type=F32)
    beta = jax.nn.sigmoid(raw)
    g = -jnp.exp(alog_ref[...]) * _softplus(raw + dt_ref[...])
    gp, gs = _segment_sums(g, GDN_CHUNK)
    lane = lax.broadcasted_iota(jnp.int32, raw.shape, 1)
    gate_ref[...] = jnp.where(lane < 2 * N_HEADS, beta,
                              jnp.where(lane < 3 * N_HEADS, gp,
                                        jnp.where(lane < 4 * N_HEADS, gs, 0.0)))


def _gdn_inproj(x2d, norm_g, mod, tiles_per_seq, w_main, w_gate, alog_row, dt_row):
    rows = x2d.shape[0]
    return pl.pallas_call(
        _gdn_inproj_kernel,
        grid=(rows // ROW_TILE,),
        in_specs=[pl.BlockSpec((ROW_TILE, D_MODEL), lambda i: (i, 0)),
                  _const_spec((4, D_MODEL)),
                  _mod_spec(0, tiles_per_seq), _mod_spec(1, tiles_per_seq),
                  _const_spec((D_MODEL, GDN_MAIN)), _const_spec((D_MODEL, GATE_LANES)),
                  _const_spec((1, GATE_LANES)), _const_spec((1, GATE_LANES))],
        out_specs=[pl.BlockSpec((ROW_TILE, GDN_MAIN), lambda i: (i, 0)),
                   pl.BlockSpec((ROW_TILE, GATE_LANES), lambda i: (i, 0))],
        out_shape=[jax.ShapeDtypeStruct((rows, GDN_MAIN), F32),
                   jax.ShapeDtypeStruct((rows, GATE_LANES), F32)],
        compiler_params=_params(1),
        name="gdn_inproj",
    )(x2d, norm_g, mod, mod, w_main, w_gate, alog_row, dt_row)


def _hgrn_inproj_kernel(x_ref, g_ref, sh_ref, sc_ref, w_ref, proj_ref):
    hb = _modulated(x_ref, g_ref, sh_ref, sc_ref)
    for j in range(HGRN_PROJ // 512):
        cols = slice(j * 512, (j + 1) * 512)
        proj_ref[:, cols] = jnp.dot(hb, w_ref[:, cols], preferred_element_type=F32)


def _hgrn_inproj(x2d, norm_g, mod, tiles_per_seq, w_in):
    rows = x2d.shape[0]
    return pl.pallas_call(
        _hgrn_inproj_kernel,
        grid=(rows // ROW_TILE,),
        in_specs=[pl.BlockSpec((ROW_TILE, D_MODEL), lambda i: (i, 0)),
                  _const_spec((4, D_MODEL)),
                  _mod_spec(0, tiles_per_seq), _mod_spec(1, tiles_per_seq),
                  _const_spec((D_MODEL, HGRN_PROJ))],
        out_specs=pl.BlockSpec((ROW_TILE, HGRN_PROJ), lambda i: (i, 0)),
        out_shape=jax.ShapeDtypeStruct((rows, HGRN_PROJ), F32),
        compiler_params=_params(1),
        name="hgrn_inproj",
    )(x2d, norm_g, mod, mod, w_in)


def _neumann_inverse(low):
    c = low.shape[0]
    eye = (lax.broadcasted_iota(jnp.int32, (c, c), 0) == lax.broadcasted_iota(jnp.int32, (c, c), 1)).astype(F32)
    acc = eye - low
    power = low
    steps = c.bit_length() - 2
    for _ in range(steps):
        power = jnp.dot(power, power, preferred_element_type=F32, precision=lax.Precision.HIGHEST)
        acc = acc + jnp.dot(acc, power, preferred_element_type=F32, precision=lax.Precision.HIGHEST)
    return acc


def _gdn_kernel(q_ref, k_ref, v_ref, z_ref, gcol_ref, grow_ref, cwq_ref, cwk_ref, cwv_ref, on_ref, *rest,
                seg, has_init, emit_state):
    rest = list(rest)
    s0_ref = rest.pop(0) if has_init else None
    o_ref = rest.pop(0)
    sfin_ref = rest.pop(0) if emit_state else None
    qn, kn, vn, ob, st = rest
    t_len = q_ref.shape[0]
    n_chunks = t_len // GDN_CHUNK
    h = pl.program_id(1)
    c = GDN_CHUNK

    pos = lax.broadcasted_iota(jnp.int32, (PRE_ROWS, HEAD_DIM), 0) & (seg - 1)

    def conv_silu(x, w_ref):
        acc = x * w_ref[CONV_W // 2:CONV_W // 2 + 1, :]
        for s in (-2, -1, 1, 2):
            shifted = pltpu.roll(x, (-s) % PRE_ROWS, axis=0)
            valid = (pos + s >= 0) & (pos + s < seg)
            acc = acc + jnp.where(valid, shifted, 0.0) * w_ref[s + 2:s + 3, :]
        return _silu(acc)

    def unit(x):
        return x * lax.rsqrt(jnp.sum(x * x, axis=-1, keepdims=True) + EPS)

    def pre(i, carry):
        rows = pl.ds(pl.multiple_of(i * PRE_ROWS, PRE_ROWS), PRE_ROWS)
        qn[rows, :] = unit(conv_silu(q_ref[rows, :], cwq_ref)) * (HEAD_DIM ** -0.5)
        kn[rows, :] = unit(conv_silu(k_ref[rows, :], cwk_ref))
        vn[rows, :] = conv_silu(v_ref[rows, :], cwv_ref)
        return carry

    lax.fori_loop(0, t_len // PRE_ROWS, pre, 0)

    for d in range(2):
        st[d] = s0_ref[d] if has_init else jnp.zeros((HEAD_DIM, HEAD_DIM), F32)

    ri = lax.broadcasted_iota(jnp.int32, (c, c), 0)
    ci = lax.broadcasted_iota(jnp.int32, (c, c), 1)
    lane = lax.broadcasted_iota(jnp.int32, (c, GATE_LANES), 1)

    def column(tile, idx):
        return jnp.sum(jnp.where(lane == idx, tile, 0.0), axis=-1, keepdims=True)

    def chunk(n, d):
        cidx = n if d == 0 else n_chunks - 1 - n
        rows = pl.ds(pl.multiple_of(cidx * c, c), c)
        q, k, v = qn[rows, :], kn[rows, :], vn[rows, :]
        gtile = gcol_ref[rows, :]
        beta = column(gtile, d * N_HEADS + h)
        gc = column(gtile, (2 + d) * N_HEADS + h)
        gr = grow_ref[cidx, pl.ds((2 + d) * N_HEADS + h, 1), :]
        g_last = gc[c - 1:c, :] if d == 0 else gc[0:1, :]
        causal = (ci <= ri) if d == 0 else (ci >= ri)
        strict = (ci < ri) if d == 0 else (ci > ri)
        decay = jnp.where(causal, jnp.exp(gc - gr), 0.0)
        eg = jnp.exp(gc)
        kb = k * beta
        a = _dot(jnp.concatenate([kb, q], axis=0), k, _NT)
        low = jnp.where(strict, a[:c] * decay, 0.0)
        qk = a[c:] * decay
        tmat = _neumann_inverse(low)
        uw = _dot(tmat, jnp.concatenate([v * beta, kb * eg], axis=1))
        u, w = uw[:, :HEAD_DIM], uw[:, HEAD_DIM:]
        state = st[d]
        ws = _dot(jnp.concatenate([w, q * eg], axis=0), state)
        v_new = u - ws[:c]
        out = ws[c:] + _dot(qk, v_new)
        k_tail = k * jnp.exp(g_last - gc)
        st[d] = state * jnp.exp(g_last) + _dot(k_tail, v_new, _TN)
        if d == 0:
            o_ref[rows, :] = out
        else:
            ob[rows, :] = out

    def body(n, carry):
        chunk(n, 0)
        chunk(n, 1)
        return carry

    lax.fori_loop(0, n_chunks, body, 0)

    def post(i, carry):
        rows = pl.ds(pl.multiple_of(i * PRE_ROWS, PRE_ROWS), PRE_ROWS)
        o_ref[rows, :] = _rms(o_ref[rows, :] + ob[rows, :], on_ref[...]) * _silu(z_ref[rows, :])
        return carry

    lax.fori_loop(0, t_len // PRE_ROWS, post, 0)
    if emit_state:
        for d in range(2):
            sfin_ref[d] = st[d]


def _gdn_mixer(proj, gates, conv_w, onorm_g, s0, layer_idx, grid_conv, emit_state):
    b, t, _ = proj.shape
    n_chunks = t // GDN_CHUNK
    grow = gates[:, :, :4 * N_HEADS].reshape(b, n_chunks, GDN_CHUNK, 4 * N_HEADS).transpose(0, 1, 3, 2)
    has_init = s0 is not None
    col = lambda off: pl.BlockSpec((None, t, HEAD_DIM), lambda i, h: (i, 0, off + h))
    cw = lambda off: pl.BlockSpec((CONV_W, HEAD_DIM), lambda i, h: (0, off + h))
    in_specs = [col(0), col(N_HEADS), col(2 * N_HEADS), col(3 * N_HEADS),
                pl.BlockSpec((None, t, GATE_LANES), lambda i, h: (i, 0, 0)),
                pl.BlockSpec((None, n_chunks, 4 * N_HEADS, GDN_CHUNK), lambda i, h: (i, 0, 0, 0)),
                cw(0), cw(N_HEADS), cw(2 * N_HEADS),
                pl.BlockSpec((1, HEAD_DIM), lambda i, h: (0, 0))]
    args = [proj, proj, proj, proj, gates, grow, conv_w, conv_w, conv_w, onorm_g.reshape(1, HEAD_DIM)]
    if has_init:
        in_specs.append(pl.BlockSpec((None, None, 2, None, HEAD_DIM, HEAD_DIM),
                                     lambda i, h: (i, layer_idx, 0, h, 0, 0)))
        args.append(s0)
    out_specs = [pl.BlockSpec((None, t, HEAD_DIM), lambda i, h: (i, 0, h))]
    out_shape = [jax.ShapeDtypeStruct((b, t, KEY_DIM), F32)]
    if emit_state:
        out_specs.append(pl.BlockSpec((None, 2, None, HEAD_DIM, HEAD_DIM), lambda i, h: (i, 0, h, 0, 0)))
        out_shape.append(jax.ShapeDtypeStruct((b, 2, N_HEADS, HEAD_DIM, HEAD_DIM), F32))
    res = pl.pallas_call(
        functools.partial(_gdn_kernel, seg=GRID_W if grid_conv else t, has_init=has_init, emit_state=emit_state),
        grid=(b, N_HEADS),
        in_specs=in_specs,
        out_specs=out_specs,
        out_shape=out_shape,
        scratch_shapes=[pltpu.VMEM((t, HEAD_DIM), F32)] * 4 + [pltpu.VMEM((2, HEAD_DIM, HEAD_DIM), F32)],
        compiler_params=_params(2),
        name="gdn_mixer",
    )(*args)
    return res[0], (res[1] if emit_state else None)


def _hgrn_kernel(q_ref, ff_ref, fb_ref, i_ref, z_ref, lbl_ref, on_ref, *rest, layer, has_init, emit_state):
    rest = list(rest)
    s0_ref = rest.pop(0) if has_init else None
    o_ref = rest.pop(0)
    sfin_ref = rest.pop(0) if emit_state else None
    qin, kout, ktail, ftot, ob, st = rest
    t_len = q_ref.shape[0]
    c = HGRN_CHUNK
    n_chunks = t_len // c

    depth = lbl_ref.shape[0]
    logits = [lbl_ref[l] for l in range(depth)]
    top = functools.reduce(jnp.maximum, logits)
    ex = [jnp.exp(l - top) for l in logits]
    denom = functools.reduce(lambda x, y: x + y, ex)
    lb = functools.reduce(lambda x, y: x + y, ex[:layer + 1]) / denom - ex[0] / denom

    def pre(i, carry):
        rows = pl.ds(pl.multiple_of(i * PRE_ROWS, PRE_ROWS), PRE_ROWS)
        q = _silu(q_ref[rows, :])
        for d, f_ref in enumerate((ff_ref, fb_ref)):
            lbd = lb[d:d + 1, :]
            fg = lbd + (1.0 - lbd) * jax.nn.sigmoid(f_ref[rows, :])
            k = 1.0 - fg
            logf = jnp.log(fg)
            psum, ssum = _segment_sums(logf, c)
            bcum = psum if d == 0 else ssum
            rest_sum = (ssum if d == 0 else psum) - logf
            qin[d, rows, :] = (q * jnp.exp(bcum)).astype(BF16)
            kout[d, rows, :] = (k * jnp.exp(-bcum)).astype(BF16)
            ktail[d, rows, :] = (k * jnp.exp(rest_sum)).astype(BF16)
            ftot[d, rows, :] = jnp.exp(psum + ssum - logf)
        return carry

    lax.fori_loop(0, t_len // PRE_ROWS, pre, 0)

    for d in range(2):
        st[d] = s0_ref[d].T if has_init else jnp.zeros((HEAD_DIM, HEAD_DIM), F32)

    ri = lax.broadcasted_iota(jnp.int32, (c, c), 0)
    ci = lax.broadcasted_iota(jnp.int32, (c, c), 1)

    def chunk(n, d):
        cidx = n if d == 0 else n_chunks - 1 - n
        rows = pl.ds(pl.multiple_of(cidx * c, c), c)
        qi, ko, kt = qin[d, rows, :], kout[d, rows, :], ktail[d, rows, :]
        v = i_ref[rows, :].astype(BF16)
        causal = (ci <= ri) if d == 0 else (ci >= ri)
        att = jnp.where(causal, _dot(qi, ko, _NT), 0.0)
        state = st[d]
        out = _dot(att, v) + _dot(qi, state, _NT)
        f_last = ftot[d, pl.ds(pl.multiple_of(cidx * c, c), 1), :]
        st[d] = state * f_last + _dot(v, kt, _TN)
        if d == 0:
            o_ref[rows, :] = out
        else:
            ob[rows, :] = out

    def body(n, carry):
        chunk(n, 0)
        chunk(n, 1)
        return carry

    lax.fori_loop(0, n_chunks, body, 0, unroll=4)

    def post(i, carry):
        rows = pl.ds(pl.multiple_of(i * PRE_ROWS, PRE_ROWS), PRE_ROWS)
        o_ref[rows, :] = _rms(o_ref[rows, :] + ob[rows, :], on_ref[...]) * _silu(z_ref[rows, :])
        return carry

    lax.fori_loop(0, t_len // PRE_ROWS, post, 0)
    if emit_state:
        for d in range(2):
            sfin_ref[d] = st[d].T


def _hgrn_mixer(proj, lb_logits, onorm_g, s0, layer, layer_idx, emit_state):
    b, t, _ = proj.shape
    depth = lb_logits.shape[0]
    has_init = s0 is not None
    col = lambda off: pl.BlockSpec((None, t, HEAD_DIM), lambda i, h: (i, 0, off + h))
    in_specs = [col(0), col(N_HEADS), col(2 * N_HEADS), col(3 * N_HEADS), col(4 * N_HEADS),
                pl.BlockSpec((depth, 2, HEAD_DIM), lambda i, h: (0, 0, h)),
                pl.BlockSpec((1, HEAD_DIM), lambda i, h: (0, 0))]
    args = [proj, proj, proj, proj, proj, lb_logits, onorm_g.reshape(1, HEAD_DIM)]
    if has_init:
        in_specs.append(pl.BlockSpec((None, None, 2, None, HEAD_DIM, HEAD_DIM),
                                     lambda i, h: (i, layer_idx, 0, h, 0, 0)))
        args.append(s0)
    out_specs = [pl.BlockSpec((None, t, HEAD_DIM), lambda i, h: (i, 0, h))]
    out_shape = [jax.ShapeDtypeStruct((b, t, KEY_DIM), F32)]
    if emit_state:
        out_specs.append(pl.BlockSpec((None, 2, None, HEAD_DIM, HEAD_DIM), lambda i, h: (i, 0, h, 0, 0)))
        out_shape.append(jax.ShapeDtypeStruct((b, 2, N_HEADS, HEAD_DIM, HEAD_DIM), F32))
    res = pl.pallas_call(
        functools.partial(_hgrn_kernel, layer=layer, has_init=has_init, emit_state=emit_state),
        grid=(b, N_HEADS),
        in_specs=in_specs,
        out_specs=out_specs,
        out_shape=out_shape,
        scratch_shapes=[pltpu.VMEM((2, t, HEAD_DIM), BF16)] * 3
                       + [pltpu.VMEM((2, t, HEAD_DIM), F32), pltpu.VMEM((t, HEAD_DIM), F32),
                          pltpu.VMEM((2, HEAD_DIM, HEAD_DIM), F32)],
        compiler_params=_params(2),
        name="hgrn_mixer",
    )(*args)
    return res[0], (res[1] if emit_state else None)


def _post_kernel(x_ref, o_ref, g_ref, gt1_ref, sh2_ref, sc2_ref, gt2_ref, wo_ref, w1_ref, w2_ref, y_ref):
    mix = jnp.dot(o_ref[...].astype(BF16), wo_ref[...], preferred_element_type=F32)
    x1 = x_ref[...] + gt1_ref[...] * _rms(mix, g_ref[1:2, :])
    hb = (_rms(x1, g_ref[2:3, :]) * (1.0 + sc2_ref[...]) + sh2_ref[...]).astype(BF16)
    ff = jnp.zeros(x1.shape, F32)
    for j in range(D_FF // 1024):
        cols = slice(j * 1024, (j + 1) * 1024)
        hid = jnp.maximum(jnp.dot(hb, w1_ref[:, cols], preferred_element_type=F32), 0.0)
        ff = ff + jnp.dot((hid * hid).astype(BF16), w2_ref[cols, :], preferred_element_type=F32)
    y_ref[...] = x1 + gt2_ref[...] * _rms(ff, g_ref[3:4, :])


def _post_mixer(x2d, o2d, norm_g, mod, tiles_per_seq, w_out, w1, w2):
    rows = x2d.shape[0]
    tile = pl.BlockSpec((ROW_TILE, D_MODEL), lambda i: (i, 0))
    return pl.pallas_call(
        _post_kernel,
        grid=(rows // ROW_TILE,),
        in_specs=[tile, tile, _const_spec((4, D_MODEL)),
                  _mod_spec(2, tiles_per_seq), _mod_spec(3, tiles_per_seq),
                  _mod_spec(4, tiles_per_seq), _mod_spec(5, tiles_per_seq),
                  _const_spec((D_MODEL, D_MODEL)), _const_spec((D_MODEL, D_FF)), _const_spec((D_FF, D_MODEL))],
        out_specs=tile,
        out_shape=jax.ShapeDtypeStruct((rows, D_MODEL), F32),
        compiler_params=_params(1),
        name="post_mixer",
    )(x2d, o2d, norm_g, mod, mod, mod, mod, w_out, w1, w2)


def _trunk(x, mod_rows, per_seq_mod, s_gdn, s_hgrn, grid_conv, emit_state, weights):
    (norm_g, gdn_main, gdn_gate, gdn_alog, gdn_dt, gdn_conv_w, gdn_onorm_g, gdn_w_out,
     hgrn_w_in, hgrn_lb_logits, hgrn_onorm_g, hgrn_w_out, mlp_w1, mlp_w2) = weights
    b, t, _ = x.shape
    tiles_per_seq = t // ROW_TILE if per_seq_mod else None
    x2d = x.reshape(b * t, D_MODEL)
    depth = norm_g.shape[0]
    fin_gdn, fin_hgrn = [], []
    for layer in range(depth):
        j = layer // 2
        mod = mod_rows[layer]
        if layer % 2 == 0:
            proj, gates = _gdn_inproj(x2d, norm_g[layer], mod, tiles_per_seq, gdn_main[j], gdn_gate[j],
                                      gdn_alog[j], gdn_dt[j])
            o, fin = _gdn_mixer(proj.reshape(b, t, GDN_MAIN), gates.reshape(b, t, GATE_LANES), gdn_conv_w[j],
                                gdn_onorm_g[j], s_gdn, j, grid_conv, emit_state)
            fin_gdn.append(fin)
            w_out = gdn_w_out[j]
        else:
            proj = _hgrn_inproj(x2d, norm_g[layer], mod, tiles_per_seq, hgrn_w_in[j])
            o, fin = _hgrn_mixer(proj.reshape(b, t, HGRN_PROJ), hgrn_lb_logits, hgrn_onorm_g[j], s_hgrn, layer, j,
                                 emit_state)
            fin_hgrn.append(fin)
            w_out = hgrn_w_out[j]
        x2d = _post_mixer(x2d, o.reshape(b * t, KEY_DIM), norm_g[layer], mod, tiles_per_seq, w_out,
                          mlp_w1[layer], mlp_w2[layer])
    y = x2d.reshape(b, t, D_MODEL)
    if emit_state:
        return y, jnp.stack(fin_gdn, axis=1), jnp.stack(fin_hgrn, axis=1)
    return y, None, None


def kernel(x_prompt, x_sample, state_gdn, state_hgrn, c, c_ctx, w_ada, b_ada, norm_g, gdn_w_in, gdn_conv_w,
           gdn_a_log, gdn_dt_bias, gdn_onorm_g, gdn_w_out, hgrn_w_in, hgrn_lb_logits, hgrn_onorm_g, hgrn_w_out,
           mlp_w1, mlp_w2):
    n_dec = c.shape[0]
    n_rows = 16
    cond = jnp.concatenate([c_ctx[None, :], c, jnp.zeros((n_rows - 1 - n_dec, D_MODEL), F32)], axis=0)
    mod = _ada_mod(cond, w_ada, b_ada)
    mod_ctx = mod[:, 0:1, None, :]
    mod_smp = mod[:, 1:1 + n_dec, None, :]

    n_gdn = gdn_w_in.shape[0]
    gate_pad = GATE_LANES - 4 * N_HEADS
    gdn_gate = jnp.pad(gdn_w_in[:, :, GDN_MAIN:], ((0, 0), (0, 0), (0, gate_pad))).astype(BF16)
    lead = jnp.zeros((n_gdn, 2 * N_HEADS), F32)
    tail = jnp.zeros((n_gdn, gate_pad), F32)
    gdn_alog = jnp.concatenate([lead, gdn_a_log.reshape(n_gdn, 2 * N_HEADS), tail], axis=1)[:, None, :]
    gdn_dt = jnp.concatenate([lead, gdn_dt_bias.reshape(n_gdn, 2 * N_HEADS), tail], axis=1)[:, None, :]
    weights = (norm_g, gdn_w_in[:, :, :GDN_MAIN].astype(BF16), gdn_gate, gdn_alog, gdn_dt, gdn_conv_w,
               gdn_onorm_g, gdn_w_out.astype(BF16), hgrn_w_in.astype(BF16), hgrn_lb_logits, hgrn_onorm_g,
               hgrn_w_out.astype(BF16), mlp_w1.astype(BF16), mlp_w2.astype(BF16))

    y_prompt, new_gdn, new_hgrn = _trunk(x_prompt, mod_ctx, False, None, None, False, True, weights)
    y_sample, _, _ = _trunk(x_sample, mod_smp, True, state_gdn, state_hgrn, True, False, weights)
    return (y_prompt, y_sample, new_gdn, new_hgrn)
```

```python
import functools

import jax
import jax.numpy as jnp
from jax import lax
from jax.experimental import pallas as pl
from jax.experimental.pallas import tpu as pltpu

D_MODEL = 1024
N_HEADS = 8
HEAD_DIM = 128
KEY_DIM = N_HEADS * HEAD_DIM
CONV_W = 5
GDN_CHUNK = 64
HGRN_CHUNK = 16
GRID_W = 64
D_FF = 4 * D_MODEL
EPS = 1e-6
GDN_MAIN = 4 * KEY_DIM
GATE_LANES = 128
HGRN_PROJ = 5 * KEY_DIM

ROW_TILE = 512
PRE_ROWS = 256
GDN_GROUP = 4
VMEM_LIMIT = 56 * 1024 * 1024

BF16 = jnp.bfloat16
F32 = jnp.float32

_NT = (((1,), (1,)), ((), ()))
_TN = (((0,), (0,)), ((), ()))


def _dot(a, b, dims=None):
    a = a.astype(BF16)
    b = b.astype(BF16)
    if dims is None:
        return jnp.dot(a, b, preferred_element_type=F32)
    return lax.dot_general(a, b, dims, preferred_element_type=F32)


def _segment_sums(x, seg):
    rows = x.shape[0]
    pos = lax.broadcasted_iota(jnp.int32, x.shape, 0) & (seg - 1)
    pre, suf = x, x
    s = 1
    while s < seg:
        pre = pre + jnp.where(pos >= s, pltpu.roll(pre, s, axis=0), 0.0)
        suf = suf + jnp.where(pos + s < seg, pltpu.roll(suf, rows - s, axis=0), 0.0)
        s *= 2
    return pre, suf


def _rms(x, g):
    return x * lax.rsqrt(jnp.mean(x * x, axis=-1, keepdims=True) + EPS) * g


def _silu(x):
    return x * jax.nn.sigmoid(x)


def _softplus(x):
    return jnp.maximum(x, 0.0) + jnp.log1p(jnp.exp(-jnp.abs(x)))


def _const_spec(shape):
    zeros = (0,) * len(shape)
    return pl.BlockSpec(shape, lambda *_: zeros, pipeline_mode=pl.Buffered(1))


def _mod_spec(col, tiles_per_seq):
    if tiles_per_seq is None:
        return pl.BlockSpec((None, 1, D_MODEL), lambda i: (0, 0, col))
    return pl.BlockSpec((None, 1, D_MODEL), lambda i: (i // tiles_per_seq, 0, col))


def _params(n_grid):
    return pltpu.CompilerParams(dimension_semantics=("arbitrary",) * n_grid, vmem_limit_bytes=VMEM_LIMIT)


def _ada_kernel(c_ref, w_ref, b_ref, o_ref):
    o_ref[...] = _dot(_silu(c_ref[...]), w_ref[...]) + b_ref[...]


def _ada_mod(cond, w_ada, b_ada):
    depth, _, n = w_ada.shape
    rows = cond.shape[0]
    tn = 1536
    return pl.pallas_call(
        _ada_kernel,
        grid=(depth, n // tn),
        in_specs=[pl.BlockSpec((rows, D_MODEL), lambda l, j: (0, 0)),
                  pl.BlockSpec((None, D_MODEL, tn), lambda l, j: (l, 0, j)),
                  pl.BlockSpec((None, 1, tn), lambda l, j: (l, 0, j))],
        out_specs=pl.BlockSpec((None, rows, tn), lambda l, j: (l, 0, j)),
        out_shape=jax.ShapeDtypeStruct((depth, rows, n), F32),
        compiler_params=_params(2),
        name="ada_mod",
    )(cond, w_ada, b_ada.reshape(depth, 1, n))


def _modulated(x_ref, g_ref, sh_ref, sc_ref):
    return (_rms(x_ref[...], g_ref[0:1, :]) * (1.0 + sc_ref[...]) + sh_ref[...]).astype(BF16)


def _gdn_inproj_kernel(x_ref, g_ref, sh_ref, sc_ref, w_ref, wg_ref, alog_ref, dt_ref, proj_ref, gate_ref):
    hb = _modulated(x_ref, g_ref, sh_ref, sc_ref)
    for j in range(GDN_MAIN // 512):
        cols = slice(j * 512, (j + 1) * 512)
        proj_ref[:, cols] = jnp.dot(hb, w_ref[:, cols], preferred_element_type=F32)
    raw = jnp.dot(hb, wg_ref[...], preferred_element_type=F32)
    beta = jax.nn.sigmoid(raw)
    g = -jnp.exp(alog_ref[...]) * _softplus(raw + dt_ref[...])
    gp, gs = _segment_sums(g, GDN_CHUNK)
    lane = lax.broadcasted_iota(jnp.int32, raw.shape, 1)
    gate_ref[...] = jnp.where(lane < 2 * N_HEADS, beta,
                              jnp.where(lane < 3 * N_HEADS, gp,
                                        jnp.where(lane < 4 * N_HEADS, gs, 0.0)))


def _gdn_inproj(x2d, norm_g, mod, tiles_per_seq, w_main, w_gate, alog_row, dt_row):
    rows = x2d.shape[0]
    return pl.pallas_call(
        _gdn_inproj_kernel,
        grid=(rows // ROW_TILE,),
        in_specs=[pl.BlockSpec((ROW_TILE, D_MODEL), lambda i: (i, 0)),
                  _const_spec((4, D_MODEL)),
                  _mod_spec(0, tiles_per_seq), _mod_spec(1, tiles_per_seq),
                  _const_spec((D_MODEL, GDN_MAIN)), _const_spec((D_MODEL, GATE_LANES)),
                  _const_spec((1, GATE_LANES)), _const_spec((1, GATE_LANES))],
        out_specs=[pl.BlockSpec((ROW_TILE, GDN_MAIN), lambda i: (i, 0)),
                   pl.BlockSpec((ROW_TILE, GATE_LANES), lambda i: (i, 0))],
        out_shape=[jax.ShapeDtypeStruct((rows, GDN_MAIN), F32),
                   jax.ShapeDtypeStruct((rows, GATE_LANES), F32)],
        compiler_params=_params(1),
        name="gdn_inproj",
    )(x2d, norm_g, mod, mod, w_main, w_gate, alog_row, dt_row)


def _hgrn_inproj_kernel(x_ref, g_ref, sh_ref, sc_ref, w_ref, proj_ref):
    hb = _modulated(x_ref, g_ref, sh_ref, sc_ref)
    for j in range(HGRN_PROJ // 512):
        cols = slice(j * 512, (j + 1) * 512)
        proj_ref[:, cols] = jnp.dot(hb, w_ref[:, cols], preferred_element_type=F32)


def _hgrn_inproj(x2d, norm_g, mod, tiles_per_seq, w_in):
    rows = x2d.shape[0]
    return pl.pallas_call(
        _hgrn_inproj_kernel,
        grid=(rows // ROW_TILE,),
        in_specs=[pl.BlockSpec((ROW_TILE, D_MODEL), lambda i: (i, 0)),
                  _const_spec((4, D_MODEL)),
                  _mod_spec(0, tiles_per_seq), _mod_spec(1, tiles_per_seq),
                  _const_spec((D_MODEL, HGRN_PROJ))],
        out_specs=pl.BlockSpec((ROW_TILE, HGRN_PROJ), lambda i: (i, 0)),
        out_shape=jax.ShapeDtypeStruct((rows, HGRN_PROJ), F32),
        compiler_params=_params(1),
        name="hgrn_inproj",
    )(x2d, norm_g, mod, mod, w_in)


def _split_bf16(x):
    hi = lax.bitcast_convert_type(lax.bitcast_convert_type(x, jnp.int32) & jnp.int32(-65536), F32)
    return hi, x - hi


def _pair_inverses(lows, left, diag2):
    c = lows[0].shape[0]

    def block_diag(x):
        return jnp.concatenate([jnp.where(left, x, 0.0), jnp.where(left, 0.0, x)], axis=0)

    def left_operand(hi, lo):
        return jnp.concatenate([hi.astype(BF16), lo.astype(BF16)] * 2, axis=1)

    def right_operand(hi, lo):
        bh, bl = block_diag(hi).astype(BF16), block_diag(lo).astype(BF16)
        return jnp.concatenate([bh, bh, bl, bl], axis=0)

    accs = [jnp.where(diag2, 1.0, 0.0) - x for x in lows]
    parts = [_split_bf16(x) for x in lows]
    powers = [jnp.dot(left_operand(hi, lo), right_operand(hi, lo), preferred_element_type=F32) for hi, lo in parts]
    levels = c.bit_length() - 2
    for level in range(levels):
        parts = [_split_bf16(x) for x in powers]
        rhs = [right_operand(hi, lo) for hi, lo in parts]
        acc_lhs = [left_operand(*_split_bf16(a)) for a in accs]
        if level + 1 < levels:
            res = [jnp.dot(jnp.concatenate([left_operand(hi, lo), al], axis=0), r, preferred_element_type=F32)
                   for (hi, lo), al, r in zip(parts, acc_lhs, rhs)]
            powers = [x[:c] for x in res]
            accs = [a + x[c:] for a, x in zip(accs, res)]
        else:
            accs = [a + jnp.dot(al, r, preferred_element_type=F32) for a, al, r in zip(accs, acc_lhs, rhs)]
    return accs


def _gdn_kernel(q_ref, k_ref, v_ref, z_ref, gcol_ref, grow_ref, cwq_ref, cwk_ref, cwv_ref, on_ref, *rest,
                seg, has_init, emit_state, group):
    rest = list(rest)
    s0_ref = rest.pop(0) if has_init else None
    o_ref = rest.pop(0)
    sfin_ref = rest.pop(0) if emit_state else None
    qn, kn, vn, ob, m_s, b_s, q_s, gl_s, st = rest
    t_len = q_ref.shape[0]
    c = GDN_CHUNK
    n_chunks = t_len // c
    h = pl.program_id(1)

    pos = lax.broadcasted_iota(jnp.int32, (PRE_ROWS, HEAD_DIM), 0) & (seg - 1)

    def conv_silu(x, w_ref):
        acc = x * w_ref[CONV_W // 2:CONV_W // 2 + 1, :]
        for s in (-2, -1, 1, 2):
            shifted = pltpu.roll(x, (-s) % PRE_ROWS, axis=0)
            valid = (pos + s >= 0) & (pos + s < seg)
            acc = acc + jnp.where(valid, shifted, 0.0) * w_ref[s + 2:s + 3, :]
        return _silu(acc)

    def unit(x):
        return x * lax.rsqrt(jnp.sum(x * x, axis=-1, keepdims=True) + EPS)

    def pre(i, carry):
        rows = pl.ds(pl.multiple_of(i * PRE_ROWS, PRE_ROWS), PRE_ROWS)
        qn[rows, :] = unit(conv_silu(q_ref[rows, :], cwq_ref)) * (HEAD_DIM ** -0.5)
        kn[rows, :] = unit(conv_silu(k_ref[rows, :], cwk_ref))
        vn[rows, :] = conv_silu(v_ref[rows, :], cwv_ref)
        return carry

    lax.fori_loop(0, t_len // PRE_ROWS, pre, 0)

    lane = lax.broadcasted_iota(jnp.int32, (c, 2 * c), 1)
    row = lax.broadcasted_iota(jnp.int32, (c, 2 * c), 0)
    left = lane < c
    ahead = jnp.where(left, row - lane, lane - c - row)

    def column(tile, idx):
        return jnp.sum(jnp.where(lane == idx, tile, 0.0), axis=-1, keepdims=True)

    zb = jnp.zeros((c, 2 * HEAD_DIM), BF16)

    def direction_blocks(a0, a1):
        return jnp.concatenate([jnp.concatenate([a0.astype(BF16), zb], axis=1),
                                jnp.concatenate([zb, a1.astype(BF16)], axis=1)], axis=0)

    def load(cidx):
        rows = pl.ds(pl.multiple_of(cidx * c, c), c)
        gtile = gcol_ref[rows, :]
        e = dict(cidx=cidx, rows=rows, q=qn[rows, :], k=kn[rows, :], v=vn[rows, :])
        e["beta"] = [column(gtile, d * N_HEADS + h) for d in range(2)]
        e["gc"] = [column(gtile, (2 + d) * N_HEADS + h) for d in range(2)]
        e["g_last"] = [e["gc"][0][c - 1:c, :], e["gc"][1][0:1, :]]
        gr2 = grow_ref[cidx, pl.ds(h, 1), :]
        e["decay2"] = jnp.exp(jnp.where(ahead >= 0, jnp.where(left, e["gc"][0], e["gc"][1]) - gr2, -jnp.inf))
        return e

    def gram(e):
        k = e["k"]
        kq = _dot(jnp.concatenate([k, e["q"]], axis=0), jnp.concatenate([k, k], axis=0), _NT)
        e["low2"] = jnp.where(ahead > 0, kq[:c] * jnp.where(left, e["beta"][0], e["beta"][1]) * e["decay2"], 0.0)
        e["qk2"] = kq[c:] * e["decay2"]

    def solve(e, t2):
        k, v, beta, gc = e["k"], e["v"], e["beta"], e["gc"]
        e["eg"] = [jnp.exp(gc[d]) for d in range(2)]
        rhs = [jnp.concatenate([v * beta[d], k * beta[d] * e["eg"][d]], axis=1) for d in range(2)]
        e["uw"] = jnp.dot(t2.astype(BF16), direction_blocks(*rhs), preferred_element_type=F32)

    def fold(e):
        uw = e["uw"]
        r2 = direction_blocks(uw[:, :2 * HEAD_DIM], uw[:, 2 * HEAD_DIM:])
        e["oq"] = jnp.dot(e["qk2"].astype(BF16), r2, preferred_element_type=F32)
        kt = jnp.concatenate([e["k"] * jnp.exp(e["g_last"][d] - e["gc"][d]) for d in range(2)], axis=0)
        e["bm"] = _dot(kt, r2, _TN)

    def store(e):
        cidx, rows, oq, bm = e["cidx"], e["rows"], e["oq"], e["bm"]
        for d in range(2):
            base = 2 * d * HEAD_DIM
            out_ref = o_ref if d == 0 else ob
            out_ref[rows, :] = oq[:, base:base + HEAD_DIM]
            q_s[cidx, d] = (e["q"] * e["eg"][d] - oq[:, base + HEAD_DIM:base + 2 * HEAD_DIM]).astype(BF16)
            b_s[cidx, d] = bm[:, base:base + HEAD_DIM]
            m_s[cidx, d] = bm[:, base + HEAD_DIM:base + 2 * HEAD_DIM].astype(BF16)
            gl_s[cidx, d] = jnp.broadcast_to(jnp.exp(e["g_last"][d]), (1, HEAD_DIM))

    def prep_body(i, carry):
        chunks = [load(i * group + j) for j in range(group)]
        for e in chunks:
            gram(e)
        inverses = _pair_inverses([e["low2"] for e in chunks], left, ahead == 0)
        for e, t2 in zip(chunks, inverses):
            solve(e, t2)
        for e in chunks:
            fold(e)
        for e in chunks:
            store(e)
        return carry

    lax.fori_loop(0, n_chunks // group, prep_body, 0)

    for d in range(2):
        st[d] = s0_ref[d] if has_init else jnp.zeros((HEAD_DIM, HEAD_DIM), F32)

    def scan_body(n, carry):
        for d in range(2):
            cidx = n if d == 0 else n_chunks - 1 - n
            rows = pl.ds(pl.multiple_of(cidx * c, c), c)
            state = st[d]
            ms = jnp.dot(jnp.concatenate([m_s[cidx, d], q_s[cidx, d]], axis=0), state.astype(BF16),
                         preferred_element_type=F32)
            out_ref = o_ref if d == 0 else ob
            out_ref[rows, :] = out_ref[rows, :] + ms[HEAD_DIM:]
            st[d] = state * gl_s[cidx, d] - ms[:HEAD_DIM] + b_s[cidx, d]
        return carry

    lax.fori_loop(0, n_chunks, scan_body, 0)

    def post(i, carry):
        rows = pl.ds(pl.multiple_of(i * PRE_ROWS, PRE_ROWS), PRE_ROWS)
        o_ref[rows, :] = _rms(o_ref[rows, :] + ob[rows, :], on_ref[...]) * _silu(z_ref[rows, :])
        return carry

    lax.fori_loop(0, t_len // PRE_ROWS, post, 0)
    if emit_state:
        for d in range(2):
            sfin_ref[d] = st[d]


def _gdn_mixer(proj, gates, conv_w, onorm_g, s0, layer_idx, grid_conv, emit_state):
    b, t, _ = proj.shape
    c = GDN_CHUNK
    n_chunks = t // c
    seg = GRID_W if grid_conv else t
    assert PRE_ROWS % seg == 0 and t % PRE_ROWS == 0 and 2 * c == HEAD_DIM
    as_rows = lambda g: g.reshape(b, n_chunks, c, N_HEADS).transpose(0, 1, 3, 2)
    grow = jnp.concatenate([as_rows(gates[:, :, 2 * N_HEADS:3 * N_HEADS]),
                            as_rows(gates[:, :, 3 * N_HEADS:4 * N_HEADS])], axis=-1)
    has_init = s0 is not None
    col = lambda off: pl.BlockSpec((None, t, HEAD_DIM), lambda i, h: (i, 0, off + h))
    cw = lambda off: pl.BlockSpec((CONV_W, HEAD_DIM), lambda i, h: (0, off + h))
    in_specs = [col(0), col(N_HEADS), col(2 * N_HEADS), col(3 * N_HEADS),
                pl.BlockSpec((None, t, GATE_LANES), lambda i, h: (i, 0, 0)),
                pl.BlockSpec((None, n_chunks, N_HEADS, 2 * c), lambda i, h: (i, 0, 0, 0)),
                cw(0), cw(N_HEADS), cw(2 * N_HEADS),
                pl.BlockSpec((1, HEAD_DIM), lambda i, h: (0, 0))]
    args = [proj, proj, proj, proj, gates, grow, conv_w, conv_w, conv_w, onorm_g.reshape(1, HEAD_DIM)]
    if has_init:
        in_specs.append(pl.BlockSpec((None, None, 2, None, HEAD_DIM, HEAD_DIM),
                                     lambda i, h: (i, layer_idx, 0, h, 0, 0)))
        args.append(s0)
    out_specs = [pl.BlockSpec((None, t, HEAD_DIM), lambda i, h: (i, 0, h))]
    out_shape = [jax.ShapeDtypeStruct((b, t, KEY_DIM), F32)]
    if emit_state:
        out_specs.append(pl.BlockSpec((None, 2, None, HEAD_DIM, HEAD_DIM), lambda i, h: (i, 0, h, 0, 0)))
        out_shape.append(jax.ShapeDtypeStruct((b, 2, N_HEADS, HEAD_DIM, HEAD_DIM), F32))
    scratch = [pltpu.VMEM((t, HEAD_DIM), F32)] * 4 + [
        pltpu.VMEM((n_chunks, 2, HEAD_DIM, HEAD_DIM), BF16),
        pltpu.VMEM((n_chunks, 2, HEAD_DIM, HEAD_DIM), F32),
        pltpu.VMEM((n_chunks, 2, c, HEAD_DIM), BF16),
        pltpu.VMEM((n_chunks, 2, 1, HEAD_DIM), F32),
        pltpu.VMEM((2, HEAD_DIM, HEAD_DIM), F32)]
    res = pl.pallas_call(
        functools.partial(_gdn_kernel, seg=seg, has_init=has_init, emit_state=emit_state, group=GDN_GROUP),
        grid=(b, N_HEADS),
        in_specs=in_specs,
        out_specs=out_specs,
        out_shape=out_shape,
        scratch_shapes=scratch,
        compiler_params=_params(2),
        name="gdn_mixer",
    )(*args)
    return res[0], (res[1] if emit_state else None)


def _hgrn_kernel(q_ref, ff_ref, fb_ref, i_ref, z_ref, lbl_ref, on_ref, *rest, layer, has_init, emit_state):
    rest = list(rest)
    s0_ref = rest.pop(0) if has_init else None
    o_ref = rest.pop(0)
    sfin_ref = rest.pop(0) if emit_state else None
    qin, kout, ktail, ftot, ob, st = rest
    t_len = q_ref.shape[0]
    c = HGRN_CHUNK
    n_chunks = t_len // c

    depth = lbl_ref.shape[0]
    logits = [lbl_ref[l] for l in range(depth)]
    top = functools.reduce(jnp.maximum, logits)
    ex = [jnp.exp(l - top) for l in logits]
    denom = functools.reduce(lambda x, y: x + y, ex)
    lb = functools.reduce(lambda x, y: x + y, ex[:layer + 1]) / denom - ex[0] / denom

    def pre(i, carry):
        rows = pl.ds(pl.multiple_of(i * PRE_ROWS, PRE_ROWS), PRE_ROWS)
        q = _silu(q_ref[rows, :])
        for d, f_ref in enumerate((ff_ref, fb_ref)):
            lbd = lb[d:d + 1, :]
            fg = lbd + (1.0 - lbd) * jax.nn.sigmoid(f_ref[rows, :])
            k = 1.0 - fg
            logf = jnp.log(fg)
            psum, ssum = _segment_sums(logf, c)
            bcum = psum if d == 0 else ssum
            rest_sum = (ssum if d == 0 else psum) - logf
            qin[d, rows, :] = (q * jnp.exp(bcum)).astype(BF16)
            kout[d, rows, :] = (k * jnp.exp(-bcum)).astype(BF16)
            ktail[d, rows, :] = (k * jnp.exp(rest_sum)).astype(BF16)
            ftot[d, rows, :] = jnp.exp(psum + ssum - logf)
        return carry

    lax.fori_loop(0, t_len // PRE_ROWS, pre, 0)

    for d in range(2):
        st[d] = s0_ref[d].T if has_init else jnp.zeros((HEAD_DIM, HEAD_DIM), F32)

    ri = lax.broadcasted_iota(jnp.int32, (c, c), 0)
    ci = lax.broadcasted_iota(jnp.int32, (c, c), 1)

    def chunk(n, d):
        cidx = n if d == 0 else n_chunks - 1 - n
        rows = pl.ds(pl.multiple_of(cidx * c, c), c)
        qi, ko, kt = qin[d, rows, :], kout[d, rows, :], ktail[d, rows, :]
        v = i_ref[rows, :].astype(BF16)
        causal = (ci <= ri) if d == 0 else (ci >= ri)
        att = jnp.where(causal, _dot(qi, ko, _NT), 0.0)
        state = st[d]
        out = _dot(att, v) + _dot(qi, state, _NT)
        f_last = ftot[d, pl.ds(pl.multiple_of(cidx * c, c), 1), :]
        st[d] = state * f_last + _dot(v, kt, _TN)
        if d == 0:
            o_ref[rows, :] = out
        else:
            ob[rows, :] = out

    def body(n, carry):
        chunk(n, 0)
        chunk(n, 1)
        return carry

    lax.fori_loop(0, n_chunks, body, 0, unroll=4)

    def post(i, carry):
        rows = pl.ds(pl.multiple_of(i * PRE_ROWS, PRE_ROWS), PRE_ROWS)
        o_ref[rows, :] = _rms(o_ref[rows, :] + ob[rows, :], on_ref[...]) * _silu(z_ref[rows, :])
        return carry

    lax.fori_loop(0, t_len // PRE_ROWS, post, 0)
    if emit_state:
        for d in range(2):
            sfin_ref[d] = st[d].T


def _hgrn_mixer(proj, lb_logits, onorm_g, s0, layer, layer_idx, emit_state):
    b, t, _ = proj.shape
    depth = lb_logits.shape[0]
    has_init = s0 is not None
    col = lambda off: pl.BlockSpec((None, t, HEAD_DIM), lambda i, h: (i, 0, off + h))
    in_specs = [col(0), col(N_HEADS), col(2 * N_HEADS), col(3 * N_HEADS), col(4 * N_HEADS),
                pl.BlockSpec((depth, 2, HEAD_DIM), lambda i, h: (0, 0, h)),
                pl.BlockSpec((1, HEAD_DIM), lambda i, h: (0, 0))]
    args = [proj, proj, proj, proj, proj, lb_logits, onorm_g.reshape(1, HEAD_DIM)]
    if has_init:
        in_specs.append(pl.BlockSpec((None, None, 2, None, HEAD_DIM, HEAD_DIM),
                                     lambda i, h: (i, layer_idx, 0, h, 0, 0)))
        args.append(s0)
    out_specs = [pl.BlockSpec((None, t, HEAD_DIM), lambda i, h: (i, 0, h))]
    out_shape = [jax.ShapeDtypeStruct((b, t, KEY_DIM), F32)]
    if emit_state:
        out_specs.append(pl.BlockSpec((None, 2, None, HEAD_DIM, HEAD_DIM), lambda i, h: (i, 0, h, 0, 0)))
        out_shape.append(jax.ShapeDtypeStruct((b, 2, N_HEADS, HEAD_DIM, HEAD_DIM), F32))
    res = pl.pallas_call(
        functools.partial(_hgrn_kernel, layer=layer, has_init=has_init, emit_state=emit_state),
        grid=(b, N_HEADS),
        in_specs=in_specs,
        out_specs=out_specs,
        out_shape=out_shape,
        scratch_shapes=[pltpu.VMEM((2, t, HEAD_DIM), BF16)] * 3
                       + [pltpu.VMEM((2, t, HEAD_DIM), F32), pltpu.VMEM((t, HEAD_DIM), F32),
                          pltpu.VMEM((2, HEAD_DIM, HEAD_DIM), F32)],
        compiler_params=_params(2),
        name="hgrn_mixer",
    )(*args)
    return res[0], (res[1] if emit_state else None)


def _post_kernel(x_ref, o_ref, g_ref, gt1_ref, sh2_ref, sc2_ref, gt2_ref, wo_ref, w1_ref, w2_ref, y_ref):
    mix = jnp.dot(o_ref[...].astype(BF16), wo_ref[...], preferred_element_type=F32)
    x1 = x_ref[...] + gt1_ref[...] * _rms(mix, g_ref[1:2, :])
    hb = (_rms(x1, g_ref[2:3, :]) * (1.0 + sc2_ref[...]) + sh2_ref[...]).astype(BF16)
    ff = jnp.zeros(x1.shape, F32)
    for j in range(D_FF // 1024):
        cols = slice(j * 1024, (j + 1) * 1024)
        hid = jnp.maximum(jnp.dot(hb, w1_ref[:, cols], preferred_element_type=F32), 0.0)
        ff = ff + jnp.dot((hid * hid).astype(BF16), w2_ref[cols, :], preferred_element_type=F32)
    y_ref[...] = x1 + gt2_ref[...] * _rms(ff, g_ref[3:4, :])


def _post_mixer(x2d, o2d, norm_g, mod, tiles_per_seq, w_out, w1, w2):
    rows = x2d.shape[0]
    tile = pl.BlockSpec((ROW_TILE, D_MODEL), lambda i: (i, 0))
    return pl.pallas_call(
        _post_kernel,
        grid=(rows // ROW_TILE,),
        in_specs=[tile, tile, _const_spec((4, D_MODEL)),
                  _mod_spec(2, tiles_per_seq), _mod_spec(3, tiles_per_seq),
                  _mod_spec(4, tiles_per_seq), _mod_spec(5, tiles_per_seq),
                  _const_spec((D_MODEL, D_MODEL)), _const_spec((D_MODEL, D_FF)), _const_spec((D_FF, D_MODEL))],
        out_specs=tile,
        out_shape=jax.ShapeDtypeStruct((rows, D_MODEL), F32),
        compiler_params=_params(1),
        name="post_mixer",
    )(x2d, o2d, norm_g, mod, mod, mod, mod, w_out, w1, w2)


def _trunk(x, mod_rows, per_seq_mod, s_gdn, s_hgrn, grid_conv, emit_state, weights):
    (norm_g, gdn_main, gdn_gate, gdn_alog, gdn_dt, gdn_conv_w, gdn_onorm_g, gdn_w_out,
     hgrn_w_in, hgrn_lb_logits, hgrn_onorm_g, hgrn_w_out, mlp_w1, mlp_w2) = weights
    b, t, _ = x.shape
    tiles_per_seq = t // ROW_TILE if per_seq_mod else None
    x2d = x.reshape(b * t, D_MODEL)
    depth = norm_g.shape[0]
    fin_gdn, fin_hgrn = [], []
    for layer in range(depth):
        j = layer // 2
        mod = mod_rows[layer]
        if layer % 2 == 0:
            proj, gates = _gdn_inproj(x2d, norm_g[layer], mod, tiles_per_seq, gdn_main[j], gdn_gate[j],
                                      gdn_alog[j], gdn_dt[j])
            o, fin = _gdn_mixer(proj.reshape(b, t, GDN_MAIN), gates.reshape(b, t, GATE_LANES), gdn_conv_w[j],
                                gdn_onorm_g[j], s_gdn, j, grid_conv, emit_state)
            fin_gdn.append(fin)
            w_out = gdn_w_out[j]
        else:
            proj = _hgrn_inproj(x2d, norm_g[layer], mod, tiles_per_seq, hgrn_w_in[j])
            o, fin = _hgrn_mixer(proj.reshape(b, t, HGRN_PROJ), hgrn_lb_logits, hgrn_onorm_g[j], s_hgrn, layer, j,
                                 emit_state)
            fin_hgrn.append(fin)
            w_out = hgrn_w_out[j]
        x2d = _post_mixer(x2d, o.reshape(b * t, KEY_DIM), norm_g[layer], mod, tiles_per_seq, w_out,
                          mlp_w1[layer], mlp_w2[layer])
    y = x2d.reshape(b, t, D_MODEL)
    if emit_state:
        return y, jnp.stack(fin_gdn, axis=1), jnp.stack(fin_hgrn, axis=1)
    return y, None, None


def kernel(x_prompt, x_sample, state_gdn, state_hgrn, c, c_ctx, w_ada, b_ada, norm_g, gdn_w_in, gdn_conv_w,
           gdn_a_log, gdn_dt_bias, gdn_onorm_g, gdn_w_out, hgrn_w_in, hgrn_lb_logits, hgrn_onorm_g, hgrn_w_out,
           mlp_w1, mlp_w2):
    n_dec = c.shape[0]
    n_rows = 16
    cond = jnp.concatenate([c_ctx[None, :], c, jnp.zeros((n_rows - 1 - n_dec, D_MODEL), F32)], axis=0)
    mod = _ada_mod(cond, w_ada, b_ada)
    mod_ctx = mod[:, 0:1, None, :]
    mod_smp = mod[:, 1:1 + n_dec, None, :]

    n_gdn = gdn_w_in.shape[0]
    gate_pad = GATE_LANES - 4 * N_HEADS
    gdn_gate = jnp.pad(gdn_w_in[:, :, GDN_MAIN:], ((0, 0), (0, 0), (0, gate_pad))).astype(BF16)
    lead = jnp.zeros((n_gdn, 2 * N_HEADS), F32)
    tail = jnp.zeros((n_gdn, gate_pad), F32)
    gdn_alog = jnp.concatenate([lead, gdn_a_log.reshape(n_gdn, 2 * N_HEADS), tail], axis=1)[:, None, :]
    gdn_dt = jnp.concatenate([lead, gdn_dt_bias.reshape(n_gdn, 2 * N_HEADS), tail], axis=1)[:, None, :]
    weights = (norm_g, gdn_w_in[:, :, :GDN_MAIN].astype(BF16), gdn_gate, gdn_alog, gdn_dt, gdn_conv_w,
               gdn_onorm_g, gdn_w_out.astype(BF16), hgrn_w_in.astype(BF16), hgrn_lb_logits, hgrn_onorm_g,
               hgrn_w_out.astype(BF16), mlp_w1.astype(BF16), mlp_w2.astype(BF16))

    y_prompt, new_gdn, new_hgrn = _trunk(x_prompt, mod_ctx, False, None, None, False, True, weights)
    y_sample, _, _ = _trunk(x_sample, mod_smp, True, state_gdn, state_hgrn, True, False, weights)
    return (y_prompt, y_sample, new_gdn, new_hgrn)
```

```python
import functools

import jax
import jax.numpy as jnp
from jax import lax
from jax.experimental import pallas as pl
from jax.experimental.pallas import tpu as pltpu

D_MODEL = 1024
N_HEADS = 8
HEAD_DIM = 128
KEY_DIM = N_HEADS * HEAD_DIM
CONV_W = 5
GDN_CHUNK = 64
HGRN_CHUNK = 16
GRID_W = 64
D_FF = 4 * D_MODEL
EPS = 1e-6
GDN_MAIN = 4 * KEY_DIM
GATE_LANES = 128
HGRN_PROJ = 5 * KEY_DIM

ROW_TILE = 512
PRE_ROWS = 256
HGRN_BLOCK = 128
HGRN_ATT = 64
GDN_GROUP = 4
VMEM_LIMIT = 56 * 1024 * 1024

BF16 = jnp.bfloat16
F32 = jnp.float32

_NT = (((1,), (1,)), ((), ()))
_TN = (((0,), (0,)), ((), ()))


def _dot(a, b, dims=None):
    a = a.astype(BF16)
    b = b.astype(BF16)
    if dims is None:
        return jnp.dot(a, b, preferred_element_type=F32)
    return lax.dot_general(a, b, dims, preferred_element_type=F32)


def _segment_sums(x, seg):
    rows = x.shape[0]
    pos = lax.broadcasted_iota(jnp.int32, x.shape, 0) & (seg - 1)
    pre, suf = x, x
    s = 1
    while s < seg:
        pre = pre + jnp.where(pos >= s, pltpu.roll(pre, s, axis=0), 0.0)
        suf = suf + jnp.where(pos + s < seg, pltpu.roll(suf, rows - s, axis=0), 0.0)
        s *= 2
    return pre, suf


def _rms(x, g):
    return x * lax.rsqrt(jnp.mean(x * x, axis=-1, keepdims=True) + EPS) * g


def _silu(x):
    return x * jax.nn.sigmoid(x)


def _softplus(x):
    return jnp.maximum(x, 0.0) + jnp.log1p(jnp.exp(-jnp.abs(x)))


def _const_spec(shape):
    zeros = (0,) * len(shape)
    return pl.BlockSpec(shape, lambda *_: zeros, pipeline_mode=pl.Buffered(1))


def _mod_spec(col, tiles_per_seq):
    if tiles_per_seq is None:
        return pl.BlockSpec((None, 1, D_MODEL), lambda i: (0, 0, col))
    return pl.BlockSpec((None, 1, D_MODEL), lambda i: (i // tiles_per_seq, 0, col))


def _params(n_grid):
    return pltpu.CompilerParams(dimension_semantics=("arbitrary",) * n_grid, vmem_limit_bytes=VMEM_LIMIT)


def _ada_kernel(c_ref, w_ref, b_ref, o_ref):
    o_ref[...] = _dot(_silu(c_ref[...]), w_ref[...]) + b_ref[...]


def _ada_mod(cond, w_ada, b_ada):
    depth, _, n = w_ada.shape
    rows = cond.shape[0]
    tn = 1536
    return pl.pallas_call(
        _ada_kernel,
        grid=(depth, n // tn),
        in_specs=[pl.BlockSpec((rows, D_MODEL), lambda l, j: (0, 0)),
                  pl.BlockSpec((None, D_MODEL, tn), lambda l, j: (l, 0, j)),
                  pl.BlockSpec((None, 1, tn), lambda l, j: (l, 0, j))],
        out_specs=pl.BlockSpec((None, rows, tn), lambda l, j: (l, 0, j)),
        out_shape=jax.ShapeDtypeStruct((depth, rows, n), F32),
        compiler_params=_params(2),
        name="ada_mod",
    )(cond, w_ada, b_ada.reshape(depth, 1, n))


def _modulated(x_ref, g_ref, sh_ref, sc_ref):
    return (_rms(x_ref[...], g_ref[0:1, :]) * (1.0 + sc_ref[...]) + sh_ref[...]).astype(BF16)


def _gdn_inproj_kernel(x_ref, g_ref, sh_ref, sc_ref, w_ref, wg_ref, alog_ref, dt_ref, proj_ref, gate_ref):
    hb = _modulated(x_ref, g_ref, sh_ref, sc_ref)
    for j in range(GDN_MAIN // 512):
        cols = slice(j * 512, (j + 1) * 512)
        proj_ref[:, cols] = jnp.dot(hb, w_ref[:, cols], preferred_element_type=F32)
    raw = jnp.dot(hb, wg_ref[...], preferred_element_type=F32)
    beta = jax.nn.sigmoid(raw)
    g = -jnp.exp(alog_ref[...]) * _softplus(raw + dt_ref[...])
    gp, gs = _segment_sums(g, GDN_CHUNK)
    lane = lax.broadcasted_iota(jnp.int32, raw.shape, 1)
    gate_ref[...] = jnp.where(lane < 2 * N_HEADS, beta,
                              jnp.where(lane < 3 * N_HEADS, gp,
                                        jnp.where(lane < 4 * N_HEADS, gs, 0.0)))


def _gdn_inproj(x2d, norm_g, mod, tiles_per_seq, w_main, w_gate, alog_row, dt_row):
    rows = x2d.shape[0]
    return pl.pallas_call(
        _gdn_inproj_kernel,
        grid=(rows // ROW_TILE,),
        in_specs=[pl.BlockSpec((ROW_TILE, D_MODEL), lambda i: (i, 0)),
                  _const_spec((4, D_MODEL)),
                  _mod_spec(0, tiles_per_seq), _mod_spec(1, tiles_per_seq),
                  _const_spec((D_MODEL, GDN_MAIN)), _const_spec((D_MODEL, GATE_LANES)),
                  _const_spec((1, GATE_LANES)), _const_spec((1, GATE_LANES))],
        out_specs=[pl.BlockSpec((ROW_TILE, GDN_MAIN), lambda i: (i, 0)),
                   pl.BlockSpec((ROW_TILE, GATE_LANES), lambda i: (i, 0))],
        out_shape=[jax.ShapeDtypeStruct((rows, GDN_MAIN), F32),
                   jax.ShapeDtypeStruct((rows, GATE_LANES), F32)],
        compiler_params=_params(1),
        name="gdn_inproj",
    )(x2d, norm_g, mod, mod, w_main, w_gate, alog_row, dt_row)


def _hgrn_inproj_kernel(x_ref, g_ref, sh_ref, sc_ref, w_ref, proj_ref):
    hb = _modulated(x_ref, g_ref, sh_ref, sc_ref)
    for j in range(HGRN_PROJ // 512):
        cols = slice(j * 512, (j + 1) * 512)
        proj_ref[:, cols] = jnp.dot(hb, w_ref[:, cols], preferred_element_type=F32)


def _hgrn_inproj(x2d, norm_g, mod, tiles_per_seq, w_in):
    rows = x2d.shape[0]
    return pl.pallas_call(
        _hgrn_inproj_kernel,
        grid=(rows // ROW_TILE,),
        in_specs=[pl.BlockSpec((ROW_TILE, D_MODEL), lambda i: (i, 0)),
                  _const_spec((4, D_MODEL)),
                  _mod_spec(0, tiles_per_seq), _mod_spec(1, tiles_per_seq),
                  _const_spec((D_MODEL, HGRN_PROJ))],
        out_specs=pl.BlockSpec((ROW_TILE, HGRN_PROJ), lambda i: (i, 0)),
        out_shape=jax.ShapeDtypeStruct((rows, HGRN_PROJ), F32),
        compiler_params=_params(1),
        name="hgrn_inproj",
    )(x2d, norm_g, mod, mod, w_in)


def _split_bf16(x):
    hi = lax.bitcast_convert_type(lax.bitcast_convert_type(x, jnp.int32) & jnp.int32(-65536), F32)
    return hi, x - hi


def _pair_inverses(lows, left, diag2):
    c = lows[0].shape[0]

    def block_diag(x):
        return jnp.concatenate([jnp.where(left, x, 0.0), jnp.where(left, 0.0, x)], axis=0)

    def left_operand(hi, lo):
        return jnp.concatenate([hi.astype(BF16), lo.astype(BF16)] * 2, axis=1)

    def right_operand(hi, lo):
        bh, bl = block_diag(hi).astype(BF16), block_diag(lo).astype(BF16)
        return jnp.concatenate([bh, bh, bl, bl], axis=0)

    accs = [jnp.where(diag2, 1.0, 0.0) - x for x in lows]
    parts = [_split_bf16(x) for x in lows]
    powers = [jnp.dot(left_operand(hi, lo), right_operand(hi, lo), preferred_element_type=F32) for hi, lo in parts]
    levels = c.bit_length() - 2
    for level in range(levels):
        parts = [_split_bf16(x) for x in powers]
        rhs = [right_operand(hi, lo) for hi, lo in parts]
        acc_lhs = [left_operand(*_split_bf16(a)) for a in accs]
        if level + 1 < levels:
            res = [jnp.dot(jnp.concatenate([left_operand(hi, lo), al], axis=0), r, preferred_element_type=F32)
                   for (hi, lo), al, r in zip(parts, acc_lhs, rhs)]
            powers = [x[:c] for x in res]
            accs = [a + x[c:] for a, x in zip(accs, res)]
        else:
            accs = [a + jnp.dot(al, r, preferred_element_type=F32) for a, al, r in zip(accs, acc_lhs, rhs)]
    return accs


def _gdn_kernel(q_ref, k_ref, v_ref, z_ref, gcol_ref, grow_ref, cwq_ref, cwk_ref, cwv_ref, on_ref, *rest,
                seg, has_init, emit_state, group):
    rest = list(rest)
    s0_ref = rest.pop(0) if has_init else None
    o_ref = rest.pop(0)
    sfin_ref = rest.pop(0) if emit_state else None
    qn, kn, vn, ob, m_s, b_s, q_s, gl_s, st = rest
    t_len = q_ref.shape[0]
    c = GDN_CHUNK
    n_chunks = t_len // c
    h = pl.program_id(1)

    pos = lax.broadcasted_iota(jnp.int32, (PRE_ROWS, HEAD_DIM), 0) & (seg - 1)

    def conv_silu(x, w_ref):
        acc = x * w_ref[CONV_W // 2:CONV_W // 2 + 1, :]
        for s in (-2, -1, 1, 2):
            shifted = pltpu.roll(x, (-s) % PRE_ROWS, axis=0)
            valid = (pos + s >= 0) & (pos + s < seg)
            acc = acc + jnp.where(valid, shifted, 0.0) * w_ref[s + 2:s + 3, :]
        return _silu(acc)

    def unit(x):
        return x * lax.rsqrt(jnp.sum(x * x, axis=-1, keepdims=True) + EPS)

    def pre(i, carry):
        rows = pl.ds(pl.multiple_of(i * PRE_ROWS, PRE_ROWS), PRE_ROWS)
        qn[rows, :] = unit(conv_silu(q_ref[rows, :], cwq_ref)) * (HEAD_DIM ** -0.5)
        kn[rows, :] = unit(conv_silu(k_ref[rows, :], cwk_ref))
        vn[rows, :] = conv_silu(v_ref[rows, :], cwv_ref)
        return carry

    lax.fori_loop(0, t_len // PRE_ROWS, pre, 0)

    lane = lax.broadcasted_iota(jnp.int32, (c, 2 * c), 1)
    row = lax.broadcasted_iota(jnp.int32, (c, 2 * c), 0)
    left = lane < c
    ahead = jnp.where(left, row - lane, lane - c - row)

    def column(tile, idx):
        return jnp.sum(jnp.where(lane == idx, tile, 0.0), axis=-1, keepdims=True)

    zb = jnp.zeros((c, 2 * HEAD_DIM), BF16)

    def direction_blocks(a0, a1):
        return jnp.concatenate([jnp.concatenate([a0.astype(BF16), zb], axis=1),
                                jnp.concatenate([zb, a1.astype(BF16)], axis=1)], axis=0)

    def load(cidx):
        rows = pl.ds(pl.multiple_of(cidx * c, c), c)
        gtile = gcol_ref[rows, :]
        e = dict(cidx=cidx, rows=rows, q=qn[rows, :], k=kn[rows, :], v=vn[rows, :])
        e["beta"] = [column(gtile, d * N_HEADS + h) for d in range(2)]
        e["gc"] = [column(gtile, (2 + d) * N_HEADS + h) for d in range(2)]
        e["g_last"] = [e["gc"][0][c - 1:c, :], e["gc"][1][0:1, :]]
        gr2 = grow_ref[cidx, pl.ds(h, 1), :]
        e["decay2"] = jnp.exp(jnp.where(ahead >= 0, jnp.where(left, e["gc"][0], e["gc"][1]) - gr2, -jnp.inf))
        return e

    def gram(e):
        k = e["k"]
        kq = _dot(jnp.concatenate([k, e["q"]], axis=0), jnp.concatenate([k, k], axis=0), _NT)
        e["low2"] = jnp.where(ahead > 0, kq[:c] * jnp.where(left, e["beta"][0], e["beta"][1]) * e["decay2"], 0.0)
        e["qk2"] = kq[c:] * e["decay2"]

    def solve(e, t2):
        k, v, beta, gc = e["k"], e["v"], e["beta"], e["gc"]
        e["eg"] = [jnp.exp(gc[d]) for d in range(2)]
        rhs = [jnp.concatenate([v * beta[d], k * beta[d] * e["eg"][d]], axis=1) for d in range(2)]
        e["uw"] = jnp.dot(t2.astype(BF16), direction_blocks(*rhs), preferred_element_type=F32)

    def fold(e):
        uw = e["uw"]
        r2 = direction_blocks(uw[:, :2 * HEAD_DIM], uw[:, 2 * HEAD_DIM:])
        e["oq"] = jnp.dot(e["qk2"].astype(BF16), r2, preferred_element_type=F32)
        kt = jnp.concatenate([e["k"] * jnp.exp(e["g_last"][d] - e["gc"][d]) for d in range(2)], axis=0)
        e["bm"] = _dot(kt, r2, _TN)

    def store(e):
        cidx, rows, oq, bm = e["cidx"], e["rows"], e["oq"], e["bm"]
        for d in range(2):
            base = 2 * d * HEAD_DIM
            out_ref = o_ref if d == 0 else ob
            out_ref[rows, :] = oq[:, base:base + HEAD_DIM]
            q_s[cidx, d] = (e["q"] * e["eg"][d] - oq[:, base + HEAD_DIM:base + 2 * HEAD_DIM]).astype(BF16)
            b_s[cidx, d] = bm[:, base:base + HEAD_DIM]
            m_s[cidx, d] = bm[:, base + HEAD_DIM:base + 2 * HEAD_DIM].astype(BF16)
            gl_s[cidx, d] = jnp.broadcast_to(jnp.exp(e["g_last"][d]), (1, HEAD_DIM))

    def prep_body(i, carry):
        chunks = [load(i * group + j) for j in range(group)]
        for e in chunks:
            gram(e)
        inverses = _pair_inverses([e["low2"] for e in chunks], left, ahead == 0)
        for e, t2 in zip(chunks, inverses):
            solve(e, t2)
        for e in chunks:
            fold(e)
        for e in chunks:
            store(e)
        return carry

    lax.fori_loop(0, n_chunks // group, prep_body, 0)

    for d in range(2):
        st[d] = s0_ref[d] if has_init else jnp.zeros((HEAD_DIM, HEAD_DIM), F32)

    def scan_body(n, carry):
        for d in range(2):
            cidx = n if d == 0 else n_chunks - 1 - n
            rows = pl.ds(pl.multiple_of(cidx * c, c), c)
            state = st[d]
            ms = jnp.dot(jnp.concatenate([m_s[cidx, d], q_s[cidx, d]], axis=0), state.astype(BF16),
                         preferred_element_type=F32)
            out_ref = o_ref if d == 0 else ob
            out_ref[rows, :] = out_ref[rows, :] + ms[HEAD_DIM:]
            st[d] = state * gl_s[cidx, d] - ms[:HEAD_DIM] + b_s[cidx, d]
        return carry

    lax.fori_loop(0, n_chunks, scan_body, 0)

    def post(i, carry):
        rows = pl.ds(pl.multiple_of(i * PRE_ROWS, PRE_ROWS), PRE_ROWS)
        o_ref[rows, :] = _rms(o_ref[rows, :] + ob[rows, :], on_ref[...]) * _silu(z_ref[rows, :])
        return carry

    lax.fori_loop(0, t_len // PRE_ROWS, post, 0)
    if emit_state:
        for d in range(2):
            sfin_ref[d] = st[d]


def _gdn_mixer(proj, gates, conv_w, onorm_g, s0, layer_idx, grid_conv, emit_state):
    b, t, _ = proj.shape
    c = GDN_CHUNK
    n_chunks = t // c
    seg = GRID_W if grid_conv else t
    assert PRE_ROWS % seg == 0 and t % PRE_ROWS == 0 and 2 * c == HEAD_DIM
    as_rows = lambda g: g.reshape(b, n_chunks, c, N_HEADS).transpose(0, 1, 3, 2)
    grow = jnp.concatenate([as_rows(gates[:, :, 2 * N_HEADS:3 * N_HEADS]),
                            as_rows(gates[:, :, 3 * N_HEADS:4 * N_HEADS])], axis=-1)
    has_init = s0 is not None
    col = lambda off: pl.BlockSpec((None, t, HEAD_DIM), lambda i, h: (i, 0, off + h))
    cw = lambda off: pl.BlockSpec((CONV_W, HEAD_DIM), lambda i, h: (0, off + h))
    in_specs = [col(0), col(N_HEADS), col(2 * N_HEADS), col(3 * N_HEADS),
                pl.BlockSpec((None, t, GATE_LANES), lambda i, h: (i, 0, 0)),
                pl.BlockSpec((None, n_chunks, N_HEADS, 2 * c), lambda i, h: (i, 0, 0, 0)),
                cw(0), cw(N_HEADS), cw(2 * N_HEADS),
                pl.BlockSpec((1, HEAD_DIM), lambda i, h: (0, 0))]
    args = [proj, proj, proj, proj, gates, grow, conv_w, conv_w, conv_w, onorm_g.reshape(1, HEAD_DIM)]
    if has_init:
        in_specs.append(pl.BlockSpec((None, None, 2, None, HEAD_DIM, HEAD_DIM),
                                     lambda i, h: (i, layer_idx, 0, h, 0, 0)))
        args.append(s0)
    out_specs = [pl.BlockSpec((None, t, HEAD_DIM), lambda i, h: (i, 0, h))]
    out_shape = [jax.ShapeDtypeStruct((b, t, KEY_DIM), F32)]
    if emit_state:
        out_specs.append(pl.BlockSpec((None, 2, None, HEAD_DIM, HEAD_DIM), lambda i, h: (i, 0, h, 0, 0)))
        out_shape.append(jax.ShapeDtypeStruct((b, 2, N_HEADS, HEAD_DIM, HEAD_DIM), F32))
    scratch = [pltpu.VMEM((t, HEAD_DIM), F32)] * 4 + [
        pltpu.VMEM((n_chunks, 2, HEAD_DIM, HEAD_DIM), BF16),
        pltpu.VMEM((n_chunks, 2, HEAD_DIM, HEAD_DIM), F32),
        pltpu.VMEM((n_chunks, 2, c, HEAD_DIM), BF16),
        pltpu.VMEM((n_chunks, 2, 1, HEAD_DIM), F32),
        pltpu.VMEM((2, HEAD_DIM, HEAD_DIM), F32)]
    res = pl.pallas_call(
        functools.partial(_gdn_kernel, seg=seg, has_init=has_init, emit_state=emit_state, group=GDN_GROUP),
        grid=(b, N_HEADS),
        in_specs=in_specs,
        out_specs=out_specs,
        out_shape=out_shape,
        scratch_shapes=scratch,
        compiler_params=_params(2),
        name="gdn_mixer",
    )(*args)
    return res[0], (res[1] if emit_state else None)


def _hgrn_kernel(q_ref, ff_ref, fb_ref, i_ref, z_ref, lbl_ref, on_ref, *rest, layer, has_init, emit_state):
    rest = list(rest)
    s0_ref = rest.pop(0) if has_init else None
    o_ref = rest.pop(0)
    sfin_ref = rest.pop(0) if emit_state else None
    qin, kout, ktail, ftot, ob, st = rest
    t_len = q_ref.shape[0]
    c = HGRN_CHUNK
    n_chunks = t_len // c

    depth = lbl_ref.shape[0]
    logits = [lbl_ref[l] for l in range(depth)]
    top = functools.reduce(jnp.maximum, logits)
    ex = [jnp.exp(l - top) for l in logits]
    denom = functools.reduce(lambda x, y: x + y, ex)
    lb = functools.reduce(lambda x, y: x + y, ex[:layer + 1]) / denom - ex[0] / denom

    def pre(i, carry):
        rows = pl.ds(pl.multiple_of(i * PRE_ROWS, PRE_ROWS), PRE_ROWS)
        q = _silu(q_ref[rows, :])
        for d, f_ref in enumerate((ff_ref, fb_ref)):
            lbd = lb[d:d + 1, :]
            fg = lbd + (1.0 - lbd) * jax.nn.sigmoid(f_ref[rows, :])
            k = 1.0 - fg
            logf = jnp.log(fg)
            psum, ssum = _segment_sums(logf, c)
            bcum = psum if d == 0 else ssum
            rest_sum = (ssum if d == 0 else psum) - logf
            qin[d, rows, :] = (q * jnp.exp(bcum)).astype(BF16)
            kout[d, rows, :] = (k * jnp.exp(-bcum)).astype(BF16)
            ktail[d, rows, :] = (k * jnp.exp(rest_sum)).astype(BF16)
            ftot[d, rows, :] = jnp.exp(psum + ssum - logf)
        return carry

    lax.fori_loop(0, t_len // PRE_ROWS, pre, 0)

    for d in range(2):
        st[d] = s0_ref[d].T if has_init else jnp.zeros((HEAD_DIM, HEAD_DIM), F32)

    blk = HGRN_BLOCK
    per_blk = blk // c
    ri = lax.broadcasted_iota(jnp.int32, (HGRN_ATT, HGRN_ATT), 0)
    ci = lax.broadcasted_iota(jnp.int32, (HGRN_ATT, HGRN_ATT), 1)
    same_chunk = (ri & -c) == (ci & -c)
    att_mask = [same_chunk & (ci <= ri), same_chunk & (ci >= ri)]

    def body(i, carry):
        ctx = []
        for d in range(2):
            r0 = pl.multiple_of((i if d == 0 else t_len // blk - 1 - i) * blk, blk)
            rows = pl.ds(r0, blk)
            ctx.append(dict(r0=r0, rows=rows, qi=qin[d, rows, :], ko=kout[d, rows, :], kt=ktail[d, rows, :],
                            v=i_ref[rows, :].astype(BF16)))
        groups = [slice(g * HGRN_ATT, (g + 1) * HGRN_ATT) for g in range(blk // HGRN_ATT)]
        for e in ctx:
            e["att"] = [_dot(e["qi"][sl], e["ko"][sl], _NT) for sl in groups]
        for e in ctx:
            e["ds"] = [_dot(e["v"][j * c:(j + 1) * c], e["kt"][j * c:(j + 1) * c], _TN) for j in range(per_blk)]
        for d, e in enumerate(ctx):
            e["intra"] = jnp.concatenate(
                [_dot(jnp.where(att_mask[d], a, 0.0), e["v"][sl]) for a, sl in zip(e["att"], groups)], axis=0)
            e["state"] = st[d]
            e["inter"] = [None] * per_blk
        for step in range(per_blk):
            for d, e in enumerate(ctx):
                j = step if d == 0 else per_blk - 1 - step
                e["inter"][j] = _dot(e["qi"][j * c:(j + 1) * c], e["state"], _NT)
                f_last = ftot[d, pl.ds(e["r0"] + j * c, 1), :]
                e["state"] = e["state"] * f_last + e["ds"][j]
        for d, e in enumerate(ctx):
            st[d] = e["state"]
            out_ref = o_ref if d == 0 else ob
            out_ref[e["rows"], :] = e["intra"] + jnp.concatenate(e["inter"], axis=0)
        return carry

    lax.fori_loop(0, t_len // blk, body, 0)

    def post(i, carry):
        rows = pl.ds(pl.multiple_of(i * PRE_ROWS, PRE_ROWS), PRE_ROWS)
        o_ref[rows, :] = _rms(o_ref[rows, :] + ob[rows, :], on_ref[...]) * _silu(z_ref[rows, :])
        return carry

    lax.fori_loop(0, t_len // PRE_ROWS, post, 0)
    if emit_state:
        for d in range(2):
            sfin_ref[d] = st[d].T


def _hgrn_mixer(proj, lb_logits, onorm_g, s0, layer, layer_idx, emit_state):
    b, t, _ = proj.shape
    depth = lb_logits.shape[0]
    has_init = s0 is not None
    col = lambda off: pl.BlockSpec((None, t, HEAD_DIM), lambda i, h: (i, 0, off + h))
    in_specs = [col(0), col(N_HEADS), col(2 * N_HEADS), col(3 * N_HEADS), col(4 * N_HEADS),
                pl.BlockSpec((depth, 2, HEAD_DIM), lambda i, h: (0, 0, h)),
                pl.BlockSpec((1, HEAD_DIM), lambda i, h: (0, 0))]
    args = [proj, proj, proj, proj, proj, lb_logits, onorm_g.reshape(1, HEAD_DIM)]
    if has_init:
        in_specs.append(pl.BlockSpec((None, None, 2, None, HEAD_DIM, HEAD_DIM),
                                     lambda i, h: (i, layer_idx, 0, h, 0, 0)))
        args.append(s0)
    out_specs = [pl.BlockSpec((None, t, HEAD_DIM), lambda i, h: (i, 0, h))]
    out_shape = [jax.ShapeDtypeStruct((b, t, KEY_DIM), F32)]
    if emit_state:
        out_specs.append(pl.BlockSpec((None, 2, None, HEAD_DIM, HEAD_DIM), lambda i, h: (i, 0, h, 0, 0)))
        out_shape.append(jax.ShapeDtypeStruct((b, 2, N_HEADS, HEAD_DIM, HEAD_DIM), F32))
    res = pl.pallas_call(
        functools.partial(_hgrn_kernel, layer=layer, has_init=has_init, emit_state=emit_state),
        grid=(b, N_HEADS),
        in_specs=in_specs,
        out_specs=out_specs,
        out_shape=out_shape,
        scratch_shapes=[pltpu.VMEM((2, t, HEAD_DIM), BF16)] * 3
                       + [pltpu.VMEM((2, t, HEAD_DIM), F32), pltpu.VMEM((t, HEAD_DIM), F32),
                          pltpu.VMEM((2, HEAD_DIM, HEAD_DIM), F32)],
        compiler_params=_params(2),
        name="hgrn_mixer",
    )(*args)
    return res[0], (res[1] if emit_state else None)


def _post_kernel(x_ref, o_ref, g_ref, gt1_ref, sh2_ref, sc2_ref, gt2_ref, wo_ref, w1_ref, w2_ref, y_ref):
    mix = jnp.dot(o_ref[...].astype(BF16), wo_ref[...], preferred_element_type=F32)
    x1 = x_ref[...] + gt1_ref[...] * _rms(mix, g_ref[1:2, :])
    hb = (_rms(x1, g_ref[2:3, :]) * (1.0 + sc2_ref[...]) + sh2_ref[...]).astype(BF16)
    ff = jnp.zeros(x1.shape, F32)
    for j in range(D_FF // 1024):
        cols = slice(j * 1024, (j + 1) * 1024)
        hid = jnp.maximum(jnp.dot(hb, w1_ref[:, cols], preferred_element_type=F32), 0.0)
        ff = ff + jnp.dot((hid * hid).astype(BF16), w2_ref[cols, :], preferred_element_type=F32)
    y_ref[...] = x1 + gt2_ref[...] * _rms(ff, g_ref[3:4, :])


def _post_mixer(x2d, o2d, norm_g, mod, tiles_per_seq, w_out, w1, w2):
    rows = x2d.shape[0]
    tile = pl.BlockSpec((ROW_TILE, D_MODEL), lambda i: (i, 0))
    return pl.pallas_call(
        _post_kernel,
        grid=(rows // ROW_TILE,),
        in_specs=[tile, tile, _const_spec((4, D_MODEL)),
                  _mod_spec(2, tiles_per_seq), _mod_spec(3, tiles_per_seq),
                  _mod_spec(4, tiles_per_seq), _mod_spec(5, tiles_per_seq),
                  _const_spec((D_MODEL, D_MODEL)), _const_spec((D_MODEL, D_FF)), _const_spec((D_FF, D_MODEL))],
        out_specs=tile,
        out_shape=jax.ShapeDtypeStruct((rows, D_MODEL), F32),
        compiler_params=_params(1),
        name="post_mixer",
    )(x2d, o2d, norm_g, mod, mod, mod, mod, w_out, w1, w2)


def _trunk(x, mod_rows, per_seq_mod, s_gdn, s_hgrn, grid_conv, emit_state, weights):
    (norm_g, gdn_main, gdn_gate, gdn_alog, gdn_dt, gdn_conv_w, gdn_onorm_g, gdn_w_out,
     hgrn_w_in, hgrn_lb_logits, hgrn_onorm_g, hgrn_w_out, mlp_w1, mlp_w2) = weights
    b, t, _ = x.shape
    tiles_per_seq = t // ROW_TILE if per_seq_mod else None
    x2d = x.reshape(b * t, D_MODEL)
    depth = norm_g.shape[0]
    fin_gdn, fin_hgrn = [], []
    for layer in range(depth):
        j = layer // 2
        mod = mod_rows[layer]
        if layer % 2 == 0:
            proj, gates = _gdn_inproj(x2d, norm_g[layer], mod, tiles_per_seq, gdn_main[j], gdn_gate[j],
                                      gdn_alog[j], gdn_dt[j])
            o, fin = _gdn_mixer(proj.reshape(b, t, GDN_MAIN), gates.reshape(b, t, GATE_LANES), gdn_conv_w[j],
                                gdn_onorm_g[j], s_gdn, j, grid_conv, emit_state)
            fin_gdn.append(fin)
            w_out = gdn_w_out[j]
        else:
            proj = _hgrn_inproj(x2d, norm_g[layer], mod, tiles_per_seq, hgrn_w_in[j])
            o, fin = _hgrn_mixer(proj.reshape(b, t, HGRN_PROJ), hgrn_lb_logits, hgrn_onorm_g[j], s_hgrn, layer, j,
                                 emit_state)
            fin_hgrn.append(fin)
            w_out = hgrn_w_out[j]
        x2d = _post_mixer(x2d, o.reshape(b * t, KEY_DIM), norm_g[layer], mod, tiles_per_seq, w_out,
                          mlp_w1[layer], mlp_w2[layer])
    y = x2d.reshape(b, t, D_MODEL)
    if emit_state:
        return y, jnp.stack(fin_gdn, axis=1), jnp.stack(fin_hgrn, axis=1)
    return y, None, None


def kernel(x_prompt, x_sample, state_gdn, state_hgrn, c, c_ctx, w_ada, b_ada, norm_g, gdn_w_in, gdn_conv_w,
           gdn_a_log, gdn_dt_bias, gdn_onorm_g, gdn_w_out, hgrn_w_in, hgrn_lb_logits, hgrn_onorm_g, hgrn_w_out,
           mlp_w1, mlp_w2):
    n_dec = c.shape[0]
    n_rows = 16
    cond = jnp.concatenate([c_ctx[None, :], c, jnp.zeros((n_rows - 1 - n_dec, D_MODEL), F32)], axis=0)
    mod = _ada_mod(cond, w_ada, b_ada)
    mod_ctx = mod[:, 0:1, None, :]
    mod_smp = mod[:, 1:1 + n_dec, None, :]

    n_gdn = gdn_w_in.shape[0]
    gate_pad = GATE_LANES - 4 * N_HEADS
    gdn_gate = jnp.pad(gdn_w_in[:, :, GDN_MAIN:], ((0, 0), (0, 0), (0, gate_pad))).astype(BF16)
    lead = jnp.zeros((n_gdn, 2 * N_HEADS), F32)
    tail = jnp.zeros((n_gdn, gate_pad), F32)
    gdn_alog = jnp.concatenate([lead, gdn_a_log.reshape(n_gdn, 2 * N_HEADS), tail], axis=1)[:, None, :]
    gdn_dt = jnp.concatenate([lead, gdn_dt_bias.reshape(n_gdn, 2 * N_HEADS), tail], axis=1)[:, None, :]
    weights = (norm_g, gdn_w_in[:, :, :GDN_MAIN].astype(BF16), gdn_gate, gdn_alog, gdn_dt, gdn_conv_w,
               gdn_onorm_g, gdn_w_out.astype(BF16), hgrn_w_in.astype(BF16), hgrn_lb_logits, hgrn_onorm_g,
               hgrn_w_out.astype(BF16), mlp_w1.astype(BF16), mlp_w2.astype(BF16))

    y_prompt, new_gdn, new_hgrn = _trunk(x_prompt, mod_ctx, False, None, None, False, True, weights)
    y_sample, _, _ = _trunk(x_sample, mod_smp, True, state_gdn, state_hgrn, True, False, weights)
    return (y_prompt, y_sample, new_gdn, new_hgrn)
```

```python
import functools

import jax
import jax.numpy as jnp
from jax import lax
from jax.experimental import pallas as pl
from jax.experimental.pallas import tpu as pltpu

D_MODEL = 1024
N_HEADS = 8
HEAD_DIM = 128
KEY_DIM = N_HEADS * HEAD_DIM
CONV_W = 5
GDN_CHUNK = 64
HGRN_CHUNK = 16
GRID_W = 64
D_FF = 4 * D_MODEL
EPS = 1e-6
GDN_MAIN = 4 * KEY_DIM
GATE_LANES = 128
HGRN_PROJ = 5 * KEY_DIM

ROW_TILE = 512
PRE_ROWS = 256
HGRN_BLOCK = 256
HGRN_GROUP = 64
GDN_GROUP = 8
VMEM_LIMIT = 56 * 1024 * 1024

BF16 = jnp.bfloat16
F32 = jnp.float32

_NT = (((1,), (1,)), ((), ()))
_TN = (((0,), (0,)), ((), ()))


def _dot(a, b, dims=None):
    a = a.astype(BF16)
    b = b.astype(BF16)
    if dims is None:
        return jnp.dot(a, b, preferred_element_type=F32)
    return lax.dot_general(a, b, dims, preferred_element_type=F32)


def _segment_sums(x, seg):
    rows = x.shape[0]
    pos = lax.broadcasted_iota(jnp.int32, x.shape, 0) & (seg - 1)
    pre, suf = x, x
    s = 1
    while s < seg:
        pre = pre + jnp.where(pos >= s, pltpu.roll(pre, s, axis=0), 0.0)
        suf = suf + jnp.where(pos + s < seg, pltpu.roll(suf, rows - s, axis=0), 0.0)
        s *= 2
    return pre, suf


def _rms(x, g):
    return x * lax.rsqrt(jnp.mean(x * x, axis=-1, keepdims=True) + EPS) * g


def _silu(x):
    return x * jax.nn.sigmoid(x)


def _softplus(x):
    return jnp.maximum(x, 0.0) + jnp.log1p(jnp.exp(-jnp.abs(x)))


def _const_spec(shape):
    zeros = (0,) * len(shape)
    return pl.BlockSpec(shape, lambda *_: zeros, pipeline_mode=pl.Buffered(1))


def _mod_spec(col, tiles_per_seq):
    if tiles_per_seq is None:
        return pl.BlockSpec((None, 1, D_MODEL), lambda i: (0, 0, col))
    return pl.BlockSpec((None, 1, D_MODEL), lambda i: (i // tiles_per_seq, 0, col))


def _params(n_grid):
    return pltpu.CompilerParams(dimension_semantics=("arbitrary",) * n_grid, vmem_limit_bytes=VMEM_LIMIT)


def _ada_kernel(c_ref, w_ref, b_ref, o_ref):
    o_ref[...] = _dot(_silu(c_ref[...]), w_ref[...]) + b_ref[...]


def _ada_mod(cond, w_ada, b_ada):
    depth, _, n = w_ada.shape
    rows = cond.shape[0]
    tn = 1536
    return pl.pallas_call(
        _ada_kernel,
        grid=(depth, n // tn),
        in_specs=[pl.BlockSpec((rows, D_MODEL), lambda l, j: (0, 0)),
                  pl.BlockSpec((None, D_MODEL, tn), lambda l, j: (l, 0, j)),
                  pl.BlockSpec((None, 1, tn), lambda l, j: (l, 0, j))],
        out_specs=pl.BlockSpec((None, rows, tn), lambda l, j: (l, 0, j)),
        out_shape=jax.ShapeDtypeStruct((depth, rows, n), F32),
        compiler_params=_params(2),
        name="ada_mod",
    )(cond, w_ada, b_ada.reshape(depth, 1, n))


def _modulated(x_ref, g_ref, sh_ref, sc_ref):
    return (_rms(x_ref[...], g_ref[0:1, :]) * (1.0 + sc_ref[...]) + sh_ref[...]).astype(BF16)


def _gdn_inproj_kernel(x_ref, g_ref, sh_ref, sc_ref, w_ref, wg_ref, alog_ref, dt_ref, proj_ref, gate_ref):
    hb = _modulated(x_ref, g_ref, sh_ref, sc_ref)
    for j in range(GDN_MAIN // 512):
        cols = slice(j * 512, (j + 1) * 512)
        proj_ref[:, cols] = jnp.dot(hb, w_ref[:, cols], preferred_element_type=F32)
    raw = jnp.dot(hb, wg_ref[...], preferred_element_type=F32)
    beta = jax.nn.sigmoid(raw)
    g = -jnp.exp(alog_ref[...]) * _softplus(raw + dt_ref[...])
    gp, gs = _segment_sums(g, GDN_CHUNK)
    lane = lax.broadcasted_iota(jnp.int32, raw.shape, 1)
    gate_ref[...] = jnp.where(lane < 2 * N_HEADS, beta,
                              jnp.where(lane < 3 * N_HEADS, gp,
                                        jnp.where(lane < 4 * N_HEADS, gs, 0.0)))


def _gdn_inproj(x2d, norm_g, mod, tiles_per_seq, w_main, w_gate, alog_row, dt_row):
    rows = x2d.shape[0]
    return pl.pallas_call(
        _gdn_inproj_kernel,
        grid=(rows // ROW_TILE,),
        in_specs=[pl.BlockSpec((ROW_TILE, D_MODEL), lambda i: (i, 0)),
                  _const_spec((4, D_MODEL)),
                  _mod_spec(0, tiles_per_seq), _mod_spec(1, tiles_per_seq),
                  _const_spec((D_MODEL, GDN_MAIN)), _const_spec((D_MODEL, GATE_LANES)),
                  _const_spec((1, GATE_LANES)), _const_spec((1, GATE_LANES))],
        out_specs=[pl.BlockSpec((ROW_TILE, GDN_MAIN), lambda i: (i, 0)),
                   pl.BlockSpec((ROW_TILE, GATE_LANES), lambda i: (i, 0))],
        out_shape=[jax.ShapeDtypeStruct((rows, GDN_MAIN), F32),
                   jax.ShapeDtypeStruct((rows, GATE_LANES), F32)],
        compiler_params=_params(1),
        name="gdn_inproj",
    )(x2d, norm_g, mod, mod, w_main, w_gate, alog_row, dt_row)


def _hgrn_inproj_kernel(x_ref, g_ref, sh_ref, sc_ref, w_ref, proj_ref):
    hb = _modulated(x_ref, g_ref, sh_ref, sc_ref)
    for j in range(HGRN_PROJ // 512):
        cols = slice(j * 512, (j + 1) * 512)
        proj_ref[:, cols] = jnp.dot(hb, w_ref[:, cols], preferred_element_type=F32)


def _hgrn_inproj(x2d, norm_g, mod, tiles_per_seq, w_in):
    rows = x2d.shape[0]
    return pl.pallas_call(
        _hgrn_inproj_kernel,
        grid=(rows // ROW_TILE,),
        in_specs=[pl.BlockSpec((ROW_TILE, D_MODEL), lambda i: (i, 0)),
                  _const_spec((4, D_MODEL)),
                  _mod_spec(0, tiles_per_seq), _mod_spec(1, tiles_per_seq),
                  _const_spec((D_MODEL, HGRN_PROJ))],
        out_specs=pl.BlockSpec((ROW_TILE, HGRN_PROJ), lambda i: (i, 0)),
        out_shape=jax.ShapeDtypeStruct((rows, HGRN_PROJ), F32),
        compiler_params=_params(1),
        name="hgrn_inproj",
    )(x2d, norm_g, mod, mod, w_in)


def _split_bf16(x):
    hi = lax.bitcast_convert_type(lax.bitcast_convert_type(x, jnp.int32) & jnp.int32(-65536), F32)
    return hi, x - hi


def _pair_inverses(lows, left, diag2, fill):
    c = lows[0].shape[0]

    def block_diag(x):
        return jnp.concatenate([jnp.where(left, x, 0.0), jnp.where(left, 0.0, x)], axis=0)

    def left_operand(hi, lo):
        return jnp.concatenate([hi.astype(BF16), lo.astype(BF16)] * 2, axis=1)

    def right_operand(hi, lo):
        bh, bl = block_diag(hi).astype(BF16), block_diag(lo).astype(BF16)
        return jnp.concatenate([bh, bh, bl, bl], axis=0)

    accs = [jnp.where(diag2, 1.0, 0.0) - x for x in lows]
    parts = [_split_bf16(x) for x in lows]
    powers = [jnp.dot(left_operand(hi, lo), right_operand(hi, lo), preferred_element_type=F32) for hi, lo in parts]
    fill(0)
    levels = c.bit_length() - 2
    for level in range(levels):
        parts = [_split_bf16(x) for x in powers]
        rhs = [right_operand(hi, lo) for hi, lo in parts]
        acc_lhs = [left_operand(*_split_bf16(a)) for a in accs]
        if level + 1 < levels:
            res = [jnp.dot(jnp.concatenate([left_operand(hi, lo), al], axis=0), r, preferred_element_type=F32)
                   for (hi, lo), al, r in zip(parts, acc_lhs, rhs)]
            fill(level + 1)
            powers = [x[:c] for x in res]
            accs = [a + x[c:] for a, x in zip(accs, res)]
        else:
            accs = [a + jnp.dot(al, r, preferred_element_type=F32) for a, al, r in zip(accs, acc_lhs, rhs)]
    return accs


def _gdn_kernel(q_ref, k_ref, v_ref, z_ref, gcol_ref, grow_ref, cwq_ref, cwk_ref, cwv_ref, on_ref, *rest,
                seg, has_init, emit_state, group):
    rest = list(rest)
    s0_ref = rest.pop(0) if has_init else None
    o_ref = rest.pop(0)
    sfin_ref = rest.pop(0) if emit_state else None
    qn, kn, vn, ob, m_s, b_s, q_s, gl_s, st = rest
    t_len = q_ref.shape[0]
    c = GDN_CHUNK
    n_chunks = t_len // c
    h = pl.program_id(1)

    pos = lax.broadcasted_iota(jnp.int32, (PRE_ROWS, HEAD_DIM), 0) & (seg - 1)

    def conv_silu(x, w_ref):
        acc = x * w_ref[CONV_W // 2:CONV_W // 2 + 1, :]
        for s in (-2, -1, 1, 2):
            shifted = pltpu.roll(x, (-s) % PRE_ROWS, axis=0)
            valid = (pos + s >= 0) & (pos + s < seg)
            acc = acc + jnp.where(valid, shifted, 0.0) * w_ref[s + 2:s + 3, :]
        return _silu(acc)

    def unit(x):
        return x * lax.rsqrt(jnp.sum(x * x, axis=-1, keepdims=True) + EPS)

    def pre(i, carry):
        rows = pl.ds(pl.multiple_of(i * PRE_ROWS, PRE_ROWS), PRE_ROWS)
        qn[rows, :] = unit(conv_silu(q_ref[rows, :], cwq_ref)) * (HEAD_DIM ** -0.5)
        kn[rows, :] = unit(conv_silu(k_ref[rows, :], cwk_ref))
        vn[rows, :] = conv_silu(v_ref[rows, :], cwv_ref)
        return carry

    lax.fori_loop(0, t_len // PRE_ROWS, pre, 0)

    lane = lax.broadcasted_iota(jnp.int32, (c, 2 * c), 1)
    row = lax.broadcasted_iota(jnp.int32, (c, 2 * c), 0)
    left = lane < c
    ahead = jnp.where(left, row - lane, lane - c - row)
    left_row = lax.broadcasted_iota(jnp.int32, (1, 2 * c), 1) < c

    def column(tile, idx):
        return jnp.sum(jnp.where(lane == idx, tile, 0.0), axis=-1, keepdims=True)

    def direction_blocks(a0, a1):
        z0, z1 = jnp.zeros(a0.shape, BF16), jnp.zeros(a1.shape, BF16)
        return jnp.concatenate([jnp.concatenate([a0.astype(BF16), z1], axis=1),
                                jnp.concatenate([z0, a1.astype(BF16)], axis=1)], axis=0)

    def load(n):
        cidx = [n, n_chunks - 1 - n]
        rows = [pl.ds(pl.multiple_of(ci * c, c), c) for ci in cidx]
        e = dict(n=n, rows=rows, q=[qn[r, :] for r in rows], k=[kn[r, :] for r in rows], v=[vn[r, :] for r in rows])
        gtile = [gcol_ref[r, :] for r in rows]
        e["beta"] = [column(gtile[d], d * N_HEADS + h) for d in range(2)]
        e["gc"] = [column(gtile[d], (2 + d) * N_HEADS + h) for d in range(2)]
        e["g_last"] = [e["gc"][0][c - 1:c, :], e["gc"][1][0:1, :]]
        gr2 = jnp.where(left_row, grow_ref[cidx[0], pl.ds(h, 1), :], grow_ref[cidx[1], pl.ds(h, 1), :])
        e["decay2"] = jnp.exp(jnp.where(ahead >= 0, jnp.where(left, e["gc"][0], e["gc"][1]) - gr2, -jnp.inf))
        return e

    def gram(e):
        k, q = e["k"], e["q"]
        kq = _dot(jnp.concatenate([jnp.concatenate(k, axis=1), jnp.concatenate(q, axis=1)], axis=0),
                  direction_blocks(k[0], k[1]), _NT)
        e["low2"] = jnp.where(ahead > 0, kq[:c] * jnp.where(left, e["beta"][0], e["beta"][1]) * e["decay2"], 0.0)
        e["qk2"] = kq[c:] * e["decay2"]

    def solve(e, t2):
        k, v, beta, gc = e["k"], e["v"], e["beta"], e["gc"]
        e["eg"] = [jnp.exp(gc[d]) for d in range(2)]
        rhs = [jnp.concatenate([v[d] * beta[d], k[d] * beta[d] * e["eg"][d]], axis=1) for d in range(2)]
        e["uw"] = jnp.dot(t2.astype(BF16), direction_blocks(*rhs), preferred_element_type=F32)

    def fold(e):
        uw = e["uw"]
        r2 = direction_blocks(uw[:, :2 * HEAD_DIM], uw[:, 2 * HEAD_DIM:])
        e["oq"] = jnp.dot(e["qk2"].astype(BF16), r2, preferred_element_type=F32)
        kt = jnp.concatenate([e["k"][d] * jnp.exp(e["g_last"][d] - e["gc"][d]) for d in range(2)], axis=0)
        e["bm"] = _dot(kt, r2, _TN)

    def store(e):
        n, oq, bm = e["n"], e["oq"], e["bm"]
        for d in range(2):
            base = 2 * d * HEAD_DIM
            out_ref = o_ref if d == 0 else ob
            out_ref[e["rows"][d], :] = oq[:, base:base + HEAD_DIM]
            q_s[n, d] = (e["q"][d] * e["eg"][d] - oq[:, base + HEAD_DIM:base + 2 * HEAD_DIM]).astype(BF16)
            b_s[n, d] = bm[:, base:base + HEAD_DIM]
            m_s[n, d] = bm[:, base + HEAD_DIM:base + 2 * HEAD_DIM].astype(BF16)
            gl_s[n, d] = jnp.broadcast_to(jnp.exp(e["g_last"][d]), (1, HEAD_DIM))

    def scan_step(n):
        for d in range(2):
            rows = pl.ds(pl.multiple_of((n if d == 0 else n_chunks - 1 - n) * c, c), c)
            state = st[d]
            ms = jnp.dot(jnp.concatenate([m_s[n, d], q_s[n, d]], axis=0), state.astype(BF16),
                         preferred_element_type=F32)
            out_ref = o_ref if d == 0 else ob
            out_ref[rows, :] = out_ref[rows, :] + ms[HEAD_DIM:]
            st[d] = state * gl_s[n, d] - ms[:HEAD_DIM] + b_s[n, d]

    def prepare(first, scan_first):
        pending = [] if scan_first is None else [scan_first + j for j in range(group)]
        n_slots = 8

        def fill(slot):
            for j in range(slot * len(pending) // n_slots, (slot + 1) * len(pending) // n_slots):
                scan_step(pending[j])

        chunks = [load(first + j) for j in range(group)]
        for e in chunks:
            gram(e)
        fill(0)
        inverses = _pair_inverses([e["low2"] for e in chunks], left, ahead == 0, lambda level: fill(1 + level))
        for e, t2 in zip(chunks, inverses):
            solve(e, t2)
        fill(6)
        for e in chunks:
            fold(e)
        fill(7)
        for e in chunks:
            store(e)

    for d in range(2):
        st[d] = s0_ref[d] if has_init else jnp.zeros((HEAD_DIM, HEAD_DIM), F32)

    n_steps = n_chunks // group
    prepare(0, None)

    def step(i, carry):
        prepare(i * group, (i - 1) * group)
        return carry

    lax.fori_loop(1, n_steps, step, 0)
    for j in range(group):
        scan_step((n_steps - 1) * group + j)

    def post(i, carry):
        rows = pl.ds(pl.multiple_of(i * PRE_ROWS, PRE_ROWS), PRE_ROWS)
        o_ref[rows, :] = _rms(o_ref[rows, :] + ob[rows, :], on_ref[...]) * _silu(z_ref[rows, :])
        return carry

    lax.fori_loop(0, t_len // PRE_ROWS, post, 0)
    if emit_state:
        for d in range(2):
            sfin_ref[d] = st[d]


def _gdn_mixer(proj, gates, conv_w, onorm_g, s0, layer_idx, grid_conv, emit_state):
    b, t, _ = proj.shape
    c = GDN_CHUNK
    n_chunks = t // c
    seg = GRID_W if grid_conv else t
    assert PRE_ROWS % seg == 0 and t % PRE_ROWS == 0 and 2 * c == HEAD_DIM
    as_rows = lambda g: g.reshape(b, n_chunks, c, N_HEADS).transpose(0, 1, 3, 2)
    grow = jnp.concatenate([as_rows(gates[:, :, 2 * N_HEADS:3 * N_HEADS]),
                            as_rows(gates[:, :, 3 * N_HEADS:4 * N_HEADS])], axis=-1)
    has_init = s0 is not None
    col = lambda off: pl.BlockSpec((None, t, HEAD_DIM), lambda i, h: (i, 0, off + h))
    cw = lambda off: pl.BlockSpec((CONV_W, HEAD_DIM), lambda i, h: (0, off + h))
    in_specs = [col(0), col(N_HEADS), col(2 * N_HEADS), col(3 * N_HEADS),
                pl.BlockSpec((None, t, GATE_LANES), lambda i, h: (i, 0, 0)),
                pl.BlockSpec((None, n_chunks, N_HEADS, 2 * c), lambda i, h: (i, 0, 0, 0)),
                cw(0), cw(N_HEADS), cw(2 * N_HEADS),
                pl.BlockSpec((1, HEAD_DIM), lambda i, h: (0, 0))]
    args = [proj, proj, proj, proj, gates, grow, conv_w, conv_w, conv_w, onorm_g.reshape(1, HEAD_DIM)]
    if has_init:
        in_specs.append(pl.BlockSpec((None, None, 2, None, HEAD_DIM, HEAD_DIM),
                                     lambda i, h: (i, layer_idx, 0, h, 0, 0)))
        args.append(s0)
    out_specs = [pl.BlockSpec((None, t, HEAD_DIM), lambda i, h: (i, 0, h))]
    out_shape = [jax.ShapeDtypeStruct((b, t, KEY_DIM), F32)]
    if emit_state:
        out_specs.append(pl.BlockSpec((None, 2, None, HEAD_DIM, HEAD_DIM), lambda i, h: (i, 0, h, 0, 0)))
        out_shape.append(jax.ShapeDtypeStruct((b, 2, N_HEADS, HEAD_DIM, HEAD_DIM), F32))
    scratch = [pltpu.VMEM((t, HEAD_DIM), F32)] * 4 + [
        pltpu.VMEM((n_chunks, 2, HEAD_DIM, HEAD_DIM), BF16),
        pltpu.VMEM((n_chunks, 2, HEAD_DIM, HEAD_DIM), F32),
        pltpu.VMEM((n_chunks, 2, c, HEAD_DIM), BF16),
        pltpu.VMEM((n_chunks, 2, 1, HEAD_DIM), F32),
        pltpu.VMEM((2, HEAD_DIM, HEAD_DIM), F32)]
    res = pl.pallas_call(
        functools.partial(_gdn_kernel, seg=seg, has_init=has_init, emit_state=emit_state,
                          group=min(GDN_GROUP, n_chunks)),
        grid=(b, N_HEADS),
        in_specs=in_specs,
        out_specs=out_specs,
        out_shape=out_shape,
        scratch_shapes=scratch,
        compiler_params=_params(2),
        name="gdn_mixer",
    )(*args)
    return res[0], (res[1] if emit_state else None)


def _hgrn_kernel(q_ref, ff_ref, fb_ref, i_ref, z_ref, lbl_ref, on_ref, *rest, layer, has_init, emit_state):
    rest = list(rest)
    s0_ref = rest.pop(0) if has_init else None
    o_ref = rest.pop(0)
    sfin_ref = rest.pop(0) if emit_state else None
    qin, kout, ktail, ftot, ob, st = rest
    t_len = q_ref.shape[0]
    c = HGRN_CHUNK
    n_chunks = t_len // c

    depth = lbl_ref.shape[0]
    logits = [lbl_ref[l] for l in range(depth)]
    top = functools.reduce(jnp.maximum, logits)
    ex = [jnp.exp(l - top) for l in logits]
    denom = functools.reduce(lambda x, y: x + y, ex)
    lb = functools.reduce(lambda x, y: x + y, ex[:layer + 1]) / denom - ex[0] / denom

    def pre(i, carry):
        rows = pl.ds(pl.multiple_of(i * PRE_ROWS, PRE_ROWS), PRE_ROWS)
        q = _silu(q_ref[rows, :])
        for d, f_ref in enumerate((ff_ref, fb_ref)):
            lbd = lb[d:d + 1, :]
            fg = lbd + (1.0 - lbd) * jax.nn.sigmoid(f_ref[rows, :])
            k = 1.0 - fg
            logf = jnp.log(fg)
            psum, ssum = _segment_sums(logf, c)
            bcum = psum if d == 0 else ssum
            rest_sum = (ssum if d == 0 else psum) - logf
            qin[d, rows, :] = q * jnp.exp(bcum)
            kout[d, rows, :] = (k * jnp.exp(-bcum)).astype(BF16)
            ktail[d, rows, :] = k * jnp.exp(rest_sum)
            ftot[d, rows, :] = jnp.exp(psum + ssum - logf)
        return carry

    lax.fori_loop(0, t_len // PRE_ROWS, pre, 0)

    for d in range(2):
        st[d] = s0_ref[d].T if has_init else jnp.zeros((HEAD_DIM, HEAD_DIM), F32)

    grp = HGRN_GROUP
    per = grp // c
    assert per == 4
    blk = HGRN_BLOCK
    ri = lax.broadcasted_iota(jnp.int32, (grp, grp), 0)
    ci = lax.broadcasted_iota(jnp.int32, (grp, grp), 1)
    row_chunk = lax.broadcasted_iota(jnp.int32, (grp, HEAD_DIM), 0) // c

    def group_factors(f, d):
        r1, r2, r3 = (pltpu.roll(f, s * c, axis=0) for s in (1, 2, 3))
        prev, nxt = ((r1, r2, r3), (r3, r2, r1)) if d == 0 else ((r3, r2, r1), (r1, r2, r3))
        order = row_chunk if d == 0 else per - 1 - row_chunk
        g1, h1 = prev[0], nxt[0]
        g2, h2 = g1 * prev[1], h1 * nxt[1]
        g3, h3 = g2 * prev[2], h2 * nxt[2]
        before = jnp.where(order == 0, 1.0, jnp.where(order == 1, g1, jnp.where(order == 2, g2, g3)))
        after = jnp.where(order == 3, 1.0, jnp.where(order == 2, h1, jnp.where(order == 1, h2, h3)))
        whole = f[0:1] * f[c:c + 1] * f[2 * c:2 * c + 1] * f[3 * c:3 * c + 1]
        return g1, g2, before, after, whole

    def att_select(d, p1, p234):
        dist = (ri // c - ci // c) if d == 0 else (ci // c - ri // c)
        inside = (ci <= ri) if d == 0 else (ci >= ri)
        return jnp.where((dist == 0) & inside, p1,
                         jnp.where(dist == 1, p234[:grp],
                                   jnp.where(dist == 2, p234[grp:2 * grp],
                                             jnp.where(dist == 3, p234[2 * grp:], 0.0))))

    def body(i, carry):
        ctx = []
        for step in range(blk // grp):
            for d in range(2):
                g_idx = i * (blk // grp) + step
                r0 = pl.multiple_of((g_idx if d == 0 else t_len // grp - 1 - g_idx) * grp, grp)
                rows = pl.ds(r0, grp)
                e = dict(d=d, rows=rows, qi=qin[d, rows, :], ko=kout[d, rows, :], kt=ktail[d, rows, :],
                         v=i_ref[rows, :].astype(BF16))
                e["g1"], e["g2"], e["before"], e["after"], e["whole"] = group_factors(ftot[d, rows, :], d)
                ctx.append(e)
        for e in ctx:
            qi = e["qi"]
            e["p1"] = _dot(qi, e["ko"], _NT)
            e["p234"] = _dot(jnp.concatenate([qi, qi * e["g1"], qi * e["g2"]], axis=0), e["kt"], _NT)
        for e in ctx:
            e["ds"] = _dot(e["v"], e["kt"] * e["after"], _TN)
        for e in ctx:
            e["intra"] = _dot(att_select(e["d"], e["p1"], e["p234"]), e["v"])
        states = [st[0], st[1]]
        for e in ctx:
            d = e["d"]
            out_ref = o_ref if d == 0 else ob
            out_ref[e["rows"], :] = e["intra"] + _dot(e["qi"] * e["before"], states[d], _NT)
            states[d] = states[d] * e["whole"] + e["ds"]
        st[0], st[1] = states
        return carry

    lax.fori_loop(0, t_len // blk, body, 0)

    def post(i, carry):
        rows = pl.ds(pl.multiple_of(i * PRE_ROWS, PRE_ROWS), PRE_ROWS)
        o_ref[rows, :] = _rms(o_ref[rows, :] + ob[rows, :], on_ref[...]) * _silu(z_ref[rows, :])
        return carry

    lax.fori_loop(0, t_len // PRE_ROWS, post, 0)
    if emit_state:
        for d in range(2):
            sfin_ref[d] = st[d].T


def _hgrn_mixer(proj, lb_logits, onorm_g, s0, layer, layer_idx, emit_state):
    b, t, _ = proj.shape
    depth = lb_logits.shape[0]
    has_init = s0 is not None
    col = lambda off: pl.BlockSpec((None, t, HEAD_DIM), lambda i, h: (i, 0, off + h))
    in_specs = [col(0), col(N_HEADS), col(2 * N_HEADS), col(3 * N_HEADS), col(4 * N_HEADS),
                pl.BlockSpec((depth, 2, HEAD_DIM), lambda i, h: (0, 0, h)),
                pl.BlockSpec((1, HEAD_DIM), lambda i, h: (0, 0))]
    args = [proj, proj, proj, proj, proj, lb_logits, onorm_g.reshape(1, HEAD_DIM)]
    if has_init:
        in_specs.append(pl.BlockSpec((None, None, 2, None, HEAD_DIM, HEAD_DIM),
                                     lambda i, h: (i, layer_idx, 0, h, 0, 0)))
        args.append(s0)
    out_specs = [pl.BlockSpec((None, t, HEAD_DIM), lambda i, h: (i, 0, h))]
    out_shape = [jax.ShapeDtypeStruct((b, t, KEY_DIM), F32)]
    if emit_state:
        out_specs.append(pl.BlockSpec((None, 2, None, HEAD_DIM, HEAD_DIM), lambda i, h: (i, 0, h, 0, 0)))
        out_shape.append(jax.ShapeDtypeStruct((b, 2, N_HEADS, HEAD_DIM, HEAD_DIM), F32))
    res = pl.pallas_call(
        functools.partial(_hgrn_kernel, layer=layer, has_init=has_init, emit_state=emit_state),
        grid=(b, N_HEADS),
        in_specs=in_specs,
        out_specs=out_specs,
        out_shape=out_shape,
        scratch_shapes=[pltpu.VMEM((2, t, HEAD_DIM), F32), pltpu.VMEM((2, t, HEAD_DIM), BF16),
                        pltpu.VMEM((2, t, HEAD_DIM), F32), pltpu.VMEM((2, t, HEAD_DIM), F32),
                        pltpu.VMEM((t, HEAD_DIM), F32), pltpu.VMEM((2, HEAD_DIM, HEAD_DIM), F32)],
        compiler_params=_params(2),
        name="hgrn_mixer",
    )(*args)
    return res[0], (res[1] if emit_state else None)


def _post_kernel(x_ref, o_ref, g_ref, gt1_ref, sh2_ref, sc2_ref, gt2_ref, wo_ref, w1_ref, w2_ref, y_ref):
    mix = jnp.dot(o_ref[...].astype(BF16), wo_ref[...], preferred_element_type=F32)
    x1 = x_ref[...] + gt1_ref[...] * _rms(mix, g_ref[1:2, :])
    hb = (_rms(x1, g_ref[2:3, :]) * (1.0 + sc2_ref[...]) + sh2_ref[...]).astype(BF16)
    ff = jnp.zeros(x1.shape, F32)
    for j in range(D_FF // 1024):
        cols = slice(j * 1024, (j + 1) * 1024)
        hid = jnp.maximum(jnp.dot(hb, w1_ref[:, cols], preferred_element_type=F32), 0.0)
        ff = ff + jnp.dot((hid * hid).astype(BF16), w2_ref[cols, :], preferred_element_type=F32)
    y_ref[...] = x1 + gt2_ref[...] * _rms(ff, g_ref[3:4, :])


def _post_mixer(x2d, o2d, norm_g, mod, tiles_per_seq, w_out, w1, w2):
    rows = x2d.shape[0]
    tile = pl.BlockSpec((ROW_TILE, D_MODEL), lambda i: (i, 0))
    return pl.pallas_call(
        _post_kernel,
        grid=(rows // ROW_TILE,),
        in_specs=[tile, tile, _const_spec((4, D_MODEL)),
                  _mod_spec(2, tiles_per_seq), _mod_spec(3, tiles_per_seq),
                  _mod_spec(4, tiles_per_seq), _mod_spec(5, tiles_per_seq),
                  _const_spec((D_MODEL, D_MODEL)), _const_spec((D_MODEL, D_FF)), _const_spec((D_FF, D_MODEL))],
        out_specs=tile,
        out_shape=jax.ShapeDtypeStruct((rows, D_MODEL), F32),
        compiler_params=_params(1),
        name="post_mixer",
    )(x2d, o2d, norm_g, mod, mod, mod, mod, w_out, w1, w2)


def _trunk(x, mod_rows, per_seq_mod, s_gdn, s_hgrn, grid_conv, emit_state, weights):
    (norm_g, gdn_main, gdn_gate, gdn_alog, gdn_dt, gdn_conv_w, gdn_onorm_g, gdn_w_out,
     hgrn_w_in, hgrn_lb_logits, hgrn_onorm_g, hgrn_w_out, mlp_w1, mlp_w2) = weights
    b, t, _ = x.shape
    tiles_per_seq = t // ROW_TILE if per_seq_mod else None
    x2d = x.reshape(b * t, D_MODEL)
    depth = norm_g.shape[0]
    fin_gdn, fin_hgrn = [], []
    for layer in range(depth):
        j = layer // 2
        mod = mod_rows[layer]
        if layer % 2 == 0:
            proj, gates = _gdn_inproj(x2d, norm_g[layer], mod, tiles_per_seq, gdn_main[j], gdn_gate[j],
                                      gdn_alog[j], gdn_dt[j])
            o, fin = _gdn_mixer(proj.reshape(b, t, GDN_MAIN), gates.reshape(b, t, GATE_LANES), gdn_conv_w[j],
                                gdn_onorm_g[j], s_gdn, j, grid_conv, emit_state)
            fin_gdn.append(fin)
            w_out = gdn_w_out[j]
        else:
            proj = _hgrn_inproj(x2d, norm_g[layer], mod, tiles_per_seq, hgrn_w_in[j])
            o, fin = _hgrn_mixer(proj.reshape(b, t, HGRN_PROJ), hgrn_lb_logits, hgrn_onorm_g[j], s_hgrn, layer, j,
                                 emit_state)
            fin_hgrn.append(fin)
            w_out = hgrn_w_out[j]
        x2d = _post_mixer(x2d, o.reshape(b * t, KEY_DIM), norm_g[layer], mod, tiles_per_seq, w_out,
                          mlp_w1[layer], mlp_w2[layer])
    y = x2d.reshape(b, t, D_MODEL)
    if emit_state:
        return y, jnp.stack(fin_gdn, axis=1), jnp.stack(fin_hgrn, axis=1)
    return y, None, None


def kernel(x_prompt, x_sample, state_gdn, state_hgrn, c, c_ctx, w_ada, b_ada, norm_g, gdn_w_in, gdn_conv_w,
           gdn_a_log, gdn_dt_bias, gdn_onorm_g, gdn_w_out, hgrn_w_in, hgrn_lb_logits, hgrn_onorm_g, hgrn_w_out,
           mlp_w1, mlp_w2):
    n_dec = c.shape[0]
    n_rows = 16
    cond = jnp.concatenate([c_ctx[None, :], c, jnp.zeros((n_rows - 1 - n_dec, D_MODEL), F32)], axis=0)
    mod = _ada_mod(cond, w_ada, b_ada)
    mod_ctx = mod[:, 0:1, None, :]
    mod_smp = mod[:, 1:1 + n_dec, None, :]

    n_gdn = gdn_w_in.shape[0]
    gate_pad = GATE_LANES - 4 * N_HEADS
    gdn_gate = jnp.pad(gdn_w_in[:, :, GDN_MAIN:], ((0, 0), (0, 0), (0, gate_pad))).astype(BF16)
    lead = jnp.zeros((n_gdn, 2 * N_HEADS), F32)
    tail = jnp.zeros((n_gdn, gate_pad), F32)
    gdn_alog = jnp.concatenate([lead, gdn_a_log.reshape(n_gdn, 2 * N_HEADS), tail], axis=1)[:, None, :]
    gdn_dt = jnp.concatenate([lead, gdn_dt_bias.reshape(n_gdn, 2 * N_HEADS), tail], axis=1)[:, None, :]
    weights = (norm_g, gdn_w_in[:, :, :GDN_MAIN].astype(BF16), gdn_gate, gdn_alog, gdn_dt, gdn_conv_w,
               gdn_onorm_g, gdn_w_out.astype(BF16), hgrn_w_in.astype(BF16), hgrn_lb_logits, hgrn_onorm_g,
               hgrn_w_out.astype(BF16), mlp_w1.astype(BF16), mlp_w2.astype(BF16))

    y_prompt, new_gdn, new_hgrn = _trunk(x_prompt, mod_ctx, False, None, None, False, True, weights)
    y_sample, _, _ = _trunk(x_sample, mod_smp, True, state_gdn, state_hgrn, True, False, weights)
    return (y_prompt, y_sample, new_gdn, new_hgrn)
```

```python
import functools

import jax
import jax.numpy as jnp
from jax import lax
from jax.experimental import pallas as pl
from jax.experimental.pallas import tpu as pltpu

D_MODEL = 1024
N_HEADS = 8
HEAD_DIM = 128
KEY_DIM = N_HEADS * HEAD_DIM
CONV_W = 5
GDN_CHUNK = 64
HGRN_CHUNK = 16
GRID_W = 64
D_FF = 4 * D_MODEL
EPS = 1e-6
GDN_MAIN = 4 * KEY_DIM
GATE_LANES = 128
HGRN_PROJ = 5 * KEY_DIM

ROW_TILE = 512
PRE_ROWS = 256
HGRN_BLOCK = 256
HGRN_GROUP = 64
GDN_GROUP = 8
VMEM_LIMIT = 56 * 1024 * 1024

BF16 = jnp.bfloat16
F32 = jnp.float32

_NT = (((1,), (1,)), ((), ()))
_TN = (((0,), (0,)), ((), ()))


def _dot(a, b, dims=None):
    a = a.astype(BF16)
    b = b.astype(BF16)
    if dims is None:
        return jnp.dot(a, b, preferred_element_type=F32)
    return lax.dot_general(a, b, dims, preferred_element_type=F32)


def _segment_sums(x, seg):
    rows = x.shape[0]
    pos = lax.broadcasted_iota(jnp.int32, x.shape, 0) & (seg - 1)
    pre, suf = x, x
    s = 1
    while s < seg:
        pre = pre + jnp.where(pos >= s, pltpu.roll(pre, s, axis=0), 0.0)
        suf = suf + jnp.where(pos + s < seg, pltpu.roll(suf, rows - s, axis=0), 0.0)
        s *= 2
    return pre, suf


def _segment_prefix(x, seg):
    pos = lax.broadcasted_iota(jnp.int32, x.shape, 0) & (seg - 1)
    s = 1
    while s < seg:
        x = x + jnp.where(pos >= s, pltpu.roll(x, s, axis=0), 0.0)
        s *= 2
    return x


def _segment_last(x, seg):
    rows, width = x.shape
    last = x.reshape(rows // seg, seg, width)[:, seg - 1:seg, :]
    return jnp.broadcast_to(last, (rows // seg, seg, width)).reshape(rows, width)


def _rms(x, g):
    return x * lax.rsqrt(jnp.mean(x * x, axis=-1, keepdims=True) + EPS) * g


def _silu(x):
    return x * jax.nn.sigmoid(x)


def _softplus(x):
    return jnp.maximum(x, 0.0) + jnp.log1p(jnp.exp(-jnp.abs(x)))


def _const_spec(shape):
    zeros = (0,) * len(shape)
    return pl.BlockSpec(shape, lambda *_: zeros, pipeline_mode=pl.Buffered(1))


def _mod_spec(col, tiles_per_seq):
    if tiles_per_seq is None:
        return pl.BlockSpec((None, 1, D_MODEL), lambda i: (0, 0, col))
    return pl.BlockSpec((None, 1, D_MODEL), lambda i: (i // tiles_per_seq, 0, col))


def _params(n_grid):
    return pltpu.CompilerParams(dimension_semantics=("arbitrary",) * n_grid, vmem_limit_bytes=VMEM_LIMIT)


def _ada_kernel(c_ref, w_ref, b_ref, o_ref):
    o_ref[...] = _dot(_silu(c_ref[...]), w_ref[...]) + b_ref[...]


def _ada_mod(cond, w_ada, b_ada):
    depth, _, n = w_ada.shape
    rows = cond.shape[0]
    tn = 1536
    return pl.pallas_call(
        _ada_kernel,
        grid=(depth, n // tn),
        in_specs=[pl.BlockSpec((rows, D_MODEL), lambda l, j: (0, 0)),
                  pl.BlockSpec((None, D_MODEL, tn), lambda l, j: (l, 0, j)),
                  pl.BlockSpec((None, 1, tn), lambda l, j: (l, 0, j))],
        out_specs=pl.BlockSpec((None, rows, tn), lambda l, j: (l, 0, j)),
        out_shape=jax.ShapeDtypeStruct((depth, rows, n), F32),
        compiler_params=_params(2),
        name="ada_mod",
    )(cond, w_ada, b_ada.reshape(depth, 1, n))


def _modulated(x_ref, g_ref, sh_ref, sc_ref):
    return (_rms(x_ref[...], g_ref[0:1, :]) * (1.0 + sc_ref[...]) + sh_ref[...]).astype(BF16)


def _gdn_inproj_kernel(x_ref, g_ref, sh_ref, sc_ref, w_ref, wg_ref, alog_ref, dt_ref, cw_ref, proj_ref, gate_ref,
                       *, seg):
    hb = _modulated(x_ref, g_ref, sh_ref, sc_ref)
    n_rows = hb.shape[0]
    chunk = 512
    period = min(seg, PRE_ROWS)
    pos = lax.broadcasted_iota(jnp.int32, (period, HEAD_DIM), 0) & (seg - 1)

    def tap_weights(w):
        taps = {}
        for s in (-2, -1, 1, 2):
            valid = (pos + s >= 0) & (pos + s < seg)
            taps[s] = jnp.tile(jnp.where(valid, w[s + 2:s + 3, :], 0.0), (PRE_ROWS // period, 1))
        return taps

    def unit(x, scale):
        return x * (lax.rsqrt(jnp.sum(x * x, axis=-1, keepdims=True) + EPS) * scale)

    for j in range(GDN_MAIN // chunk):
        y = jnp.dot(hb, w_ref[:, j * chunk:(j + 1) * chunk], preferred_element_type=F32)
        kind = j * chunk // KEY_DIM
        if kind == 3:
            proj_ref[:, j * chunk:(j + 1) * chunk] = _silu(y)
            continue
        for hh in range(chunk // HEAD_DIM):
            c0 = j * chunk + hh * HEAD_DIM
            w = cw_ref[:, c0:c0 + HEAD_DIM]
            taps = tap_weights(w)
            for r in range(n_rows // PRE_ROWS):
                x = y[r * PRE_ROWS:(r + 1) * PRE_ROWS, hh * HEAD_DIM:(hh + 1) * HEAD_DIM]
                acc = x * w[CONV_W // 2:CONV_W // 2 + 1, :]
                for s in (-2, -1, 1, 2):
                    acc = acc + pltpu.roll(x, (-s) % PRE_ROWS, axis=0) * taps[s]
                out = _silu(acc)
                if kind == 0:
                    out = unit(out, HEAD_DIM ** -0.5)
                elif kind == 1:
                    out = unit(out, 1.0)
                proj_ref[r * PRE_ROWS:(r + 1) * PRE_ROWS, c0:c0 + HEAD_DIM] = out
    raw = jnp.dot(hb, wg_ref[...], preferred_element_type=F32)
    beta = jax.nn.sigmoid(raw)
    g = -jnp.exp(alog_ref[...]) * _softplus(raw + dt_ref[...])
    gp, gs = _segment_sums(g, GDN_CHUNK)
    lane = lax.broadcasted_iota(jnp.int32, raw.shape, 1)
    gate_ref[...] = jnp.where(lane < 2 * N_HEADS, beta,
                              jnp.where(lane < 3 * N_HEADS, gp,
                                        jnp.where(lane < 4 * N_HEADS, gs, 0.0)))


def _gdn_inproj(x2d, norm_g, mod, tiles_per_seq, w_main, w_gate, alog_row, dt_row, conv_w, seg):
    rows = x2d.shape[0]
    assert PRE_ROWS % seg == 0 and ROW_TILE % PRE_ROWS == 0
    return pl.pallas_call(
        functools.partial(_gdn_inproj_kernel, seg=seg),
        grid=(rows // ROW_TILE,),
        in_specs=[pl.BlockSpec((ROW_TILE, D_MODEL), lambda i: (i, 0)),
                  _const_spec((4, D_MODEL)),
                  _mod_spec(0, tiles_per_seq), _mod_spec(1, tiles_per_seq),
                  _const_spec((D_MODEL, GDN_MAIN)), _const_spec((D_MODEL, GATE_LANES)),
                  _const_spec((1, GATE_LANES)), _const_spec((1, GATE_LANES)),
                  _const_spec((CONV_W, 3 * KEY_DIM))],
        out_specs=[pl.BlockSpec((ROW_TILE, GDN_MAIN), lambda i: (i, 0)),
                   pl.BlockSpec((ROW_TILE, GATE_LANES), lambda i: (i, 0))],
        out_shape=[jax.ShapeDtypeStruct((rows, GDN_MAIN), F32),
                   jax.ShapeDtypeStruct((rows, GATE_LANES), F32)],
        compiler_params=_params(1),
        name="gdn_inproj",
    )(x2d, norm_g, mod, mod, w_main, w_gate, alog_row, dt_row, conv_w)


def _hgrn_inproj_kernel(x_ref, g_ref, sh_ref, sc_ref, w_ref, proj_ref):
    hb = _modulated(x_ref, g_ref, sh_ref, sc_ref)
    for j in range(HGRN_PROJ // 512):
        cols = slice(j * 512, (j + 1) * 512)
        y = jnp.dot(hb, w_ref[:, cols], preferred_element_type=F32)
        proj_ref[:, cols] = _silu(y) if j * 512 // KEY_DIM in (0, 4) else y


def _hgrn_inproj(x2d, norm_g, mod, tiles_per_seq, w_in):
    rows = x2d.shape[0]
    return pl.pallas_call(
        _hgrn_inproj_kernel,
        grid=(rows // ROW_TILE,),
        in_specs=[pl.BlockSpec((ROW_TILE, D_MODEL), lambda i: (i, 0)),
                  _const_spec((4, D_MODEL)),
                  _mod_spec(0, tiles_per_seq), _mod_spec(1, tiles_per_seq),
                  _const_spec((D_MODEL, HGRN_PROJ))],
        out_specs=pl.BlockSpec((ROW_TILE, HGRN_PROJ), lambda i: (i, 0)),
        out_shape=jax.ShapeDtypeStruct((rows, HGRN_PROJ), F32),
        compiler_params=_params(1),
        name="hgrn_inproj",
    )(x2d, norm_g, mod, mod, w_in)


def _split_bf16(x):
    hi = lax.bitcast_convert_type(lax.bitcast_convert_type(x, jnp.int32) & jnp.int32(-65536), F32)
    return hi, x - hi


def _pair_inverses(lows, left, diag2, fill):
    c = lows[0].shape[0]

    def block_diag(x):
        return jnp.concatenate([jnp.where(left, x, 0.0), jnp.where(left, 0.0, x)], axis=0)

    def left_operand(hi, lo):
        return jnp.concatenate([hi.astype(BF16), lo.astype(BF16)] * 2, axis=1)

    def right_operand(hi, lo):
        bh, bl = block_diag(hi).astype(BF16), block_diag(lo).astype(BF16)
        return jnp.concatenate([bh, bh, bl, bl], axis=0)

    accs = [jnp.where(diag2, 1.0, 0.0) - x for x in lows]
    parts = [_split_bf16(x) for x in lows]
    powers = [jnp.dot(left_operand(hi, lo), right_operand(hi, lo), preferred_element_type=F32) for hi, lo in parts]
    fill(0)
    levels = c.bit_length() - 2
    for level in range(levels):
        parts = [_split_bf16(x) for x in powers]
        rhs = [right_operand(hi, lo) for hi, lo in parts]
        acc_lhs = [left_operand(*_split_bf16(a)) for a in accs]
        if level + 1 < levels:
            res = [jnp.dot(jnp.concatenate([left_operand(hi, lo), al], axis=0), r, preferred_element_type=F32)
                   for (hi, lo), al, r in zip(parts, acc_lhs, rhs)]
            fill(level + 1)
            powers = [x[:c] for x in res]
            accs = [a + x[c:] for a, x in zip(accs, res)]
        else:
            accs = [a + jnp.dot(al, r, preferred_element_type=F32) for a, al, r in zip(accs, acc_lhs, rhs)]
    return accs


def _gdn_kernel(qn, kn, vn, gcol_ref, grow_ref, *rest, has_init, emit_state, group):
    rest = list(rest)
    s0_ref = rest.pop(0) if has_init else None
    o_ref = rest.pop(0)
    sfin_ref = rest.pop(0) if emit_state else None
    ob, m_s, b_s, q_s, gl_s, st = rest
    t_len = qn.shape[0]
    c = GDN_CHUNK
    n_chunks = t_len // c
    h = pl.program_id(1)

    lane = lax.broadcasted_iota(jnp.int32, (c, 2 * c), 1)
    row = lax.broadcasted_iota(jnp.int32, (c, 2 * c), 0)
    left = lane < c
    ahead = jnp.where(left, row - lane, lane - c - row)
    left_row = lax.broadcasted_iota(jnp.int32, (1, 2 * c), 1) < c

    def column(tile, idx):
        return jnp.sum(jnp.where(lane == idx, tile, 0.0), axis=-1, keepdims=True)

    def direction_blocks(a0, a1):
        z0, z1 = jnp.zeros(a0.shape, BF16), jnp.zeros(a1.shape, BF16)
        return jnp.concatenate([jnp.concatenate([a0.astype(BF16), z1], axis=1),
                                jnp.concatenate([z0, a1.astype(BF16)], axis=1)], axis=0)

    def load(n):
        cidx = [n, n_chunks - 1 - n]
        rows = [pl.ds(pl.multiple_of(ci * c, c), c) for ci in cidx]
        e = dict(n=n, rows=rows, q=[qn[r, :] for r in rows], k=[kn[r, :] for r in rows], v=[vn[r, :] for r in rows])
        gtile = [gcol_ref[r, :] for r in rows]
        e["beta"] = [column(gtile[d], d * N_HEADS + h) for d in range(2)]
        e["gc"] = [column(gtile[d], (2 + d) * N_HEADS + h) for d in range(2)]
        e["g_last"] = [e["gc"][0][c - 1:c, :], e["gc"][1][0:1, :]]
        gr2 = jnp.where(left_row, grow_ref[cidx[0], pl.ds(h, 1), :], grow_ref[cidx[1], pl.ds(h, 1), :])
        e["decay2"] = jnp.exp(jnp.where(ahead >= 0, jnp.where(left, e["gc"][0], e["gc"][1]) - gr2, -jnp.inf))
        return e

    def gram(e):
        k, q = e["k"], e["q"]
        kq = _dot(jnp.concatenate([jnp.concatenate(k, axis=1), jnp.concatenate(q, axis=1)], axis=0),
                  direction_blocks(k[0], k[1]), _NT)
        e["low2"] = jnp.where(ahead > 0, kq[:c] * jnp.where(left, e["beta"][0], e["beta"][1]) * e["decay2"], 0.0)
        e["qk2"] = kq[c:] * e["decay2"]

    def solve(e, t2):
        k, v, beta, gc = e["k"], e["v"], e["beta"], e["gc"]
        e["eg"] = [jnp.exp(gc[d]) for d in range(2)]
        rhs = [jnp.concatenate([v[d] * beta[d], k[d] * beta[d] * e["eg"][d]], axis=1) for d in range(2)]
        e["uw"] = jnp.dot(t2.astype(BF16), direction_blocks(*rhs), preferred_element_type=F32)

    def fold(e):
        uw = e["uw"]
        r2 = direction_blocks(uw[:, :2 * HEAD_DIM], uw[:, 2 * HEAD_DIM:])
        e["oq"] = jnp.dot(e["qk2"].astype(BF16), r2, preferred_element_type=F32)
        kt = jnp.concatenate([e["k"][d] * jnp.exp(e["g_last"][d] - e["gc"][d]) for d in range(2)], axis=0)
        e["bm"] = _dot(kt, r2, _TN)

    def store(e):
        n, oq, bm = e["n"], e["oq"], e["bm"]
        for d in range(2):
            base = 2 * d * HEAD_DIM
            ob[d, e["rows"][d], :] = oq[:, base:base + HEAD_DIM]
            q_s[n, d] = (e["q"][d] * e["eg"][d] - oq[:, base + HEAD_DIM:base + 2 * HEAD_DIM]).astype(BF16)
            b_s[n, d] = bm[:, base:base + HEAD_DIM]
            m_s[n, d] = bm[:, base + HEAD_DIM:base + 2 * HEAD_DIM].astype(BF16)
            gl_s[n, d] = jnp.broadcast_to(jnp.exp(e["g_last"][d]), (1, HEAD_DIM))

    def scan_step(n):
        for d in range(2):
            rows = pl.ds(pl.multiple_of((n if d == 0 else n_chunks - 1 - n) * c, c), c)
            state = st[d]
            ms = jnp.dot(jnp.concatenate([m_s[n, d], q_s[n, d]], axis=0), state.astype(BF16),
                         preferred_element_type=F32)
            ob[d, rows, :] = ob[d, rows, :] + ms[HEAD_DIM:]
            st[d] = state * gl_s[n, d] - ms[:HEAD_DIM] + b_s[n, d]

    def prepare(first, scan_first):
        pending = [] if scan_first is None else [scan_first + j for j in range(group)]
        n_slots = 8

        def fill(slot):
            for j in range(slot * len(pending) // n_slots, (slot + 1) * len(pending) // n_slots):
                scan_step(pending[j])

        chunks = [load(first + j) for j in range(group)]
        for e in chunks:
            gram(e)
        fill(0)
        inverses = _pair_inverses([e["low2"] for e in chunks], left, ahead == 0, lambda level: fill(1 + level))
        for e, t2 in zip(chunks, inverses):
            solve(e, t2)
        fill(6)
        for e in chunks:
            fold(e)
        fill(7)
        for e in chunks:
            store(e)

    for d in range(2):
        st[d] = s0_ref[d] if has_init else jnp.zeros((HEAD_DIM, HEAD_DIM), F32)

    n_steps = n_chunks // group
    prepare(0, None)

    def step(i, carry):
        prepare(i * group, (i - 1) * group)
        return carry

    lax.fori_loop(1, n_steps, step, 0)
    for j in range(group):
        scan_step((n_steps - 1) * group + j)

    def post(i, carry):
        rows = pl.ds(pl.multiple_of(i * PRE_ROWS, PRE_ROWS), PRE_ROWS)
        o_ref[rows, :] = ob[0, rows, :] + ob[1, rows, :]
        return carry

    lax.fori_loop(0, t_len // PRE_ROWS, post, 0)
    if emit_state:
        for d in range(2):
            sfin_ref[d] = st[d]


def _gdn_mixer(proj, gates, s0, layer_idx, emit_state):
    b, t, _ = proj.shape
    c = GDN_CHUNK
    n_chunks = t // c
    assert t % PRE_ROWS == 0 and 2 * c == HEAD_DIM
    as_rows = lambda g: g.reshape(b, n_chunks, c, N_HEADS).transpose(0, 1, 3, 2)
    grow = jnp.concatenate([as_rows(gates[:, :, 2 * N_HEADS:3 * N_HEADS]),
                            as_rows(gates[:, :, 3 * N_HEADS:4 * N_HEADS])], axis=-1)
    has_init = s0 is not None
    col = lambda off: pl.BlockSpec((None, t, HEAD_DIM), lambda i, h: (i, 0, off + h))
    in_specs = [col(0), col(N_HEADS), col(2 * N_HEADS),
                pl.BlockSpec((None, t, GATE_LANES), lambda i, h: (i, 0, 0)),
                pl.BlockSpec((None, n_chunks, N_HEADS, 2 * c), lambda i, h: (i, 0, 0, 0))]
    args = [proj, proj, proj, gates, grow]
    if has_init:
        in_specs.append(pl.BlockSpec((None, None, 2, None, HEAD_DIM, HEAD_DIM),
                                     lambda i, h: (i, layer_idx, 0, h, 0, 0)))
        args.append(s0)
    out_specs = [pl.BlockSpec((None, t, HEAD_DIM), lambda i, h: (i, 0, h))]
    out_shape = [jax.ShapeDtypeStruct((b, t, KEY_DIM), F32)]
    if emit_state:
        out_specs.append(pl.BlockSpec((None, 2, None, HEAD_DIM, HEAD_DIM), lambda i, h: (i, 0, h, 0, 0)))
        out_shape.append(jax.ShapeDtypeStruct((b, 2, N_HEADS, HEAD_DIM, HEAD_DIM), F32))
    scratch = [
        pltpu.VMEM((2, t, HEAD_DIM), F32),
        pltpu.VMEM((n_chunks, 2, HEAD_DIM, HEAD_DIM), BF16),
        pltpu.VMEM((n_chunks, 2, HEAD_DIM, HEAD_DIM), F32),
        pltpu.VMEM((n_chunks, 2, c, HEAD_DIM), BF16),
        pltpu.VMEM((n_chunks, 2, 1, HEAD_DIM), F32),
        pltpu.VMEM((2, HEAD_DIM, HEAD_DIM), F32)]
    res = pl.pallas_call(
        functools.partial(_gdn_kernel, has_init=has_init, emit_state=emit_state, group=min(GDN_GROUP, n_chunks)),
        grid=(b, N_HEADS),
        in_specs=in_specs,
        out_specs=out_specs,
        out_shape=out_shape,
        scratch_shapes=scratch,
        compiler_params=_params(2),
        name="gdn_mixer",
    )(*args)
    return res[0], (res[1] if emit_state else None)


def _hgrn_kernel(q_ref, ff_ref, fb_ref, i_ref, lbl_ref, *rest, layer, has_init, emit_state):
    rest = list(rest)
    s0_ref = rest.pop(0) if has_init else None
    o_ref = rest.pop(0)
    sfin_ref = rest.pop(0) if emit_state else None
    qin, kout, ktail, ftot, ob, st = rest
    t_len = q_ref.shape[0]
    c = HGRN_CHUNK
    n_chunks = t_len // c

    depth = lbl_ref.shape[0]
    logits = [lbl_ref[l] for l in range(depth)]
    top = functools.reduce(jnp.maximum, logits)
    ex = [jnp.exp(l - top) for l in logits]
    denom = functools.reduce(lambda x, y: x + y, ex)
    lb = functools.reduce(lambda x, y: x + y, ex[:layer + 1]) / denom - ex[0] / denom

    def pre(d, rows, q):
        lbd = lb[d:d + 1, :]
        fg = lbd + (1.0 - lbd) * jax.nn.sigmoid((ff_ref, fb_ref)[d][rows, :])
        k = 1.0 - fg
        logf = jnp.log(fg)
        psum = _segment_prefix(logf, c)
        total = _segment_last(psum, c)
        bcum = psum if d == 0 else total - psum + logf
        rest_sum = total - psum if d == 0 else psum - logf
        qin[d, rows, :] = q * jnp.exp(bcum)
        kout[d, rows, :] = (k * jnp.exp(-bcum)).astype(BF16)
        ktail[d, rows, :] = k * jnp.exp(rest_sum)
        ftot[d, rows, :] = jnp.exp(total)

    def pre_body(i, carry):
        rows = pl.ds(pl.multiple_of(i * HGRN_BLOCK, HGRN_BLOCK), HGRN_BLOCK)
        q = q_ref[rows, :]
        pre(0, rows, q)
        pre(1, rows, q)
        return carry

    lax.fori_loop(0, t_len // HGRN_BLOCK, pre_body, 0)

    for d in range(2):
        st[d] = s0_ref[d].T if has_init else jnp.zeros((HEAD_DIM, HEAD_DIM), F32)

    grp = HGRN_GROUP
    per = grp // c
    assert per == 4
    blk = HGRN_BLOCK
    ri = lax.broadcasted_iota(jnp.int32, (grp, grp), 0)
    ci = lax.broadcasted_iota(jnp.int32, (grp, grp), 1)
    row_chunk = lax.broadcasted_iota(jnp.int32, (grp, HEAD_DIM), 0) // c

    def group_factors(f, d):
        r1, r2, r3 = (pltpu.roll(f, s * c, axis=0) for s in (1, 2, 3))
        prev, nxt = ((r1, r2, r3), (r3, r2, r1)) if d == 0 else ((r3, r2, r1), (r1, r2, r3))
        order = row_chunk if d == 0 else per - 1 - row_chunk
        g1, h1 = prev[0], nxt[0]
        g2, h2 = g1 * prev[1], h1 * nxt[1]
        g3, h3 = g2 * prev[2], h2 * nxt[2]
        before = jnp.where(order == 0, 1.0, jnp.where(order == 1, g1, jnp.where(order == 2, g2, g3)))
        after = jnp.where(order == 3, 1.0, jnp.where(order == 2, h1, jnp.where(order == 1, h2, h3)))
        whole = f[0:1] * f[c:c + 1] * f[2 * c:2 * c + 1] * f[3 * c:3 * c + 1]
        return g1, g2, before, after, whole

    def att_select(d, p1, p234):
        dist = (ri // c - ci // c) if d == 0 else (ci // c - ri // c)
        inside = (ci <= ri) if d == 0 else (ci >= ri)
        return jnp.where((dist == 0) & inside, p1,
                         jnp.where(dist == 1, p234[:grp],
                                   jnp.where(dist == 2, p234[grp:2 * grp],
                                             jnp.where(dist == 3, p234[2 * grp:], 0.0))))

    def body(i, carry):
        ctx = []
        for step in range(blk // grp):
            for d in range(2):
                g_idx = i * (blk // grp) + step
                r0 = pl.multiple_of((g_idx if d == 0 else t_len // grp - 1 - g_idx) * grp, grp)
                rows = pl.ds(r0, grp)
                e = dict(d=d, rows=rows, qi=qin[d, rows, :], ko=kout[d, rows, :], kt=ktail[d, rows, :],
                         v=i_ref[rows, :].astype(BF16))
                e["g1"], e["g2"], e["before"], e["after"], e["whole"] = group_factors(ftot[d, rows, :], d)
                ctx.append(e)
        for e in ctx:
            qi = e["qi"]
            e["p1"] = _dot(qi, e["ko"], _NT)
            e["p234"] = _dot(jnp.concatenate([qi, qi * e["g1"], qi * e["g2"]], axis=0), e["kt"], _NT)
        for e in ctx:
            e["ds"] = _dot(e["v"], e["kt"] * e["after"], _TN)
        for e in ctx:
            e["intra"] = _dot(att_select(e["d"], e["p1"], e["p234"]), e["v"])
        states = [st[0], st[1]]
        for e in ctx:
            d = e["d"]
            ob[d, e["rows"], :] = e["intra"] + _dot(e["qi"] * e["before"], states[d], _NT)
            states[d] = states[d] * e["whole"] + e["ds"]
        st[0], st[1] = states
        return carry

    lax.fori_loop(0, t_len // blk, body, 0)

    def post(i, carry):
        rows = pl.ds(pl.multiple_of(i * PRE_ROWS, PRE_ROWS), PRE_ROWS)
        o_ref[rows, :] = ob[0, rows, :] + ob[1, rows, :]
        return carry

    lax.fori_loop(0, t_len // PRE_ROWS, post, 0)
    if emit_state:
        for d in range(2):
            sfin_ref[d] = st[d].T


def _hgrn_mixer(proj, lb_logits, s0, layer, layer_idx, emit_state):
    b, t, _ = proj.shape
    depth = lb_logits.shape[0]
    has_init = s0 is not None
    col = lambda off: pl.BlockSpec((None, t, HEAD_DIM), lambda i, h: (i, 0, off + h))
    in_specs = [col(0), col(N_HEADS), col(2 * N_HEADS), col(3 * N_HEADS),
                pl.BlockSpec((depth, 2, HEAD_DIM), lambda i, h: (0, 0, h))]
    args = [proj, proj, proj, proj, lb_logits]
    if has_init:
        in_specs.append(pl.BlockSpec((None, None, 2, None, HEAD_DIM, HEAD_DIM),
                                     lambda i, h: (i, layer_idx, 0, h, 0, 0)))
        args.append(s0)
    out_specs = [pl.BlockSpec((None, t, HEAD_DIM), lambda i, h: (i, 0, h))]
    out_shape = [jax.ShapeDtypeStruct((b, t, KEY_DIM), F32)]
    if emit_state:
        out_specs.append(pl.BlockSpec((None, 2, None, HEAD_DIM, HEAD_DIM), lambda i, h: (i, 0, h, 0, 0)))
        out_shape.append(jax.ShapeDtypeStruct((b, 2, N_HEADS, HEAD_DIM, HEAD_DIM), F32))
    res = pl.pallas_call(
        functools.partial(_hgrn_kernel, layer=layer, has_init=has_init, emit_state=emit_state),
        grid=(b, N_HEADS),
        in_specs=in_specs,
        out_specs=out_specs,
        out_shape=out_shape,
        scratch_shapes=[pltpu.VMEM((2, t, HEAD_DIM), F32), pltpu.VMEM((2, t, HEAD_DIM), BF16),
                        pltpu.VMEM((2, t, HEAD_DIM), F32), pltpu.VMEM((2, t, HEAD_DIM), F32),
                        pltpu.VMEM((2, t, HEAD_DIM), F32), pltpu.VMEM((2, HEAD_DIM, HEAD_DIM), F32)],
        compiler_params=_params(2),
        name="hgrn_mixer",
    )(*args)
    return res[0], (res[1] if emit_state else None)


def _post_kernel(x_ref, o_ref, z_ref, on_ref, g_ref, gt1_ref, sh2_ref, sc2_ref, gt2_ref, wo_ref, w1_ref, w2_ref,
                 y_ref):
    rows = x_ref.shape[0]
    n_parts = 2
    part = rows // n_parts
    mixes = []
    for r in range(n_parts):
        rs = slice(r * part, (r + 1) * part)
        gated = [(_rms(o_ref[rs, h * HEAD_DIM:(h + 1) * HEAD_DIM], on_ref[...])
                  * z_ref[rs, h * HEAD_DIM:(h + 1) * HEAD_DIM]).astype(BF16) for h in range(N_HEADS)]
        mixes.append(jnp.dot(jnp.concatenate(gated, axis=1), wo_ref[...], preferred_element_type=F32))
    mix = jnp.concatenate(mixes, axis=0)
    x1 = x_ref[...] + gt1_ref[...] * _rms(mix, g_ref[1:2, :])
    hb = (_rms(x1, g_ref[2:3, :]) * (1.0 + sc2_ref[...]) + sh2_ref[...]).astype(BF16)
    ff = jnp.zeros(x1.shape, F32)
    for j in range(D_FF // 1024):
        cols = slice(j * 1024, (j + 1) * 1024)
        hid = jnp.maximum(jnp.dot(hb, w1_ref[:, cols], preferred_element_type=F32), 0.0)
        ff = ff + jnp.dot((hid * hid).astype(BF16), w2_ref[cols, :], preferred_element_type=F32)
    y_ref[...] = x1 + gt2_ref[...] * _rms(ff, g_ref[3:4, :])


def _post_mixer(x2d, o2d, proj2d, z_block, onorm_g, norm_g, mod, tiles_per_seq, w_out, w1, w2):
    rows = x2d.shape[0]
    tile = pl.BlockSpec((ROW_TILE, D_MODEL), lambda i: (i, 0))
    return pl.pallas_call(
        _post_kernel,
        grid=(rows // ROW_TILE,),
        in_specs=[tile, tile, pl.BlockSpec((ROW_TILE, D_MODEL), lambda i: (i, z_block)),
                  _const_spec((1, HEAD_DIM)), _const_spec((4, D_MODEL)),
                  _mod_spec(2, tiles_per_seq), _mod_spec(3, tiles_per_seq),
                  _mod_spec(4, tiles_per_seq), _mod_spec(5, tiles_per_seq),
                  _const_spec((D_MODEL, D_MODEL)), _const_spec((D_MODEL, D_FF)), _const_spec((D_FF, D_MODEL))],
        out_specs=tile,
        out_shape=jax.ShapeDtypeStruct((rows, D_MODEL), F32),
        compiler_params=_params(1),
        name="post_mixer",
    )(x2d, o2d, proj2d, onorm_g.reshape(1, HEAD_DIM), norm_g, mod, mod, mod, mod, w_out, w1, w2)


def _trunk(x, mod_rows, per_seq_mod, s_gdn, s_hgrn, grid_conv, emit_state, weights):
    (norm_g, gdn_main, gdn_gate, gdn_alog, gdn_dt, gdn_conv_w, gdn_onorm_g, gdn_w_out,
     hgrn_w_in, hgrn_lb_logits, hgrn_onorm_g, hgrn_w_out, mlp_w1, mlp_w2) = weights
    b, t, _ = x.shape
    tiles_per_seq = t // ROW_TILE if per_seq_mod else None
    x2d = x.reshape(b * t, D_MODEL)
    depth = norm_g.shape[0]
    fin_gdn, fin_hgrn = [], []
    for layer in range(depth):
        j = layer // 2
        mod = mod_rows[layer]
        if layer % 2 == 0:
            proj, gates = _gdn_inproj(x2d, norm_g[layer], mod, tiles_per_seq, gdn_main[j], gdn_gate[j],
                                      gdn_alog[j], gdn_dt[j], gdn_conv_w[j], GRID_W if grid_conv else t)
            o, fin = _gdn_mixer(proj.reshape(b, t, GDN_MAIN), gates.reshape(b, t, GATE_LANES), s_gdn, j, emit_state)
            fin_gdn.append(fin)
            z_block, onorm_g, w_out = 3, gdn_onorm_g[j], gdn_w_out[j]
        else:
            proj = _hgrn_inproj(x2d, norm_g[layer], mod, tiles_per_seq, hgrn_w_in[j])
            o, fin = _hgrn_mixer(proj.reshape(b, t, HGRN_PROJ), hgrn_lb_logits, s_hgrn, layer, j, emit_state)
            fin_hgrn.append(fin)
            z_block, onorm_g, w_out = 4, hgrn_onorm_g[j], hgrn_w_out[j]
        x2d = _post_mixer(x2d, o.reshape(b * t, KEY_DIM), proj, z_block, onorm_g, norm_g[layer], mod, tiles_per_seq,
                          w_out, mlp_w1[layer], mlp_w2[layer])
    y = x2d.reshape(b, t, D_MODEL)
    if emit_state:
        return y, jnp.stack(fin_gdn, axis=1), jnp.stack(fin_hgrn, axis=1)
    return y, None, None


def kernel(x_prompt, x_sample, state_gdn, state_hgrn, c, c_ctx, w_ada, b_ada, norm_g, gdn_w_in, gdn_conv_w,
           gdn_a_log, gdn_dt_bias, gdn_onorm_g, gdn_w_out, hgrn_w_in, hgrn_lb_logits, hgrn_onorm_g, hgrn_w_out,
           mlp_w1, mlp_w2):
    n_dec = c.shape[0]
    n_rows = 16
    cond = jnp.concatenate([c_ctx[None, :], c, jnp.zeros((n_rows - 1 - n_dec, D_MODEL), F32)], axis=0)
    mod = _ada_mod(cond, w_ada, b_ada)
    mod_ctx = mod[:, 0:1, None, :]
    mod_smp = mod[:, 1:1 + n_dec, None, :]

    n_gdn = gdn_w_in.shape[0]
    gate_pad = GATE_LANES - 4 * N_HEADS
    gdn_gate = jnp.pad(gdn_w_in[:, :, GDN_MAIN:], ((0, 0), (0, 0), (0, gate_pad))).astype(BF16)
    lead = jnp.zeros((n_gdn, 2 * N_HEADS), F32)
    tail = jnp.zeros((n_gdn, gate_pad), F32)
    gdn_alog = jnp.concatenate([lead, gdn_a_log.reshape(n_gdn, 2 * N_HEADS), tail], axis=1)[:, None, :]
    gdn_dt = jnp.concatenate([lead, gdn_dt_bias.reshape(n_gdn, 2 * N_HEADS), tail], axis=1)[:, None, :]
    weights = (norm_g, gdn_w_in[:, :, :GDN_MAIN].astype(BF16), gdn_gate, gdn_alog, gdn_dt, gdn_conv_w,
               gdn_onorm_g, gdn_w_out.astype(BF16), hgrn_w_in.astype(BF16), hgrn_lb_logits, hgrn_onorm_g,
               hgrn_w_out.astype(BF16), mlp_w1.astype(BF16), mlp_w2.astype(BF16))

    y_prompt, new_gdn, new_hgrn = _trunk(x_prompt, mod_ctx, False, None, None, False, True, weights)
    y_sample, _, _ = _trunk(x_sample, mod_smp, True, state_gdn, state_hgrn, True, False, weights)
    return (y_prompt, y_sample, new_gdn, new_hgrn)
```

```python
import functools

import jax
import jax.numpy as jnp
from jax import lax
from jax.experimental import pallas as pl
from jax.experimental.pallas import tpu as pltpu

D_MODEL = 1024
N_HEADS = 8
HEAD_DIM = 128
KEY_DIM = N_HEADS * HEAD_DIM
CONV_W = 5
GDN_CHUNK = 64
HGRN_CHUNK = 16
GRID_W = 64
D_FF = 4 * D_MODEL
EPS = 1e-6
GDN_MAIN = 4 * KEY_DIM
GATE_LANES = 128
HGRN_PROJ = 5 * KEY_DIM
HGRN_OUT = 7 * KEY_DIM

ROW_TILE = 512
PRE_ROWS = 256
HGRN_BLOCK = 256
HGRN_GROUP = 64
GDN_GROUP = 8
VMEM_LIMIT = 56 * 1024 * 1024

BF16 = jnp.bfloat16
F32 = jnp.float32

_NT = (((1,), (1,)), ((), ()))
_TN = (((0,), (0,)), ((), ()))


def _dot(a, b, dims=None):
    a = a.astype(BF16)
    b = b.astype(BF16)
    if dims is None:
        return jnp.dot(a, b, preferred_element_type=F32)
    return lax.dot_general(a, b, dims, preferred_element_type=F32)


def _segment_sums(x, seg):
    rows = x.shape[0]
    pos = lax.broadcasted_iota(jnp.int32, x.shape, 0) & (seg - 1)
    pre, suf = x, x
    s = 1
    while s < seg:
        pre = pre + jnp.where(pos >= s, pltpu.roll(pre, s, axis=0), 0.0)
        suf = suf + jnp.where(pos + s < seg, pltpu.roll(suf, rows - s, axis=0), 0.0)
        s *= 2
    return pre, suf


def _segment_prefix(x, seg):
    pos = lax.broadcasted_iota(jnp.int32, x.shape, 0) & (seg - 1)
    s = 1
    while s < seg:
        x = x + jnp.where(pos >= s, pltpu.roll(x, s, axis=0), 0.0)
        s *= 2
    return x


def _segment_pick(x, seg, which):
    rows, width = x.shape
    picked = x.reshape(rows // seg, seg, width)[:, which:which + 1, :]
    return jnp.broadcast_to(picked, (rows // seg, seg, width)).reshape(rows, width)


def _segment_last(x, seg):
    return _segment_pick(x, seg, seg - 1)


def _rms(x, g):
    return x * lax.rsqrt(jnp.mean(x * x, axis=-1, keepdims=True) + EPS) * g


def _silu(x):
    return x * jax.nn.sigmoid(x)


def _softplus(x):
    return jnp.maximum(x, 0.0) + jnp.log1p(jnp.exp(-jnp.abs(x)))


def _const_spec(shape):
    zeros = (0,) * len(shape)
    return pl.BlockSpec(shape, lambda *_: zeros, pipeline_mode=pl.Buffered(1))


def _mod_spec(col, tiles_per_seq):
    if tiles_per_seq is None:
        return pl.BlockSpec((None, 1, D_MODEL), lambda i: (0, 0, col))
    return pl.BlockSpec((None, 1, D_MODEL), lambda i: (i // tiles_per_seq, 0, col))


def _params(n_grid):
    return pltpu.CompilerParams(dimension_semantics=("arbitrary",) * n_grid, vmem_limit_bytes=VMEM_LIMIT)


def _ada_kernel(c_ref, w_ref, b_ref, o_ref):
    o_ref[...] = _dot(_silu(c_ref[...]), w_ref[...]) + b_ref[...]


def _ada_mod(cond, w_ada, b_ada):
    depth, _, n = w_ada.shape
    rows = cond.shape[0]
    tn = 1536
    return pl.pallas_call(
        _ada_kernel,
        grid=(depth, n // tn),
        in_specs=[pl.BlockSpec((rows, D_MODEL), lambda l, j: (0, 0)),
                  pl.BlockSpec((None, D_MODEL, tn), lambda l, j: (l, 0, j)),
                  pl.BlockSpec((None, 1, tn), lambda l, j: (l, 0, j))],
        out_specs=pl.BlockSpec((None, rows, tn), lambda l, j: (l, 0, j)),
        out_shape=jax.ShapeDtypeStruct((depth, rows, n), F32),
        compiler_params=_params(2),
        name="ada_mod",
    )(cond, w_ada, b_ada.reshape(depth, 1, n))


def _modulated(x_ref, g_ref, sh_ref, sc_ref):
    return (_rms(x_ref[...], g_ref[0:1, :]) * (1.0 + sc_ref[...]) + sh_ref[...]).astype(BF16)


def _gdn_inproj_kernel(x_ref, g_ref, sh_ref, sc_ref, w_ref, wg_ref, alog_ref, dt_ref, cw_ref, proj_ref, gate_ref,
                       *, seg):
    hb = _modulated(x_ref, g_ref, sh_ref, sc_ref)
    n_rows = hb.shape[0]
    chunk = 512
    period = min(seg, PRE_ROWS)
    pos = lax.broadcasted_iota(jnp.int32, (period, HEAD_DIM), 0) & (seg - 1)

    def tap_weights(w):
        taps = {}
        for s in (-2, -1, 1, 2):
            valid = (pos + s >= 0) & (pos + s < seg)
            taps[s] = jnp.tile(jnp.where(valid, w[s + 2:s + 3, :], 0.0), (PRE_ROWS // period, 1))
        return taps

    def unit(x, scale):
        return x * (lax.rsqrt(jnp.sum(x * x, axis=-1, keepdims=True) + EPS) * scale)

    n_chunks = GDN_MAIN // chunk
    product = lambda j: jnp.dot(hb, w_ref[:, j * chunk:(j + 1) * chunk], preferred_element_type=F32)
    y_next = product(0)
    for j in range(n_chunks):
        y, y_next = y_next, (product(j + 1) if j + 1 < n_chunks else None)
        kind = j * chunk // KEY_DIM
        if kind == 3:
            proj_ref[:, j * chunk:(j + 1) * chunk] = _silu(y)
            continue
        for hh in range(chunk // HEAD_DIM):
            c0 = j * chunk + hh * HEAD_DIM
            w = cw_ref[:, c0:c0 + HEAD_DIM]
            taps = tap_weights(w)
            for r in range(n_rows // PRE_ROWS):
                x = y[r * PRE_ROWS:(r + 1) * PRE_ROWS, hh * HEAD_DIM:(hh + 1) * HEAD_DIM]
                acc = x * w[CONV_W // 2:CONV_W // 2 + 1, :]
                for s in (-2, -1, 1, 2):
                    acc = acc + pltpu.roll(x, (-s) % PRE_ROWS, axis=0) * taps[s]
                out = _silu(acc)
                if kind == 0:
                    out = unit(out, HEAD_DIM ** -0.5)
                elif kind == 1:
                    out = unit(out, 1.0)
                proj_ref[r * PRE_ROWS:(r + 1) * PRE_ROWS, c0:c0 + HEAD_DIM] = out
    raw = jnp.dot(hb, wg_ref[...], preferred_element_type=F32)
    beta = jax.nn.sigmoid(raw)
    g = -jnp.exp(alog_ref[...]) * _softplus(raw + dt_ref[...])
    gp, gs = _segment_sums(g, GDN_CHUNK)
    lane = lax.broadcasted_iota(jnp.int32, raw.shape, 1)
    gate_ref[...] = jnp.where(lane < 2 * N_HEADS, beta,
                              jnp.where(lane < 3 * N_HEADS, gp,
                                        jnp.where(lane < 4 * N_HEADS, gs, 0.0)))


def _gdn_inproj(x2d, norm_g, mod, tiles_per_seq, w_main, w_gate, alog_row, dt_row, conv_w, seg):
    rows = x2d.shape[0]
    assert PRE_ROWS % seg == 0 and ROW_TILE % PRE_ROWS == 0
    return pl.pallas_call(
        functools.partial(_gdn_inproj_kernel, seg=seg),
        grid=(rows // ROW_TILE,),
        in_specs=[pl.BlockSpec((ROW_TILE, D_MODEL), lambda i: (i, 0)),
                  _const_spec((4, D_MODEL)),
                  _mod_spec(0, tiles_per_seq), _mod_spec(1, tiles_per_seq),
                  _const_spec((D_MODEL, GDN_MAIN)), _const_spec((D_MODEL, GATE_LANES)),
                  _const_spec((1, GATE_LANES)), _const_spec((1, GATE_LANES)),
                  _const_spec((CONV_W, 3 * KEY_DIM))],
        out_specs=[pl.BlockSpec((ROW_TILE, GDN_MAIN), lambda i: (i, 0)),
                   pl.BlockSpec((ROW_TILE, GATE_LANES), lambda i: (i, 0))],
        out_shape=[jax.ShapeDtypeStruct((rows, GDN_MAIN), F32),
                   jax.ShapeDtypeStruct((rows, GATE_LANES), F32)],
        compiler_params=_params(1),
        name="gdn_inproj",
    )(x2d, norm_g, mod, mod, w_main, w_gate, alog_row, dt_row, conv_w)


def _hgrn_inproj_kernel(x_ref, g_ref, sh_ref, sc_ref, w_ref, lbl_ref, out_ref, *, layer):
    hb = _modulated(x_ref, g_ref, sh_ref, sc_ref)
    n_rows = hb.shape[0]
    c = HGRN_CHUNK
    depth = lbl_ref.shape[0]
    logits = [lbl_ref[l] for l in range(depth)]
    top = functools.reduce(jnp.maximum, logits)
    ex = [jnp.exp(l - top) for l in logits]
    denom = functools.reduce(lambda x, y: x + y, ex)
    lb = functools.reduce(lambda x, y: x + y, ex[:layer + 1]) / denom - ex[0] / denom

    chunk = 512
    n_chunks = HGRN_PROJ // chunk
    product = lambda j: jnp.dot(hb, w_ref[:, j * chunk:(j + 1) * chunk], preferred_element_type=F32)
    y_next = product(0)
    for j in range(n_chunks):
        y, y_next = y_next, (product(j + 1) if j + 1 < n_chunks else None)
        kind, off = divmod(j * chunk, KEY_DIM)
        if kind in (0, 3, 4):
            dst = {0: 0, 3: 5, 4: 6}[kind] * KEY_DIM + off
            out_ref[:, dst:dst + chunk] = y if kind == 3 else _silu(y)
            continue
        d = kind - 1
        for hh in range(chunk // HEAD_DIM):
            c0 = off + hh * HEAD_DIM
            lbd = lb[d:d + 1, c0:c0 + HEAD_DIM]
            for r in range(n_rows // PRE_ROWS):
                rows = slice(r * PRE_ROWS, (r + 1) * PRE_ROWS)
                fg = lbd + (1.0 - lbd) * jax.nn.sigmoid(y[rows, hh * HEAD_DIM:(hh + 1) * HEAD_DIM])
                logf = jnp.log(fg)
                run = _segment_prefix(logf, c)
                if d == 1:
                    run = _segment_last(run, c) - run + logf
                out_ref[rows, (1 + 2 * d) * KEY_DIM + c0:(1 + 2 * d) * KEY_DIM + c0 + HEAD_DIM] = 1.0 - fg
                out_ref[rows, (2 + 2 * d) * KEY_DIM + c0:(2 + 2 * d) * KEY_DIM + c0 + HEAD_DIM] = jnp.exp(run)


def _hgrn_inproj(x2d, norm_g, mod, tiles_per_seq, w_in, lb_logits, layer):
    rows = x2d.shape[0]
    depth = lb_logits.shape[0]
    return pl.pallas_call(
        functools.partial(_hgrn_inproj_kernel, layer=layer),
        grid=(rows // ROW_TILE,),
        in_specs=[pl.BlockSpec((ROW_TILE, D_MODEL), lambda i: (i, 0)),
                  _const_spec((4, D_MODEL)),
                  _mod_spec(0, tiles_per_seq), _mod_spec(1, tiles_per_seq),
                  _const_spec((D_MODEL, HGRN_PROJ)), _const_spec((depth, 2, KEY_DIM))],
        out_specs=pl.BlockSpec((ROW_TILE, HGRN_OUT), lambda i: (i, 0)),
        out_shape=jax.ShapeDtypeStruct((rows, HGRN_OUT), F32),
        compiler_params=_params(1),
        name="hgrn_inproj",
    )(x2d, norm_g, mod, mod, w_in, lb_logits)


def _split_bf16(x):
    hi = lax.bitcast_convert_type(lax.bitcast_convert_type(x, jnp.int32) & jnp.int32(-65536), F32)
    return hi, x - hi


def _pair_inverses(lows, left, diag2, fill):
    c = lows[0].shape[0]

    def block_diag(x):
        return jnp.concatenate([jnp.where(left, x, 0.0), jnp.where(left, 0.0, x)], axis=0)

    def left_operand(hi, lo):
        return jnp.concatenate([hi.astype(BF16), lo.astype(BF16)] * 2, axis=1)

    def right_operand(hi, lo):
        bh, bl = block_diag(hi).astype(BF16), block_diag(lo).astype(BF16)
        return jnp.concatenate([bh, bh, bl, bl], axis=0)

    accs = [jnp.where(diag2, 1.0, 0.0) - x for x in lows]
    parts = [_split_bf16(x) for x in lows]
    powers = [jnp.dot(left_operand(hi, lo), right_operand(hi, lo), preferred_element_type=F32) for hi, lo in parts]
    fill(0)
    levels = c.bit_length() - 2
    for level in range(levels):
        parts = [_split_bf16(x) for x in powers]
        rhs = [right_operand(hi, lo) for hi, lo in parts]
        acc_lhs = [left_operand(*_split_bf16(a)) for a in accs]
        if level + 1 < levels:
            res = [jnp.dot(jnp.concatenate([left_operand(hi, lo), al], axis=0), r, preferred_element_type=F32)
                   for (hi, lo), al, r in zip(parts, acc_lhs, rhs)]
            fill(level + 1)
            powers = [x[:c] for x in res]
            accs = [a + x[c:] for a, x in zip(accs, res)]
        else:
            accs = [a + jnp.dot(al, r, preferred_element_type=F32) for a, al, r in zip(accs, acc_lhs, rhs)]
    return accs


def _gdn_kernel(qn, kn, vn, gcol_ref, grow_ref, *rest, has_init, emit_state, group):
    rest = list(rest)
    s0_ref = rest.pop(0) if has_init else None
    o_ref = rest.pop(0)
    sfin_ref = rest.pop(0) if emit_state else None
    ob, m_s, b_s, q_s, gl_s, st = rest
    t_len = qn.shape[0]
    c = GDN_CHUNK
    n_chunks = t_len // c
    h = pl.program_id(1)

    lane = lax.broadcasted_iota(jnp.int32, (c, 2 * c), 1)
    row = lax.broadcasted_iota(jnp.int32, (c, 2 * c), 0)
    left = lane < c
    ahead = jnp.where(left, row - lane, lane - c - row)
    left_row = lax.broadcasted_iota(jnp.int32, (1, 2 * c), 1) < c

    def column(tile, idx):
        return jnp.sum(jnp.where(lane == idx, tile, 0.0), axis=-1, keepdims=True)

    def direction_blocks(a0, a1):
        z0, z1 = jnp.zeros(a0.shape, BF16), jnp.zeros(a1.shape, BF16)
        return jnp.concatenate([jnp.concatenate([a0.astype(BF16), z1], axis=1),
                                jnp.concatenate([z0, a1.astype(BF16)], axis=1)], axis=0)

    def load(n):
        cidx = [n, n_chunks - 1 - n]
        rows = [pl.ds(pl.multiple_of(ci * c, c), c) for ci in cidx]
        e = dict(n=n, rows=rows, q=[qn[r, :] for r in rows], k=[kn[r, :] for r in rows], v=[vn[r, :] for r in rows])
        gtile = [gcol_ref[r, :] for r in rows]
        e["beta"] = [column(gtile[d], d * N_HEADS + h) for d in range(2)]
        e["gc"] = [column(gtile[d], (2 + d) * N_HEADS + h) for d in range(2)]
        e["g_last"] = [e["gc"][0][c - 1:c, :], e["gc"][1][0:1, :]]
        gr2 = jnp.where(left_row, grow_ref[cidx[0], pl.ds(h, 1), :], grow_ref[cidx[1], pl.ds(h, 1), :])
        e["decay2"] = jnp.exp(jnp.where(ahead >= 0, jnp.where(left, e["gc"][0], e["gc"][1]) - gr2, -jnp.inf))
        return e

    def gram(e):
        k, q = e["k"], e["q"]
        kq = _dot(jnp.concatenate([jnp.concatenate(k, axis=1), jnp.concatenate(q, axis=1)], axis=0),
                  direction_blocks(k[0], k[1]), _NT)
        e["low2"] = jnp.where(ahead > 0, kq[:c] * jnp.where(left, e["beta"][0], e["beta"][1]) * e["decay2"], 0.0)
        e["qk2"] = kq[c:] * e["decay2"]

    def solve(e, t2):
        k, v, beta, gc = e["k"], e["v"], e["beta"], e["gc"]
        e["eg"] = [jnp.exp(gc[d]) for d in range(2)]
        rhs = [jnp.concatenate([v[d] * beta[d], k[d] * beta[d] * e["eg"][d]], axis=1) for d in range(2)]
        e["uw"] = jnp.dot(t2.astype(BF16), direction_blocks(*rhs), preferred_element_type=F32)

    def fold(e):
        uw = e["uw"]
        r2 = direction_blocks(uw[:, :2 * HEAD_DIM], uw[:, 2 * HEAD_DIM:])
        e["oq"] = jnp.dot(e["qk2"].astype(BF16), r2, preferred_element_type=F32)
        kt = jnp.concatenate([e["k"][d] * jnp.exp(e["g_last"][d] - e["gc"][d]) for d in range(2)], axis=0)
        e["bm"] = _dot(kt, r2, _TN)

    def store(e):
        n, oq, bm = e["n"], e["oq"], e["bm"]
        for d in range(2):
            base = 2 * d * HEAD_DIM
            ob[d, e["rows"][d], :] = oq[:, base:base + HEAD_DIM]
            q_s[n, d] = (e["q"][d] * e["eg"][d] - oq[:, base + HEAD_DIM:base + 2 * HEAD_DIM]).astype(BF16)
            b_s[n, d] = bm[:, base:base + HEAD_DIM]
            m_s[n, d] = bm[:, base + HEAD_DIM:base + 2 * HEAD_DIM].astype(BF16)
            gl_s[n, d] = jnp.broadcast_to(jnp.exp(e["g_last"][d]), (1, HEAD_DIM))

    def scan_step(n):
        for d in range(2):
            rows = pl.ds(pl.multiple_of((n if d == 0 else n_chunks - 1 - n) * c, c), c)
            state = st[d]
            ms = jnp.dot(jnp.concatenate([m_s[n, d], q_s[n, d]], axis=0), state.astype(BF16),
                         preferred_element_type=F32)
            ob[d, rows, :] = ob[d, rows, :] + ms[HEAD_DIM:]
            st[d] = state * gl_s[n, d] - ms[:HEAD_DIM] + b_s[n, d]

    def prepare(first, scan_first):
        pending = [] if scan_first is None else [scan_first + j for j in range(group)]
        n_slots = 8

        def fill(slot):
            for j in range(slot * len(pending) // n_slots, (slot + 1) * len(pending) // n_slots):
                scan_step(pending[j])

        chunks = [load(first + j) for j in range(group)]
        for e in chunks:
            gram(e)
        fill(0)
        inverses = _pair_inverses([e["low2"] for e in chunks], left, ahead == 0, lambda level: fill(1 + level))
        for e, t2 in zip(chunks, inverses):
            solve(e, t2)
        fill(6)
        for e in chunks:
            fold(e)
        fill(7)
        for e in chunks:
            store(e)

    for d in range(2):
        st[d] = s0_ref[d] if has_init else jnp.zeros((HEAD_DIM, HEAD_DIM), F32)

    n_steps = n_chunks // group
    prepare(0, None)

    def step(i, carry):
        prepare(i * group, (i - 1) * group)
        return carry

    lax.fori_loop(1, n_steps, step, 0)
    for j in range(group):
        scan_step((n_steps - 1) * group + j)

    def post(i, carry):
        rows = pl.ds(pl.multiple_of(i * PRE_ROWS, PRE_ROWS), PRE_ROWS)
        o_ref[rows, :] = ob[0, rows, :] + ob[1, rows, :]
        return carry

    lax.fori_loop(0, t_len // PRE_ROWS, post, 0)
    if emit_state:
        for d in range(2):
            sfin_ref[d] = st[d]


def _gdn_mixer(proj, gates, s0, layer_idx, emit_state):
    b, t, _ = proj.shape
    c = GDN_CHUNK
    n_chunks = t // c
    assert t % PRE_ROWS == 0 and 2 * c == HEAD_DIM
    as_rows = lambda g: g.reshape(b, n_chunks, c, N_HEADS).transpose(0, 1, 3, 2)
    grow = jnp.concatenate([as_rows(gates[:, :, 2 * N_HEADS:3 * N_HEADS]),
                            as_rows(gates[:, :, 3 * N_HEADS:4 * N_HEADS])], axis=-1)
    has_init = s0 is not None
    col = lambda off: pl.BlockSpec((None, t, HEAD_DIM), lambda i, h: (i, 0, off + h))
    in_specs = [col(0), col(N_HEADS), col(2 * N_HEADS),
                pl.BlockSpec((None, t, GATE_LANES), lambda i, h: (i, 0, 0)),
                pl.BlockSpec((None, n_chunks, N_HEADS, 2 * c), lambda i, h: (i, 0, 0, 0))]
    args = [proj, proj, proj, gates, grow]
    if has_init:
        in_specs.append(pl.BlockSpec((None, None, 2, None, HEAD_DIM, HEAD_DIM),
                                     lambda i, h: (i, layer_idx, 0, h, 0, 0)))
        args.append(s0)
    out_specs = [pl.BlockSpec((None, t, HEAD_DIM), lambda i, h: (i, 0, h))]
    out_shape = [jax.ShapeDtypeStruct((b, t, KEY_DIM), F32)]
    if emit_state:
        out_specs.append(pl.BlockSpec((None, 2, None, HEAD_DIM, HEAD_DIM), lambda i, h: (i, 0, h, 0, 0)))
        out_shape.append(jax.ShapeDtypeStruct((b, 2, N_HEADS, HEAD_DIM, HEAD_DIM), F32))
    scratch = [
        pltpu.VMEM((2, t, HEAD_DIM), F32),
        pltpu.VMEM((n_chunks, 2, HEAD_DIM, HEAD_DIM), BF16),
        pltpu.VMEM((n_chunks, 2, HEAD_DIM, HEAD_DIM), F32),
        pltpu.VMEM((n_chunks, 2, c, HEAD_DIM), BF16),
        pltpu.VMEM((n_chunks, 2, 1, HEAD_DIM), F32),
        pltpu.VMEM((2, HEAD_DIM, HEAD_DIM), F32)]
    res = pl.pallas_call(
        functools.partial(_gdn_kernel, has_init=has_init, emit_state=emit_state, group=min(GDN_GROUP, n_chunks)),
        grid=(b, N_HEADS),
        in_specs=in_specs,
        out_specs=out_specs,
        out_shape=out_shape,
        scratch_shapes=scratch,
        compiler_params=_params(2),
        name="gdn_mixer",
    )(*args)
    return res[0], (res[1] if emit_state else None)


def _hgrn_kernel(q_ref, kf_ref, ef_ref, kb_ref, eb_ref, i_ref, *rest, has_init, emit_state):
    rest = list(rest)
    s0_ref = rest.pop(0) if has_init else None
    o_ref = rest.pop(0)
    sfin_ref = rest.pop(0) if emit_state else None
    qin, kout, ktail, ftot, ob, st = rest
    t_len = q_ref.shape[0]
    c = HGRN_CHUNK

    def pre(d, rows, q):
        k = (kf_ref, kb_ref)[d][rows, :]
        e = (ef_ref, eb_ref)[d][rows, :]
        whole = _segment_pick(e, c, c - 1 if d == 0 else 0)
        k_over_e = k / e
        qin[d, rows, :] = q * e
        kout[d, rows, :] = k_over_e.astype(BF16)
        ktail[d, rows, :] = k_over_e * whole
        ftot[d, rows, :] = whole

    def pre_body(i, carry):
        rows = pl.ds(pl.multiple_of(i * HGRN_BLOCK, HGRN_BLOCK), HGRN_BLOCK)
        q = q_ref[rows, :]
        pre(0, rows, q)
        pre(1, rows, q)
        return carry

    lax.fori_loop(0, t_len // HGRN_BLOCK, pre_body, 0)

    for d in range(2):
        st[d] = s0_ref[d].T if has_init else jnp.zeros((HEAD_DIM, HEAD_DIM), F32)

    grp = HGRN_GROUP
    per = grp // c
    assert per == 4
    blk = HGRN_BLOCK
    ri = lax.broadcasted_iota(jnp.int32, (grp, grp), 0)
    ci = lax.broadcasted_iota(jnp.int32, (grp, grp), 1)
    row_chunk = lax.broadcasted_iota(jnp.int32, (grp, HEAD_DIM), 0) // c

    def group_factors(f, d):
        r1, r2, r3 = (pltpu.roll(f, s * c, axis=0) for s in (1, 2, 3))
        prev, nxt = ((r1, r2, r3), (r3, r2, r1)) if d == 0 else ((r3, r2, r1), (r1, r2, r3))
        order = row_chunk if d == 0 else per - 1 - row_chunk
        g1, h1 = prev[0], nxt[0]
        g2, h2 = g1 * prev[1], h1 * nxt[1]
        g3, h3 = g2 * prev[2], h2 * nxt[2]
        before = jnp.where(order == 0, 1.0, jnp.where(order == 1, g1, jnp.where(order == 2, g2, g3)))
        after = jnp.where(order == 3, 1.0, jnp.where(order == 2, h1, jnp.where(order == 1, h2, h3)))
        whole = f[0:1] * f[c:c + 1] * f[2 * c:2 * c + 1] * f[3 * c:3 * c + 1]
        return g1, g2, before, after, whole

    def att_select(d, p1, p234):
        dist = (ri // c - ci // c) if d == 0 else (ci // c - ri // c)
        inside = (ci <= ri) if d == 0 else (ci >= ri)
        return jnp.where((dist == 0) & inside, p1,
                         jnp.where(dist == 1, p234[:grp],
                                   jnp.where(dist == 2, p234[grp:2 * grp],
                                             jnp.where(dist == 3, p234[2 * grp:], 0.0))))

    def body(i, carry):
        ctx = []
        for step in range(blk // grp):
            for d in range(2):
                g_idx = i * (blk // grp) + step
                r0 = pl.multiple_of((g_idx if d == 0 else t_len // grp - 1 - g_idx) * grp, grp)
                rows = pl.ds(r0, grp)
                e = dict(d=d, rows=rows, qi=qin[d, rows, :], ko=kout[d, rows, :], kt=ktail[d, rows, :],
                         v=i_ref[rows, :].astype(BF16))
                e["g1"], e["g2"], e["before"], e["after"], e["whole"] = group_factors(ftot[d, rows, :], d)
                ctx.append(e)
        for e in ctx:
            qi = e["qi"]
            e["p1"] = _dot(qi, e["ko"], _NT)
            e["p234"] = _dot(jnp.concatenate([qi, qi * e["g1"], qi * e["g2"]], axis=0), e["kt"], _NT)
        for e in ctx:
            e["ds"] = _dot(e["v"], e["kt"] * e["after"], _TN)
        for e in ctx:
            e["intra"] = _dot(att_select(e["d"], e["p1"], e["p234"]), e["v"])
        states = [st[0], st[1]]
        for e in ctx:
            d = e["d"]
            ob[d, e["rows"], :] = e["intra"] + _dot(e["qi"] * e["before"], states[d], _NT)
            states[d] = states[d] * e["whole"] + e["ds"]
        st[0], st[1] = states
        return carry

    lax.fori_loop(0, t_len // blk, body, 0)

    def post(i, carry):
        rows = pl.ds(pl.multiple_of(i * PRE_ROWS, PRE_ROWS), PRE_ROWS)
        o_ref[rows, :] = ob[0, rows, :] + ob[1, rows, :]
        return carry

    lax.fori_loop(0, t_len // PRE_ROWS, post, 0)
    if emit_state:
        for d in range(2):
            sfin_ref[d] = st[d].T


def _hgrn_mixer(proj, s0, layer_idx, emit_state):
    b, t, _ = proj.shape
    has_init = s0 is not None
    col = lambda off: pl.BlockSpec((None, t, HEAD_DIM), lambda i, h: (i, 0, off + h))
    in_specs = [col(j * N_HEADS) for j in range(6)]
    args = [proj] * 6
    if has_init:
        in_specs.append(pl.BlockSpec((None, None, 2, None, HEAD_DIM, HEAD_DIM),
                                     lambda i, h: (i, layer_idx, 0, h, 0, 0)))
        args.append(s0)
    out_specs = [pl.BlockSpec((None, t, HEAD_DIM), lambda i, h: (i, 0, h))]
    out_shape = [jax.ShapeDtypeStruct((b, t, KEY_DIM), F32)]
    if emit_state:
        out_specs.append(pl.BlockSpec((None, 2, None, HEAD_DIM, HEAD_DIM), lambda i, h: (i, 0, h, 0, 0)))
        out_shape.append(jax.ShapeDtypeStruct((b, 2, N_HEADS, HEAD_DIM, HEAD_DIM), F32))
    res = pl.pallas_call(
        functools.partial(_hgrn_kernel, has_init=has_init, emit_state=emit_state),
        grid=(b, N_HEADS),
        in_specs=in_specs,
        out_specs=out_specs,
        out_shape=out_shape,
        scratch_shapes=[pltpu.VMEM((2, t, HEAD_DIM), F32), pltpu.VMEM((2, t, HEAD_DIM), BF16),
                        pltpu.VMEM((2, t, HEAD_DIM), F32), pltpu.VMEM((2, t, HEAD_DIM), F32),
                        pltpu.VMEM((2, t, HEAD_DIM), F32), pltpu.VMEM((2, HEAD_DIM, HEAD_DIM), F32)],
        compiler_params=_params(2),
        name="hgrn_mixer",
    )(*args)
    return res[0], (res[1] if emit_state else None)


def _post_kernel(x_ref, o_ref, z_ref, on_ref, g_ref, gt1_ref, sh2_ref, sc2_ref, gt2_ref, wo_ref, w1_ref, w2_ref,
                 y_ref):
    rows = x_ref.shape[0]
    n_parts = 2
    part = rows // n_parts
    mixes = []
    for r in range(n_parts):
        rs = slice(r * part, (r + 1) * part)
        gated = [(_rms(o_ref[rs, h * HEAD_DIM:(h + 1) * HEAD_DIM], on_ref[...])
                  * z_ref[rs, h * HEAD_DIM:(h + 1) * HEAD_DIM]).astype(BF16) for h in range(N_HEADS)]
        mixes.append(jnp.dot(jnp.concatenate(gated, axis=1), wo_ref[...], preferred_element_type=F32))
    mix = jnp.concatenate(mixes, axis=0)
    x1 = x_ref[...] + gt1_ref[...] * _rms(mix, g_ref[1:2, :])
    hb = (_rms(x1, g_ref[2:3, :]) * (1.0 + sc2_ref[...]) + sh2_ref[...]).astype(BF16)
    ff = jnp.zeros(x1.shape, F32)
    for j in range(D_FF // 1024):
        cols = slice(j * 1024, (j + 1) * 1024)
        hid = jnp.maximum(jnp.dot(hb, w1_ref[:, cols], preferred_element_type=F32), 0.0)
        ff = ff + jnp.dot((hid * hid).astype(BF16), w2_ref[cols, :], preferred_element_type=F32)
    y_ref[...] = x1 + gt2_ref[...] * _rms(ff, g_ref[3:4, :])


def _post_mixer(x2d, o2d, proj2d, z_block, onorm_g, norm_g, mod, tiles_per_seq, w_out, w1, w2):
    rows = x2d.shape[0]
    tile = pl.BlockSpec((ROW_TILE, D_MODEL), lambda i: (i, 0))
    return pl.pallas_call(
        _post_kernel,
        grid=(rows // ROW_TILE,),
        in_specs=[tile, tile, pl.BlockSpec((ROW_TILE, D_MODEL), lambda i: (i, z_block)),
                  _const_spec((1, HEAD_DIM)), _const_spec((4, D_MODEL)),
                  _mod_spec(2, tiles_per_seq), _mod_spec(3, tiles_per_seq),
                  _mod_spec(4, tiles_per_seq), _mod_spec(5, tiles_per_seq),
                  _const_spec((D_MODEL, D_MODEL)), _const_spec((D_MODEL, D_FF)), _const_spec((D_FF, D_MODEL))],
        out_specs=tile,
        out_shape=jax.ShapeDtypeStruct((rows, D_MODEL), F32),
        compiler_params=_params(1),
        name="post_mixer",
    )(x2d, o2d, proj2d, onorm_g.reshape(1, HEAD_DIM), norm_g, mod, mod, mod, mod, w_out, w1, w2)


def _trunk(x, mod_rows, per_seq_mod, s_gdn, s_hgrn, grid_conv, emit_state, weights):
    (norm_g, gdn_main, gdn_gate, gdn_alog, gdn_dt, gdn_conv_w, gdn_onorm_g, gdn_w_out,
     hgrn_w_in, hgrn_lb_logits, hgrn_onorm_g, hgrn_w_out, mlp_w1, mlp_w2) = weights
    b, t, _ = x.shape
    tiles_per_seq = t // ROW_TILE if per_seq_mod else None
    x2d = x.reshape(b * t, D_MODEL)
    depth = norm_g.shape[0]
    fin_gdn, fin_hgrn = [], []
    for layer in range(depth):
        j = layer // 2
        mod = mod_rows[layer]
        if layer % 2 == 0:
            proj, gates = _gdn_inproj(x2d, norm_g[layer], mod, tiles_per_seq, gdn_main[j], gdn_gate[j],
                                      gdn_alog[j], gdn_dt[j], gdn_conv_w[j], GRID_W if grid_conv else t)
            o, fin = _gdn_mixer(proj.reshape(b, t, GDN_MAIN), gates.reshape(b, t, GATE_LANES), s_gdn, j, emit_state)
            fin_gdn.append(fin)
            z_block, onorm_g, w_out = 3, gdn_onorm_g[j], gdn_w_out[j]
        else:
            proj = _hgrn_inproj(x2d, norm_g[layer], mod, tiles_per_seq, hgrn_w_in[j], hgrn_lb_logits, layer)
            o, fin = _hgrn_mixer(proj.reshape(b, t, HGRN_OUT), s_hgrn, j, emit_state)
            fin_hgrn.append(fin)
            z_block, onorm_g, w_out = 6, hgrn_onorm_g[j], hgrn_w_out[j]
        x2d = _post_mixer(x2d, o.reshape(b * t, KEY_DIM), proj, z_block, onorm_g, norm_g[layer], mod, tiles_per_seq,
                          w_out, mlp_w1[layer], mlp_w2[layer])
    y = x2d.reshape(b, t, D_MODEL)
    if emit_state:
        return y, jnp.stack(fin_gdn, axis=1), jnp.stack(fin_hgrn, axis=1)
    return y, None, None


def kernel(x_prompt, x_sample, state_gdn, state_hgrn, c, c_ctx, w_ada, b_ada, norm_g, gdn_w_in, gdn_conv_w,
           gdn_a_log, gdn_dt_bias, gdn_onorm_g, gdn_w_out, hgrn_w_in, hgrn_lb_logits, hgrn_onorm_g, hgrn_w_out,
           mlp_w1, mlp_w2):
    n_dec = c.shape[0]
    n_rows = 16
    cond = jnp.concatenate([c_ctx[None, :], c, jnp.zeros((n_rows - 1 - n_dec, D_MODEL), F32)], axis=0)
    mod = _ada_mod(cond, w_ada, b_ada)
    mod_ctx = mod[:, 0:1, None, :]
    mod_smp = mod[:, 1:1 + n_dec, None, :]

    n_gdn = gdn_w_in.shape[0]
    gate_pad = GATE_LANES - 4 * N_HEADS
    gdn_gate = jnp.pad(gdn_w_in[:, :, GDN_MAIN:], ((0, 0), (0, 0), (0, gate_pad))).astype(BF16)
    lead = jnp.zeros((n_gdn, 2 * N_HEADS), F32)
    tail = jnp.zeros((n_gdn, gate_pad), F32)
    gdn_alog = jnp.concatenate([lead, gdn_a_log.reshape(n_gdn, 2 * N_HEADS), tail], axis=1)[:, None, :]
    gdn_dt = jnp.concatenate([lead, gdn_dt_bias.reshape(n_gdn, 2 * N_HEADS), tail], axis=1)[:, None, :]
    weights = (norm_g, gdn_w_in[:, :, :GDN_MAIN].astype(BF16), gdn_gate, gdn_alog, gdn_dt, gdn_conv_w,
               gdn_onorm_g, gdn_w_out.astype(BF16), hgrn_w_in.astype(BF16), hgrn_lb_logits, hgrn_onorm_g,
               hgrn_w_out.astype(BF16), mlp_w1.astype(BF16), mlp_w2.astype(BF16))

    y_prompt, new_gdn, new_hgrn = _trunk(x_prompt, mod_ctx, False, None, None, False, True, weights)
    y_sample, _, _ = _trunk(x_sample, mod_smp, True, state_gdn, state_hgrn, True, False, weights)
    return (y_prompt, y_sample, new_gdn, new_hgrn)
```

```python
import functools

import jax
import jax.numpy as jnp
from jax import lax
from jax.experimental import pallas as pl
from jax.experimental.pallas import tpu as pltpu

D_MODEL = 1024
N_HEADS = 8
HEAD_DIM = 128
KEY_DIM = N_HEADS * HEAD_DIM
CONV_W = 5
GDN_CHUNK = 64
HGRN_CHUNK = 16
GRID_W = 64
D_FF = 4 * D_MODEL
EPS = 1e-6
GDN_MAIN = 4 * KEY_DIM
GATE_LANES = 128
HGRN_PROJ = 5 * KEY_DIM
HGRN_OUT = 7 * KEY_DIM

ROW_TILE = 512
EPI_ROWS = 128
EPI_COLS = 256
PRE_ROWS = 256
HGRN_BLOCK = 256
HGRN_GROUP = 64
GDN_GROUP = 8
VMEM_LIMIT = 56 * 1024 * 1024

BF16 = jnp.bfloat16
F32 = jnp.float32

_NT = (((1,), (1,)), ((), ()))
_TN = (((0,), (0,)), ((), ()))


def _dot(a, b, dims=None):
    a = a.astype(BF16)
    b = b.astype(BF16)
    if dims is None:
        return jnp.dot(a, b, preferred_element_type=F32)
    return lax.dot_general(a, b, dims, preferred_element_type=F32)


def _segment_sums(x, seg):
    rows = x.shape[0]
    pos = lax.broadcasted_iota(jnp.int32, x.shape, 0) & (seg - 1)
    pre, suf = x, x
    s = 1
    while s < seg:
        pre = pre + jnp.where(pos >= s, pltpu.roll(pre, s, axis=0), 0.0)
        suf = suf + jnp.where(pos + s < seg, pltpu.roll(suf, rows - s, axis=0), 0.0)
        s *= 2
    return pre, suf


def _segment_prefix(x, seg):
    pos = lax.broadcasted_iota(jnp.int32, x.shape, 0) & (seg - 1)
    s = 1
    while s < seg:
        x = x + jnp.where(pos >= s, pltpu.roll(x, s, axis=0), 0.0)
        s *= 2
    return x


def _segment_pick(x, seg, which):
    rows, width = x.shape
    picked = x.reshape(rows // seg, seg, width)[:, which:which + 1, :]
    return jnp.broadcast_to(picked, (rows // seg, seg, width)).reshape(rows, width)


def _segment_last(x, seg):
    return _segment_pick(x, seg, seg - 1)


def _rms(x, g):
    return x * lax.rsqrt(jnp.mean(x * x, axis=-1, keepdims=True) + EPS) * g


def _silu(x):
    return x * jax.nn.sigmoid(x)


def _softplus(x):
    return jnp.maximum(x, 0.0) + jnp.log1p(jnp.exp(-jnp.abs(x)))


def _const_spec(shape):
    zeros = (0,) * len(shape)
    return pl.BlockSpec(shape, lambda *_: zeros, pipeline_mode=pl.Buffered(1))


def _mod_spec(col, tiles_per_seq):
    if tiles_per_seq is None:
        return pl.BlockSpec((None, 1, D_MODEL), lambda i: (0, 0, col))
    return pl.BlockSpec((None, 1, D_MODEL), lambda i: (i // tiles_per_seq, 0, col))


def _params(n_grid):
    return pltpu.CompilerParams(dimension_semantics=("arbitrary",) * n_grid, vmem_limit_bytes=VMEM_LIMIT)


def _ada_kernel(c_ref, w_ref, b_ref, o_ref):
    o_ref[...] = _dot(_silu(c_ref[...]), w_ref[...]) + b_ref[...]


def _ada_mod(cond, w_ada, b_ada):
    depth, _, n = w_ada.shape
    rows = cond.shape[0]
    tn = 1536
    return pl.pallas_call(
        _ada_kernel,
        grid=(depth, n // tn),
        in_specs=[pl.BlockSpec((rows, D_MODEL), lambda l, j: (0, 0)),
                  pl.BlockSpec((None, D_MODEL, tn), lambda l, j: (l, 0, j)),
                  pl.BlockSpec((None, 1, tn), lambda l, j: (l, 0, j))],
        out_specs=pl.BlockSpec((None, rows, tn), lambda l, j: (l, 0, j)),
        out_shape=jax.ShapeDtypeStruct((depth, rows, n), F32),
        compiler_params=_params(2),
        name="ada_mod",
    )(cond, w_ada, b_ada.reshape(depth, 1, n))


def _modulated(x_ref, g_ref, sh_ref, sc_ref):
    return (_rms(x_ref[...], g_ref[0:1, :]) * (1.0 + sc_ref[...]) + sh_ref[...]).astype(BF16)


def _gdn_inproj_kernel(x_ref, g_ref, sh_ref, sc_ref, w_ref, wg_ref, alog_ref, dt_ref, cw_ref, proj_ref, gate_ref,
                       *, seg):
    hb = _modulated(x_ref, g_ref, sh_ref, sc_ref)
    n_rows = hb.shape[0]
    p_rows = max(EPI_ROWS, min(seg, PRE_ROWS))
    p_cols = EPI_COLS
    period = min(seg, p_rows)
    pos = lax.broadcasted_iota(jnp.int32, (period, HEAD_DIM), 0) & (seg - 1)

    def tap_weights(w):
        taps = {}
        for s in (-2, -1, 1, 2):
            valid = (pos + s >= 0) & (pos + s < seg)
            taps[s] = jnp.tile(jnp.where(valid, w[s + 2:s + 3, :], 0.0), (p_rows // period, 1))
        return taps

    def unit(x, scale):
        return x * (lax.rsqrt(jnp.sum(x * x, axis=-1, keepdims=True) + EPS) * scale)

    pieces = [(j, r) for j in range(GDN_MAIN // p_cols) for r in range(n_rows // p_rows)]

    def product(piece):
        j, r = piece
        return jnp.dot(hb[r * p_rows:(r + 1) * p_rows], w_ref[:, j * p_cols:(j + 1) * p_cols],
                       preferred_element_type=F32)

    def finish(piece, y):
        j, r = piece
        rows = slice(r * p_rows, (r + 1) * p_rows)
        kind = j * p_cols // KEY_DIM
        if kind == 3:
            proj_ref[rows, j * p_cols:(j + 1) * p_cols] = _silu(y)
            return
        for hh in range(p_cols // HEAD_DIM):
            c0 = j * p_cols + hh * HEAD_DIM
            w = cw_ref[:, c0:c0 + HEAD_DIM]
            taps = tap_weights(w)
            x = y[:, hh * HEAD_DIM:(hh + 1) * HEAD_DIM]
            acc = x * w[CONV_W // 2:CONV_W // 2 + 1, :]
            for s in (-2, -1, 1, 2):
                acc = acc + pltpu.roll(x, (-s) % p_rows, axis=0) * taps[s]
            out = _silu(acc)
            if kind == 0:
                out = unit(out, HEAD_DIM ** -0.5)
            elif kind == 1:
                out = unit(out, 1.0)
            proj_ref[rows, c0:c0 + HEAD_DIM] = out

    y_next = product(pieces[0])
    for n, piece in enumerate(pieces):
        y, y_next = y_next, (product(pieces[n + 1]) if n + 1 < len(pieces) else None)
        finish(piece, y)
    raw = jnp.dot(hb, wg_ref[...], preferred_element_type=F32)
    beta = jax.nn.sigmoid(raw)
    g = -jnp.exp(alog_ref[...]) * _softplus(raw + dt_ref[...])
    gp, gs = _segment_sums(g, GDN_CHUNK)
    lane = lax.broadcasted_iota(jnp.int32, raw.shape, 1)
    gate_ref[...] = jnp.where(lane < 2 * N_HEADS, beta,
                              jnp.where(lane < 3 * N_HEADS, gp,
                                        jnp.where(lane < 4 * N_HEADS, gs, 0.0)))


def _gdn_inproj(x2d, norm_g, mod, tiles_per_seq, w_main, w_gate, alog_row, dt_row, conv_w, seg):
    rows = x2d.shape[0]
    assert PRE_ROWS % seg == 0 and ROW_TILE % PRE_ROWS == 0
    return pl.pallas_call(
        functools.partial(_gdn_inproj_kernel, seg=seg),
        grid=(rows // ROW_TILE,),
        in_specs=[pl.BlockSpec((ROW_TILE, D_MODEL), lambda i: (i, 0)),
                  _const_spec((4, D_MODEL)),
                  _mod_spec(0, tiles_per_seq), _mod_spec(1, tiles_per_seq),
                  _const_spec((D_MODEL, GDN_MAIN)), _const_spec((D_MODEL, GATE_LANES)),
                  _const_spec((1, GATE_LANES)), _const_spec((1, GATE_LANES)),
                  _const_spec((CONV_W, 3 * KEY_DIM))],
        out_specs=[pl.BlockSpec((ROW_TILE, GDN_MAIN), lambda i: (i, 0)),
                   pl.BlockSpec((ROW_TILE, GATE_LANES), lambda i: (i, 0))],
        out_shape=[jax.ShapeDtypeStruct((rows, GDN_MAIN), F32),
                   jax.ShapeDtypeStruct((rows, GATE_LANES), F32)],
        compiler_params=_params(1),
        name="gdn_inproj",
    )(x2d, norm_g, mod, mod, w_main, w_gate, alog_row, dt_row, conv_w)


def _hgrn_inproj_kernel(x_ref, g_ref, sh_ref, sc_ref, w_ref, lbl_ref, out_ref, *, layer):
    hb = _modulated(x_ref, g_ref, sh_ref, sc_ref)
    n_rows = hb.shape[0]
    c = HGRN_CHUNK
    depth = lbl_ref.shape[0]
    logits = [lbl_ref[l] for l in range(depth)]
    top = functools.reduce(jnp.maximum, logits)
    ex = [jnp.exp(l - top) for l in logits]
    denom = functools.reduce(lambda x, y: x + y, ex)
    lb = functools.reduce(lambda x, y: x + y, ex[:layer + 1]) / denom - ex[0] / denom

    pieces = [(j, r) for j in range(HGRN_PROJ // EPI_COLS) for r in range(n_rows // EPI_ROWS)]

    def product(piece):
        j, r = piece
        return jnp.dot(hb[r * EPI_ROWS:(r + 1) * EPI_ROWS], w_ref[:, j * EPI_COLS:(j + 1) * EPI_COLS],
                       preferred_element_type=F32)

    def finish(piece, y):
        j, r = piece
        rows = slice(r * EPI_ROWS, (r + 1) * EPI_ROWS)
        kind, off = divmod(j * EPI_COLS, KEY_DIM)
        if kind in (0, 3, 4):
            dst = {0: 0, 3: 5, 4: 6}[kind] * KEY_DIM + off
            out_ref[rows, dst:dst + EPI_COLS] = y if kind == 3 else _silu(y)
            return
        d = kind - 1
        for hh in range(EPI_COLS // HEAD_DIM):
            c0 = off + hh * HEAD_DIM
            lbd = lb[d:d + 1, c0:c0 + HEAD_DIM]
            fg = lbd + (1.0 - lbd) * jax.nn.sigmoid(y[:, hh * HEAD_DIM:(hh + 1) * HEAD_DIM])
            logf = jnp.log(fg)
            run = _segment_prefix(logf, c)
            if d == 1:
                run = _segment_last(run, c) - run + logf
            out_ref[rows, (1 + 2 * d) * KEY_DIM + c0:(1 + 2 * d) * KEY_DIM + c0 + HEAD_DIM] = 1.0 - fg
            out_ref[rows, (2 + 2 * d) * KEY_DIM + c0:(2 + 2 * d) * KEY_DIM + c0 + HEAD_DIM] = jnp.exp(run)

    y_next = product(pieces[0])
    for n, piece in enumerate(pieces):
        y, y_next = y_next, (product(pieces[n + 1]) if n + 1 < len(pieces) else None)
        finish(piece, y)


def _hgrn_inproj(x2d, norm_g, mod, tiles_per_seq, w_in, lb_logits, layer):
    rows = x2d.shape[0]
    depth = lb_logits.shape[0]
    return pl.pallas_call(
        functools.partial(_hgrn_inproj_kernel, layer=layer),
        grid=(rows // ROW_TILE,),
        in_specs=[pl.BlockSpec((ROW_TILE, D_MODEL), lambda i: (i, 0)),
                  _const_spec((4, D_MODEL)),
                  _mod_spec(0, tiles_per_seq), _mod_spec(1, tiles_per_seq),
                  _const_spec((D_MODEL, HGRN_PROJ)), _const_spec((depth, 2, KEY_DIM))],
        out_specs=pl.BlockSpec((ROW_TILE, HGRN_OUT), lambda i: (i, 0)),
        out_shape=jax.ShapeDtypeStruct((rows, HGRN_OUT), F32),
        compiler_params=_params(1),
        name="hgrn_inproj",
    )(x2d, norm_g, mod, mod, w_in, lb_logits)


def _split_bf16(x):
    hi = lax.bitcast_convert_type(lax.bitcast_convert_type(x, jnp.int32) & jnp.int32(-65536), F32)
    return hi, x - hi


def _pair_inverses(lows, left, diag2, fill):
    c = lows[0].shape[0]

    def block_diag(x):
        return jnp.concatenate([jnp.where(left, x, 0.0), jnp.where(left, 0.0, x)], axis=0)

    def left_operand(hi, lo):
        return jnp.concatenate([hi.astype(BF16), lo.astype(BF16)] * 2, axis=1)

    def right_operand(hi, lo):
        bh, bl = block_diag(hi).astype(BF16), block_diag(lo).astype(BF16)
        return jnp.concatenate([bh, bh, bl, bl], axis=0)

    accs = [jnp.where(diag2, 1.0, 0.0) - x for x in lows]
    parts = [_split_bf16(x) for x in lows]
    powers = [jnp.dot(left_operand(hi, lo), right_operand(hi, lo), preferred_element_type=F32) for hi, lo in parts]
    fill(0)
    levels = c.bit_length() - 2
    for level in range(levels):
        parts = [_split_bf16(x) for x in powers]
        rhs = [right_operand(hi, lo) for hi, lo in parts]
        acc_lhs = [left_operand(*_split_bf16(a)) for a in accs]
        if level + 1 < levels:
            res = [jnp.dot(jnp.concatenate([left_operand(hi, lo), al], axis=0), r, preferred_element_type=F32)
                   for (hi, lo), al, r in zip(parts, acc_lhs, rhs)]
            fill(level + 1)
            powers = [x[:c] for x in res]
            accs = [a + x[c:] for a, x in zip(accs, res)]
        else:
            accs = [a + jnp.dot(al, r, preferred_element_type=F32) for a, al, r in zip(accs, acc_lhs, rhs)]
    return accs


def _gdn_kernel(qn, kn, vn, gcol_ref, grow_ref, *rest, has_init, emit_state, group):
    rest = list(rest)
    s0_ref = rest.pop(0) if has_init else None
    o_ref = rest.pop(0)
    sfin_ref = rest.pop(0) if emit_state else None
    ob, m_s, b_s, q_s, gl_s, st = rest
    t_len = qn.shape[0]
    c = GDN_CHUNK
    n_chunks = t_len // c
    h = pl.program_id(1)

    lane = lax.broadcasted_iota(jnp.int32, (c, 2 * c), 1)
    row = lax.broadcasted_iota(jnp.int32, (c, 2 * c), 0)
    left = lane < c
    ahead = jnp.where(left, row - lane, lane - c - row)
    left_row = lax.broadcasted_iota(jnp.int32, (1, 2 * c), 1) < c

    def column(tile, idx):
        return jnp.sum(jnp.where(lane == idx, tile, 0.0), axis=-1, keepdims=True)

    def direction_blocks(a0, a1):
        z0, z1 = jnp.zeros(a0.shape, BF16), jnp.zeros(a1.shape, BF16)
        return jnp.concatenate([jnp.concatenate([a0.astype(BF16), z1], axis=1),
                                jnp.concatenate([z0, a1.astype(BF16)], axis=1)], axis=0)

    def load(n):
        cidx = [n, n_chunks - 1 - n]
        rows = [pl.ds(pl.multiple_of(ci * c, c), c) for ci in cidx]
        e = dict(n=n, rows=rows, q=[qn[r, :] for r in rows], k=[kn[r, :] for r in rows], v=[vn[r, :] for r in rows])
        gtile = [gcol_ref[r, :] for r in rows]
        e["beta"] = [column(gtile[d], d * N_HEADS + h) for d in range(2)]
        e["gc"] = [column(gtile[d], (2 + d) * N_HEADS + h) for d in range(2)]
        e["g_last"] = [e["gc"][0][c - 1:c, :], e["gc"][1][0:1, :]]
        gr2 = jnp.where(left_row, grow_ref[cidx[0], pl.ds(h, 1), :], grow_ref[cidx[1], pl.ds(h, 1), :])
        e["decay2"] = jnp.exp(jnp.where(ahead >= 0, jnp.where(left, e["gc"][0], e["gc"][1]) - gr2, -jnp.inf))
        return e

    def gram(e):
        k, q = e["k"], e["q"]
        kq = _dot(jnp.concatenate([jnp.concatenate(k, axis=1), jnp.concatenate(q, axis=1)], axis=0),
                  direction_blocks(k[0], k[1]), _NT)
        e["low2"] = jnp.where(ahead > 0, kq[:c] * jnp.where(left, e["beta"][0], e["beta"][1]) * e["decay2"], 0.0)
        e["qk2"] = kq[c:] * e["decay2"]

    def solve(e, t2):
        k, v, beta, gc = e["k"], e["v"], e["beta"], e["gc"]
        e["eg"] = [jnp.exp(gc[d]) for d in range(2)]
        rhs = [jnp.concatenate([v[d] * beta[d], k[d] * beta[d] * e["eg"][d]], axis=1) for d in range(2)]
        e["uw"] = jnp.dot(t2.astype(BF16), direction_blocks(*rhs), preferred_element_type=F32)

    def fold(e):
        uw = e["uw"]
        r2 = direction_blocks(uw[:, :2 * HEAD_DIM], uw[:, 2 * HEAD_DIM:])
        e["oq"] = jnp.dot(e["qk2"].astype(BF16), r2, preferred_element_type=F32)
        kt = jnp.concatenate([e["k"][d] * jnp.exp(e["g_last"][d] - e["gc"][d]) for d in range(2)], axis=0)
        e["bm"] = _dot(kt, r2, _TN)

    def store(e):
        n, oq, bm = e["n"], e["oq"], e["bm"]
        for d in range(2):
            base = 2 * d * HEAD_DIM
            ob[d, e["rows"][d], :] = oq[:, base:base + HEAD_DIM]
            q_s[n, d] = (e["q"][d] * e["eg"][d] - oq[:, base + HEAD_DIM:base + 2 * HEAD_DIM]).astype(BF16)
            b_s[n, d] = bm[:, base:base + HEAD_DIM]
            m_s[n, d] = bm[:, base + HEAD_DIM:base + 2 * HEAD_DIM].astype(BF16)
            gl_s[n, d] = jnp.broadcast_to(jnp.exp(e["g_last"][d]), (1, HEAD_DIM))

    def scan_step(n):
        for d in range(2):
            rows = pl.ds(pl.multiple_of((n if d == 0 else n_chunks - 1 - n) * c, c), c)
            state = st[d]
            ms = jnp.dot(jnp.concatenate([m_s[n, d], q_s[n, d]], axis=0), state.astype(BF16),
                         preferred_element_type=F32)
            ob[d, rows, :] = ob[d, rows, :] + ms[HEAD_DIM:]
            st[d] = state * gl_s[n, d] - ms[:HEAD_DIM] + b_s[n, d]

    def prepare(first, scan_first):
        pending = [] if scan_first is None else [scan_first + j for j in range(group)]
        n_slots = 8

        def fill(slot):
            for j in range(slot * len(pending) // n_slots, (slot + 1) * len(pending) // n_slots):
                scan_step(pending[j])

        chunks = [load(first + j) for j in range(group)]
        for e in chunks:
            gram(e)
        fill(0)
        inverses = _pair_inverses([e["low2"] for e in chunks], left, ahead == 0, lambda level: fill(1 + level))
        for e, t2 in zip(chunks, inverses):
            solve(e, t2)
        fill(6)
        for e in chunks:
            fold(e)
        fill(7)
        for e in chunks:
            store(e)

    for d in range(2):
        st[d] = s0_ref[d] if has_init else jnp.zeros((HEAD_DIM, HEAD_DIM), F32)

    n_steps = n_chunks // group
    prepare(0, None)

    def step(i, carry):
        prepare(i * group, (i - 1) * group)
        return carry

    lax.fori_loop(1, n_steps, step, 0)
    for j in range(group):
        scan_step((n_steps - 1) * group + j)

    def post(i, carry):
        rows = pl.ds(pl.multiple_of(i * PRE_ROWS, PRE_ROWS), PRE_ROWS)
        o_ref[rows, :] = ob[0, rows, :] + ob[1, rows, :]
        return carry

    lax.fori_loop(0, t_len // PRE_ROWS, post, 0)
    if emit_state:
        for d in range(2):
            sfin_ref[d] = st[d]


def _gdn_mixer(proj, gates, s0, layer_idx, emit_state):
    b, t, _ = proj.shape
    c = GDN_CHUNK
    n_chunks = t // c
    assert t % PRE_ROWS == 0 and 2 * c == HEAD_DIM
    as_rows = lambda g: g.reshape(b, n_chunks, c, N_HEADS).transpose(0, 1, 3, 2)
    grow = jnp.concatenate([as_rows(gates[:, :, 2 * N_HEADS:3 * N_HEADS]),
                            as_rows(gates[:, :, 3 * N_HEADS:4 * N_HEADS])], axis=-1)
    has_init = s0 is not None
    col = lambda off: pl.BlockSpec((None, t, HEAD_DIM), lambda i, h: (i, 0, off + h))
    in_specs = [col(0), col(N_HEADS), col(2 * N_HEADS),
                pl.BlockSpec((None, t, GATE_LANES), lambda i, h: (i, 0, 0)),
                pl.BlockSpec((None, n_chunks, N_HEADS, 2 * c), lambda i, h: (i, 0, 0, 0))]
    args = [proj, proj, proj, gates, grow]
    if has_init:
        in_specs.append(pl.BlockSpec((None, None, 2, None, HEAD_DIM, HEAD_DIM),
                                     lambda i, h: (i, layer_idx, 0, h, 0, 0)))
        args.append(s0)
    out_specs = [pl.BlockSpec((None, t, HEAD_DIM), lambda i, h: (i, 0, h))]
    out_shape = [jax.ShapeDtypeStruct((b, t, KEY_DIM), F32)]
    if emit_state:
        out_specs.append(pl.BlockSpec((None, 2, None, HEAD_DIM, HEAD_DIM), lambda i, h: (i, 0, h, 0, 0)))
        out_shape.append(jax.ShapeDtypeStruct((b, 2, N_HEADS, HEAD_DIM, HEAD_DIM), F32))
    scratch = [
        pltpu.VMEM((2, t, HEAD_DIM), F32),
        pltpu.VMEM((n_chunks, 2, HEAD_DIM, HEAD_DIM), BF16),
        pltpu.VMEM((n_chunks, 2, HEAD_DIM, HEAD_DIM), F32),
        pltpu.VMEM((n_chunks, 2, c, HEAD_DIM), BF16),
        pltpu.VMEM((n_chunks, 2, 1, HEAD_DIM), F32),
        pltpu.VMEM((2, HEAD_DIM, HEAD_DIM), F32)]
    res = pl.pallas_call(
        functools.partial(_gdn_kernel, has_init=has_init, emit_state=emit_state, group=min(GDN_GROUP, n_chunks)),
        grid=(b, N_HEADS),
        in_specs=in_specs,
        out_specs=out_specs,
        out_shape=out_shape,
        scratch_shapes=scratch,
        compiler_params=_params(2),
        name="gdn_mixer",
    )(*args)
    return res[0], (res[1] if emit_state else None)


def _hgrn_kernel(q_ref, kf_ref, ef_ref, kb_ref, eb_ref, i_ref, *rest, has_init, emit_state):
    rest = list(rest)
    s0_ref = rest.pop(0) if has_init else None
    o_ref = rest.pop(0)
    sfin_ref = rest.pop(0) if emit_state else None
    qin, kout, ktail, ftot, ob, st = rest
    t_len = q_ref.shape[0]
    c = HGRN_CHUNK

    def pre(d, rows, q):
        k = (kf_ref, kb_ref)[d][rows, :]
        e = (ef_ref, eb_ref)[d][rows, :]
        whole = _segment_pick(e, c, c - 1 if d == 0 else 0)
        k_over_e = k / e
        qin[d, rows, :] = q * e
        kout[d, rows, :] = k_over_e.astype(BF16)
        ktail[d, rows, :] = k_over_e * whole
        ftot[d, rows, :] = whole

    def pre_body(i, carry):
        rows = pl.ds(pl.multiple_of(i * HGRN_BLOCK, HGRN_BLOCK), HGRN_BLOCK)
        q = q_ref[rows, :]
        pre(0, rows, q)
        pre(1, rows, q)
        return carry

    lax.fori_loop(0, t_len // HGRN_BLOCK, pre_body, 0)

    for d in range(2):
        st[d] = s0_ref[d].T if has_init else jnp.zeros((HEAD_DIM, HEAD_DIM), F32)

    grp = HGRN_GROUP
    per = grp // c
    assert per == 4
    blk = HGRN_BLOCK
    ri = lax.broadcasted_iota(jnp.int32, (grp, grp), 0)
    ci = lax.broadcasted_iota(jnp.int32, (grp, grp), 1)
    row_chunk = lax.broadcasted_iota(jnp.int32, (grp, HEAD_DIM), 0) // c

    def group_factors(f, d):
        r1, r2, r3 = (pltpu.roll(f, s * c, axis=0) for s in (1, 2, 3))
        prev, nxt = ((r1, r2, r3), (r3, r2, r1)) if d == 0 else ((r3, r2, r1), (r1, r2, r3))
        order = row_chunk if d == 0 else per - 1 - row_chunk
        g1, h1 = prev[0], nxt[0]
        g2, h2 = g1 * prev[1], h1 * nxt[1]
        g3, h3 = g2 * prev[2], h2 * nxt[2]
        before = jnp.where(order == 0, 1.0, jnp.where(order == 1, g1, jnp.where(order == 2, g2, g3)))
        after = jnp.where(order == 3, 1.0, jnp.where(order == 2, h1, jnp.where(order == 1, h2, h3)))
        whole = f[0:1] * f[c:c + 1] * f[2 * c:2 * c + 1] * f[3 * c:3 * c + 1]
        return g1, g2, before, after, whole

    def att_select(d, p1, p234):
        dist = (ri // c - ci // c) if d == 0 else (ci // c - ri // c)
        inside = (ci <= ri) if d == 0 else (ci >= ri)
        return jnp.where((dist == 0) & inside, p1,
                         jnp.where(dist == 1, p234[:grp],
                                   jnp.where(dist == 2, p234[grp:2 * grp],
                                             jnp.where(dist == 3, p234[2 * grp:], 0.0))))

    def body(i, carry):
        ctx = []
        for step in range(blk // grp):
            for d in range(2):
                g_idx = i * (blk // grp) + step
                r0 = pl.multiple_of((g_idx if d == 0 else t_len // grp - 1 - g_idx) * grp, grp)
                rows = pl.ds(r0, grp)
                e = dict(d=d, rows=rows, qi=qin[d, rows, :], ko=kout[d, rows, :], kt=ktail[d, rows, :],
                         v=i_ref[rows, :].astype(BF16))
                e["g1"], e["g2"], e["before"], e["after"], e["whole"] = group_factors(ftot[d, rows, :], d)
                ctx.append(e)
        for e in ctx:
            qi = e["qi"]
            e["p1"] = _dot(qi, e["ko"], _NT)
            e["p234"] = _dot(jnp.concatenate([qi, qi * e["g1"], qi * e["g2"]], axis=0), e["kt"], _NT)
        for e in ctx:
            e["ds"] = _dot(e["v"], e["kt"] * e["after"], _TN)
        for e in ctx:
            e["intra"] = _dot(att_select(e["d"], e["p1"], e["p234"]), e["v"])
        states = [st[0], st[1]]
        for e in ctx:
            d = e["d"]
            ob[d, e["rows"], :] = e["intra"] + _dot(e["qi"] * e["before"], states[d], _NT)
            states[d] = states[d] * e["whole"] + e["ds"]
        st[0], st[1] = states
        return carry

    lax.fori_loop(0, t_len // blk, body, 0)

    def post(i, carry):
        rows = pl.ds(pl.multiple_of(i * PRE_ROWS, PRE_ROWS), PRE_ROWS)
        o_ref[rows, :] = ob[0, rows, :] + ob[1, rows, :]
        return carry

    lax.fori_loop(0, t_len // PRE_ROWS, post, 0)
    if emit_state:
        for d in range(2):
            sfin_ref[d] = st[d].T


def _hgrn_mixer(proj, s0, layer_idx, emit_state):
    b, t, _ = proj.shape
    has_init = s0 is not None
    col = lambda off: pl.BlockSpec((None, t, HEAD_DIM), lambda i, h: (i, 0, off + h))
    in_specs = [col(j * N_HEADS) for j in range(6)]
    args = [proj] * 6
    if has_init:
        in_specs.append(pl.BlockSpec((None, None, 2, None, HEAD_DIM, HEAD_DIM),
                                     lambda i, h: (i, layer_idx, 0, h, 0, 0)))
        args.append(s0)
    out_specs = [pl.BlockSpec((None, t, HEAD_DIM), lambda i, h: (i, 0, h))]
    out_shape = [jax.ShapeDtypeStruct((b, t, KEY_DIM), F32)]
    if emit_state:
        out_specs.append(pl.BlockSpec((None, 2, None, HEAD_DIM, HEAD_DIM), lambda i, h: (i, 0, h, 0, 0)))
        out_shape.append(jax.ShapeDtypeStruct((b, 2, N_HEADS, HEAD_DIM, HEAD_DIM), F32))
    res = pl.pallas_call(
        functools.partial(_hgrn_kernel, has_init=has_init, emit_state=emit_state),
        grid=(b, N_HEADS),
        in_specs=in_specs,
        out_specs=out_specs,
        out_shape=out_shape,
        scratch_shapes=[pltpu.VMEM((2, t, HEAD_DIM), F32), pltpu.VMEM((2, t, HEAD_DIM), BF16),
                        pltpu.VMEM((2, t, HEAD_DIM), F32), pltpu.VMEM((2, t, HEAD_DIM), F32),
                        pltpu.VMEM((2, t, HEAD_DIM), F32), pltpu.VMEM((2, HEAD_DIM, HEAD_DIM), F32)],
        compiler_params=_params(2),
        name="hgrn_mixer",
    )(*args)
    return res[0], (res[1] if emit_state else None)


def _post_kernel(x_ref, o_ref, z_ref, on_ref, g_ref, gt1_ref, sh2_ref, sc2_ref, gt2_ref, wo_ref, w1_ref, w2_ref,
                 y_ref):
    rows = x_ref.shape[0]
    n_parts = 2
    part = rows // n_parts
    mixes = []
    for r in range(n_parts):
        rs = slice(r * part, (r + 1) * part)
        gated = [(_rms(o_ref[rs, h * HEAD_DIM:(h + 1) * HEAD_DIM], on_ref[...])
                  * z_ref[rs, h * HEAD_DIM:(h + 1) * HEAD_DIM]).astype(BF16) for h in range(N_HEADS)]
        mixes.append(jnp.dot(jnp.concatenate(gated, axis=1), wo_ref[...], preferred_element_type=F32))
    mix = jnp.concatenate(mixes, axis=0)
    x1 = x_ref[...] + gt1_ref[...] * _rms(mix, g_ref[1:2, :])
    hb = (_rms(x1, g_ref[2:3, :]) * (1.0 + sc2_ref[...]) + sh2_ref[...]).astype(BF16)
    ff = jnp.zeros(x1.shape, F32)
    for j in range(D_FF // 1024):
        cols = slice(j * 1024, (j + 1) * 1024)
        hid = jnp.maximum(jnp.dot(hb, w1_ref[:, cols], preferred_element_type=F32), 0.0)
        ff = ff + jnp.dot((hid * hid).astype(BF16), w2_ref[cols, :], preferred_element_type=F32)
    y_ref[...] = x1 + gt2_ref[...] * _rms(ff, g_ref[3:4, :])


def _post_mixer(x2d, o2d, proj2d, z_block, onorm_g, norm_g, mod, tiles_per_seq, w_out, w1, w2):
    rows = x2d.shape[0]
    tile = pl.BlockSpec((ROW_TILE, D_MODEL), lambda i: (i, 0))
    return pl.pallas_call(
        _post_kernel,
        grid=(rows // ROW_TILE,),
        in_specs=[tile, tile, pl.BlockSpec((ROW_TILE, D_MODEL), lambda i: (i, z_block)),
                  _const_spec((1, HEAD_DIM)), _const_spec((4, D_MODEL)),
                  _mod_spec(2, tiles_per_seq), _mod_spec(3, tiles_per_seq),
                  _mod_spec(4, tiles_per_seq), _mod_spec(5, tiles_per_seq),
                  _const_spec((D_MODEL, D_MODEL)), _const_spec((D_MODEL, D_FF)), _const_spec((D_FF, D_MODEL))],
        out_specs=tile,
        out_shape=jax.ShapeDtypeStruct((rows, D_MODEL), F32),
        compiler_params=_params(1),
        name="post_mixer",
    )(x2d, o2d, proj2d, onorm_g.reshape(1, HEAD_DIM), norm_g, mod, mod, mod, mod, w_out, w1, w2)


def _trunk(x, mod_rows, per_seq_mod, s_gdn, s_hgrn, grid_conv, emit_state, weights):
    (norm_g, gdn_main, gdn_gate, gdn_alog, gdn_dt, gdn_conv_w, gdn_onorm_g, gdn_w_out,
     hgrn_w_in, hgrn_lb_logits, hgrn_onorm_g, hgrn_w_out, mlp_w1, mlp_w2) = weights
    b, t, _ = x.shape
    tiles_per_seq = t // ROW_TILE if per_seq_mod else None
    x2d = x.reshape(b * t, D_MODEL)
    depth = norm_g.shape[0]
    fin_gdn, fin_hgrn = [], []
    for layer in range(depth):
        j = layer // 2
        mod = mod_rows[layer]
        if layer % 2 == 0:
            proj, gates = _gdn_inproj(x2d, norm_g[layer], mod, tiles_per_seq, gdn_main[j], gdn_gate[j],
                                      gdn_alog[j], gdn_dt[j], gdn_conv_w[j], GRID_W if grid_conv else t)
            o, fin = _gdn_mixer(proj.reshape(b, t, GDN_MAIN), gates.reshape(b, t, GATE_LANES), s_gdn, j, emit_state)
            fin_gdn.append(fin)
            z_block, onorm_g, w_out = 3, gdn_onorm_g[j], gdn_w_out[j]
        else:
            proj = _hgrn_inproj(x2d, norm_g[layer], mod, tiles_per_seq, hgrn_w_in[j], hgrn_lb_logits, layer)
            o, fin = _hgrn_mixer(proj.reshape(b, t, HGRN_OUT), s_hgrn, j, emit_state)
            fin_hgrn.append(fin)
            z_block, onorm_g, w_out = 6, hgrn_onorm_g[j], hgrn_w_out[j]
        x2d = _post_mixer(x2d, o.reshape(b * t, KEY_DIM), proj, z_block, onorm_g, norm_g[layer], mod, tiles_per_seq,
                          w_out, mlp_w1[layer], mlp_w2[layer])
    y = x2d.reshape(b, t, D_MODEL)
    if emit_state:
        return y, jnp.stack(fin_gdn, axis=1), jnp.stack(fin_hgrn, axis=1)
    return y, None, None


def kernel(x_prompt, x_sample, state_gdn, state_hgrn, c, c_ctx, w_ada, b_ada, norm_g, gdn_w_in, gdn_conv_w,
           gdn_a_log, gdn_dt_bias, gdn_onorm_g, gdn_w_out, hgrn_w_in, hgrn_lb_logits, hgrn_onorm_g, hgrn_w_out,
           mlp_w1, mlp_w2):
    n_dec = c.shape[0]
    n_rows = 16
    cond = jnp.concatenate([c_ctx[None, :], c, jnp.zeros((n_rows - 1 - n_dec, D_MODEL), F32)], axis=0)
    mod = _ada_mod(cond, w_ada, b_ada)
    mod_ctx = mod[:, 0:1, None, :]
    mod_smp = mod[:, 1:1 + n_dec, None, :]

    n_gdn = gdn_w_in.shape[0]
    gate_pad = GATE_LANES - 4 * N_HEADS
    gdn_gate = jnp.pad(gdn_w_in[:, :, GDN_MAIN:], ((0, 0), (0, 0), (0, gate_pad))).astype(BF16)
    lead = jnp.zeros((n_gdn, 2 * N_HEADS), F32)
    tail = jnp.zeros((n_gdn, gate_pad), F32)
    gdn_alog = jnp.concatenate([lead, gdn_a_log.reshape(n_gdn, 2 * N_HEADS), tail], axis=1)[:, None, :]
    gdn_dt = jnp.concatenate([lead, gdn_dt_bias.reshape(n_gdn, 2 * N_HEADS), tail], axis=1)[:, None, :]
    weights = (norm_g, gdn_w_in[:, :, :GDN_MAIN].astype(BF16), gdn_gate, gdn_alog, gdn_dt, gdn_conv_w,
               gdn_onorm_g, gdn_w_out.astype(BF16), hgrn_w_in.astype(BF16), hgrn_lb_logits, hgrn_onorm_g,
               hgrn_w_out.astype(BF16), mlp_w1.astype(BF16), mlp_w2.astype(BF16))

    y_prompt, new_gdn, new_hgrn = _trunk(x_prompt, mod_ctx, False, None, None, False, True, weights)
    y_sample, _, _ = _trunk(x_sample, mod_smp, True, state_gdn, state_hgrn, True, False, weights)
    return (y_prompt, y_sample, new_gdn, new_hgrn)
```

```python
import functools

import jax
import jax.numpy as jnp
from jax import lax
from jax.experimental import pallas as pl
from jax.experimental.pallas import tpu as pltpu

D_MODEL = 1024
N_HEADS = 8
HEAD_DIM = 128
KEY_DIM = N_HEADS * HEAD_DIM
CONV_W = 5
GDN_CHUNK = 64
HGRN_CHUNK = 16
GRID_W = 64
D_FF = 4 * D_MODEL
EPS = 1e-6
GDN_MAIN = 4 * KEY_DIM
GATE_LANES = 128
HGRN_PROJ = 5 * KEY_DIM
HGRN_OUT = 7 * KEY_DIM

ROW_TILE = 512
EPI_ROWS = 128
EPI_COLS = 256
PRE_ROWS = 256
HGRN_BLOCK = 256
HGRN_SCAN_ROWS = 512
HGRN_GROUP = 64
GDN_GROUP = 8
VMEM_LIMIT = 56 * 1024 * 1024

BF16 = jnp.bfloat16
F32 = jnp.float32

_NT = (((1,), (1,)), ((), ()))
_TN = (((0,), (0,)), ((), ()))


def _dot(a, b, dims=None):
    a = a.astype(BF16)
    b = b.astype(BF16)
    if dims is None:
        return jnp.dot(a, b, preferred_element_type=F32)
    return lax.dot_general(a, b, dims, preferred_element_type=F32)


def _segment_sums(x, seg):
    rows = x.shape[0]
    pos = lax.broadcasted_iota(jnp.int32, x.shape, 0) & (seg - 1)
    pre, suf = x, x
    s = 1
    while s < seg:
        pre = pre + jnp.where(pos >= s, pltpu.roll(pre, s, axis=0), 0.0)
        suf = suf + jnp.where(pos + s < seg, pltpu.roll(suf, rows - s, axis=0), 0.0)
        s *= 2
    return pre, suf


def _segment_prefix(x, seg):
    pos = lax.broadcasted_iota(jnp.int32, x.shape, 0) & (seg - 1)
    s = 1
    while s < seg:
        x = x + jnp.where(pos >= s, pltpu.roll(x, s, axis=0), 0.0)
        s *= 2
    return x


def _segment_pick(x, seg, which):
    rows, width = x.shape
    picked = x.reshape(rows // seg, seg, width)[:, which:which + 1, :]
    return jnp.broadcast_to(picked, (rows // seg, seg, width)).reshape(rows, width)


def _segment_last(x, seg):
    return _segment_pick(x, seg, seg - 1)


def _rms(x, g):
    return x * lax.rsqrt(jnp.mean(x * x, axis=-1, keepdims=True) + EPS) * g


def _silu(x):
    return x * jax.nn.sigmoid(x)


def _softplus(x):
    return jnp.maximum(x, 0.0) + jnp.log1p(jnp.exp(-jnp.abs(x)))


def _const_spec(shape):
    zeros = (0,) * len(shape)
    return pl.BlockSpec(shape, lambda *_: zeros, pipeline_mode=pl.Buffered(1))


def _mod_spec(col, tiles_per_seq):
    if tiles_per_seq is None:
        return pl.BlockSpec((None, 1, D_MODEL), lambda i: (0, 0, col))
    return pl.BlockSpec((None, 1, D_MODEL), lambda i: (i // tiles_per_seq, 0, col))


def _params(n_grid):
    return pltpu.CompilerParams(dimension_semantics=("arbitrary",) * n_grid, vmem_limit_bytes=VMEM_LIMIT)


def _ada_kernel(c_ref, w_ref, b_ref, o_ref):
    o_ref[...] = _dot(_silu(c_ref[...]), w_ref[...]) + b_ref[...]


def _ada_mod(cond, w_ada, b_ada):
    depth, _, n = w_ada.shape
    rows = cond.shape[0]
    tn = 1536
    return pl.pallas_call(
        _ada_kernel,
        grid=(depth, n // tn),
        in_specs=[pl.BlockSpec((rows, D_MODEL), lambda l, j: (0, 0)),
                  pl.BlockSpec((None, D_MODEL, tn), lambda l, j: (l, 0, j)),
                  pl.BlockSpec((None, 1, tn), lambda l, j: (l, 0, j))],
        out_specs=pl.BlockSpec((None, rows, tn), lambda l, j: (l, 0, j)),
        out_shape=jax.ShapeDtypeStruct((depth, rows, n), F32),
        compiler_params=_params(2),
        name="ada_mod",
    )(cond, w_ada, b_ada.reshape(depth, 1, n))


def _modulated(x_ref, g_ref, sh_ref, sc_ref):
    return (_rms(x_ref[...], g_ref[0:1, :]) * (1.0 + sc_ref[...]) + sh_ref[...]).astype(BF16)


def _gdn_inproj_kernel(x_ref, g_ref, sh_ref, sc_ref, w_ref, wg_ref, alog_ref, dt_ref, cw_ref, proj_ref, gate_ref,
                       *, seg):
    hb = _modulated(x_ref, g_ref, sh_ref, sc_ref)
    n_rows = hb.shape[0]
    p_rows = PRE_ROWS
    p_cols = EPI_COLS
    period = min(seg, p_rows)
    pos = lax.broadcasted_iota(jnp.int32, (period, HEAD_DIM), 0) & (seg - 1)

    def tap_weights(w):
        taps = {}
        for s in (-2, -1, 1, 2):
            valid = (pos + s >= 0) & (pos + s < seg)
            taps[s] = jnp.tile(jnp.where(valid, w[s + 2:s + 3, :], 0.0), (p_rows // period, 1))
        return taps

    def unit(x, scale):
        return x * (lax.rsqrt(jnp.sum(x * x, axis=-1, keepdims=True) + EPS) * scale)

    pieces = [(j, r) for j in range(GDN_MAIN // p_cols) for r in range(n_rows // p_rows)]

    def product(piece):
        j, r = piece
        return jnp.dot(hb[r * p_rows:(r + 1) * p_rows], w_ref[:, j * p_cols:(j + 1) * p_cols],
                       preferred_element_type=F32)

    def finish(piece, y):
        j, r = piece
        rows = slice(r * p_rows, (r + 1) * p_rows)
        kind = j * p_cols // KEY_DIM
        if kind == 3:
            proj_ref[rows, j * p_cols:(j + 1) * p_cols] = _silu(y)
            return
        for hh in range(p_cols // HEAD_DIM):
            c0 = j * p_cols + hh * HEAD_DIM
            w = cw_ref[:, c0:c0 + HEAD_DIM]
            taps = tap_weights(w)
            x = y[:, hh * HEAD_DIM:(hh + 1) * HEAD_DIM]
            acc = x * w[CONV_W // 2:CONV_W // 2 + 1, :]
            for s in (-2, -1, 1, 2):
                acc = acc + pltpu.roll(x, (-s) % p_rows, axis=0) * taps[s]
            out = _silu(acc)
            if kind == 0:
                out = unit(out, HEAD_DIM ** -0.5)
            elif kind == 1:
                out = unit(out, 1.0)
            proj_ref[rows, c0:c0 + HEAD_DIM] = out

    y_next = product(pieces[0])
    for n, piece in enumerate(pieces):
        y, y_next = y_next, (product(pieces[n + 1]) if n + 1 < len(pieces) else None)
        finish(piece, y)
    raw = jnp.dot(hb, wg_ref[...], preferred_element_type=F32)
    beta = jax.nn.sigmoid(raw)
    g = -jnp.exp(alog_ref[...]) * _softplus(raw + dt_ref[...])
    gp, gs = _segment_sums(g, GDN_CHUNK)
    lane = lax.broadcasted_iota(jnp.int32, raw.shape, 1)
    gate_ref[...] = jnp.where(lane < 2 * N_HEADS, beta,
                              jnp.where(lane < 3 * N_HEADS, gp,
                                        jnp.where(lane < 4 * N_HEADS, gs, 0.0)))


def _gdn_inproj(x2d, norm_g, mod, tiles_per_seq, w_main, w_gate, alog_row, dt_row, conv_w, seg):
    rows = x2d.shape[0]
    assert PRE_ROWS % seg == 0 and ROW_TILE % PRE_ROWS == 0
    return pl.pallas_call(
        functools.partial(_gdn_inproj_kernel, seg=seg),
        grid=(rows // ROW_TILE,),
        in_specs=[pl.BlockSpec((ROW_TILE, D_MODEL), lambda i: (i, 0)),
                  _const_spec((4, D_MODEL)),
                  _mod_spec(0, tiles_per_seq), _mod_spec(1, tiles_per_seq),
                  _const_spec((D_MODEL, GDN_MAIN)), _const_spec((D_MODEL, GATE_LANES)),
                  _const_spec((1, GATE_LANES)), _const_spec((1, GATE_LANES)),
                  _const_spec((CONV_W, 3 * KEY_DIM))],
        out_specs=[pl.BlockSpec((ROW_TILE, GDN_MAIN), lambda i: (i, 0)),
                   pl.BlockSpec((ROW_TILE, GATE_LANES), lambda i: (i, 0))],
        out_shape=[jax.ShapeDtypeStruct((rows, GDN_MAIN), F32),
                   jax.ShapeDtypeStruct((rows, GATE_LANES), F32)],
        compiler_params=_params(1),
        name="gdn_inproj",
    )(x2d, norm_g, mod, mod, w_main, w_gate, alog_row, dt_row, conv_w)


def _hgrn_inproj_kernel(x_ref, g_ref, sh_ref, sc_ref, w_ref, lbl_ref, out_ref, *, layer):
    hb = _modulated(x_ref, g_ref, sh_ref, sc_ref)
    n_rows = hb.shape[0]
    c = HGRN_CHUNK
    depth = lbl_ref.shape[0]
    logits = [lbl_ref[l] for l in range(depth)]
    top = functools.reduce(jnp.maximum, logits)
    ex = [jnp.exp(l - top) for l in logits]
    denom = functools.reduce(lambda x, y: x + y, ex)
    lb = functools.reduce(lambda x, y: x + y, ex[:layer + 1]) / denom - ex[0] / denom

    pieces = [(j, r) for j in range(HGRN_PROJ // EPI_COLS) for r in range(n_rows // EPI_ROWS)]

    def product(piece):
        j, r = piece
        return jnp.dot(hb[r * EPI_ROWS:(r + 1) * EPI_ROWS], w_ref[:, j * EPI_COLS:(j + 1) * EPI_COLS],
                       preferred_element_type=F32)

    def finish(piece, y):
        j, r = piece
        rows = slice(r * EPI_ROWS, (r + 1) * EPI_ROWS)
        kind, off = divmod(j * EPI_COLS, KEY_DIM)
        if kind in (0, 3, 4):
            dst = {0: 0, 3: 5, 4: 6}[kind] * KEY_DIM + off
            out_ref[rows, dst:dst + EPI_COLS] = y if kind == 3 else _silu(y)
            return
        d = kind - 1
        for hh in range(EPI_COLS // HEAD_DIM):
            c0 = off + hh * HEAD_DIM
            lbd = lb[d:d + 1, c0:c0 + HEAD_DIM]
            fg = lbd + (1.0 - lbd) * jax.nn.sigmoid(y[:, hh * HEAD_DIM:(hh + 1) * HEAD_DIM])
            logf = jnp.log(fg)
            run = _segment_prefix(logf, c)
            if d == 1:
                run = _segment_last(run, c) - run + logf
            out_ref[rows, (1 + 2 * d) * KEY_DIM + c0:(1 + 2 * d) * KEY_DIM + c0 + HEAD_DIM] = 1.0 - fg
            out_ref[rows, (2 + 2 * d) * KEY_DIM + c0:(2 + 2 * d) * KEY_DIM + c0 + HEAD_DIM] = jnp.exp(run)

    y_next = product(pieces[0])
    for n, piece in enumerate(pieces):
        y, y_next = y_next, (product(pieces[n + 1]) if n + 1 < len(pieces) else None)
        finish(piece, y)


def _hgrn_inproj(x2d, norm_g, mod, tiles_per_seq, w_in, lb_logits, layer):
    rows = x2d.shape[0]
    depth = lb_logits.shape[0]
    return pl.pallas_call(
        functools.partial(_hgrn_inproj_kernel, layer=layer),
        grid=(rows // ROW_TILE,),
        in_specs=[pl.BlockSpec((ROW_TILE, D_MODEL), lambda i: (i, 0)),
                  _const_spec((4, D_MODEL)),
                  _mod_spec(0, tiles_per_seq), _mod_spec(1, tiles_per_seq),
                  _const_spec((D_MODEL, HGRN_PROJ)), _const_spec((depth, 2, KEY_DIM))],
        out_specs=pl.BlockSpec((ROW_TILE, HGRN_OUT), lambda i: (i, 0)),
        out_shape=jax.ShapeDtypeStruct((rows, HGRN_OUT), F32),
        compiler_params=_params(1),
        name="hgrn_inproj",
    )(x2d, norm_g, mod, mod, w_in, lb_logits)


def _split_bf16(x):
    hi = lax.bitcast_convert_type(lax.bitcast_convert_type(x, jnp.int32) & jnp.int32(-65536), F32)
    return hi, x - hi


def _pair_inverses(lows, left, diag2, fill):
    c = lows[0].shape[0]

    def block_diag(x):
        return jnp.concatenate([jnp.where(left, x, 0.0), jnp.where(left, 0.0, x)], axis=0)

    def left_operand(hi, lo):
        return jnp.concatenate([hi.astype(BF16), lo.astype(BF16)] * 2, axis=1)

    def right_operand(hi, lo):
        bh, bl = block_diag(hi).astype(BF16), block_diag(lo).astype(BF16)
        return jnp.concatenate([bh, bh, bl, bl], axis=0)

    accs = [jnp.where(diag2, 1.0, 0.0) - x for x in lows]
    parts = [_split_bf16(x) for x in lows]
    powers = [jnp.dot(left_operand(hi, lo), right_operand(hi, lo), preferred_element_type=F32) for hi, lo in parts]
    fill(0)
    levels = c.bit_length() - 2
    for level in range(levels):
        parts = [_split_bf16(x) for x in powers]
        rhs = [right_operand(hi, lo) for hi, lo in parts]
        acc_lhs = [left_operand(*_split_bf16(a)) for a in accs]
        if level + 1 < levels:
            res = [jnp.dot(jnp.concatenate([left_operand(hi, lo), al], axis=0), r, preferred_element_type=F32)
                   for (hi, lo), al, r in zip(parts, acc_lhs, rhs)]
            fill(level + 1)
            powers = [x[:c] for x in res]
            accs = [a + x[c:] for a, x in zip(accs, res)]
        else:
            accs = [a + jnp.dot(al, r, preferred_element_type=F32) for a, al, r in zip(accs, acc_lhs, rhs)]
    return accs


def _gdn_kernel(qn, kn, vn, gcol_ref, grow_ref, *rest, has_init, emit_state, group):
    rest = list(rest)
    s0_ref = rest.pop(0) if has_init else None
    o_ref = rest.pop(0)
    sfin_ref = rest.pop(0) if emit_state else None
    ob, m_s, b_s, q_s, gl_s, st = rest
    t_len = qn.shape[0]
    c = GDN_CHUNK
    n_chunks = t_len // c
    heads = qn.shape[1] // HEAD_DIM
    h0 = pl.program_id(1) * heads

    lane = lax.broadcasted_iota(jnp.int32, (c, 2 * c), 1)
    row = lax.broadcasted_iota(jnp.int32, (c, 2 * c), 0)
    left = lane < c
    ahead = jnp.where(left, row - lane, lane - c - row)
    left_row = lax.broadcasted_iota(jnp.int32, (1, 2 * c), 1) < c

    def column(tile, idx):
        return jnp.sum(jnp.where(lane == idx, tile, 0.0), axis=-1, keepdims=True)

    def direction_blocks(a0, a1):
        z0, z1 = jnp.zeros(a0.shape, BF16), jnp.zeros(a1.shape, BF16)
        return jnp.concatenate([jnp.concatenate([a0.astype(BF16), z1], axis=1),
                                jnp.concatenate([z0, a1.astype(BF16)], axis=1)], axis=0)

    def load(hh, n):
        h = h0 + hh
        cidx = [n, n_chunks - 1 - n]
        rows = [pl.ds(pl.multiple_of(ci * c, c), c) for ci in cidx]
        lanes = slice(hh * HEAD_DIM, (hh + 1) * HEAD_DIM)
        e = dict(hh=hh, n=n, rows=rows, q=[qn[r, lanes] for r in rows], k=[kn[r, lanes] for r in rows],
                 v=[vn[r, lanes] for r in rows])
        gtile = [gcol_ref[r, :] for r in rows]
        e["beta"] = [column(gtile[d], d * N_HEADS + h) for d in range(2)]
        e["gc"] = [column(gtile[d], (2 + d) * N_HEADS + h) for d in range(2)]
        e["g_last"] = [e["gc"][0][c - 1:c, :], e["gc"][1][0:1, :]]
        gr2 = jnp.where(left_row, grow_ref[cidx[0], pl.ds(h, 1), :], grow_ref[cidx[1], pl.ds(h, 1), :])
        e["decay2"] = jnp.exp(jnp.where(ahead >= 0, jnp.where(left, e["gc"][0], e["gc"][1]) - gr2, -jnp.inf))
        return e

    def gram(e):
        k, q = e["k"], e["q"]
        kq = _dot(jnp.concatenate([jnp.concatenate(k, axis=1), jnp.concatenate(q, axis=1)], axis=0),
                  direction_blocks(k[0], k[1]), _NT)
        e["low2"] = jnp.where(ahead > 0, kq[:c] * jnp.where(left, e["beta"][0], e["beta"][1]) * e["decay2"], 0.0)
        e["qk2"] = kq[c:] * e["decay2"]

    def solve(e, t2):
        k, v, beta, gc = e["k"], e["v"], e["beta"], e["gc"]
        e["eg"] = [jnp.exp(gc[d]) for d in range(2)]
        rhs = [jnp.concatenate([v[d] * beta[d], k[d] * beta[d] * e["eg"][d]], axis=1) for d in range(2)]
        e["uw"] = jnp.dot(t2.astype(BF16), direction_blocks(*rhs), preferred_element_type=F32)

    def fold(e):
        uw = e["uw"]
        r2 = direction_blocks(uw[:, :2 * HEAD_DIM], uw[:, 2 * HEAD_DIM:])
        e["oq"] = jnp.dot(e["qk2"].astype(BF16), r2, preferred_element_type=F32)
        kt = jnp.concatenate([e["k"][d] * jnp.exp(e["g_last"][d] - e["gc"][d]) for d in range(2)], axis=0)
        e["bm"] = _dot(kt, r2, _TN)

    def store(e):
        hh, n, oq, bm = e["hh"], e["n"], e["oq"], e["bm"]
        for d in range(2):
            base = 2 * d * HEAD_DIM
            ob[hh, d, e["rows"][d], :] = oq[:, base:base + HEAD_DIM]
            q_s[hh, n, d] = (e["q"][d] * e["eg"][d] - oq[:, base + HEAD_DIM:base + 2 * HEAD_DIM]).astype(BF16)
            b_s[hh, n, d] = bm[:, base:base + HEAD_DIM]
            m_s[hh, n, d] = bm[:, base + HEAD_DIM:base + 2 * HEAD_DIM].astype(BF16)
            gl_s[hh, n, d] = jnp.broadcast_to(jnp.exp(e["g_last"][d]), (1, HEAD_DIM))

    def scan_step(n):
        for hh in range(heads):
            for d in range(2):
                rows = pl.ds(pl.multiple_of((n if d == 0 else n_chunks - 1 - n) * c, c), c)
                state = st[hh, d]
                ms = jnp.dot(jnp.concatenate([m_s[hh, n, d], q_s[hh, n, d]], axis=0), state.astype(BF16),
                             preferred_element_type=F32)
                ob[hh, d, rows, :] = ob[hh, d, rows, :] + ms[HEAD_DIM:]
                st[hh, d] = state * gl_s[hh, n, d] - ms[:HEAD_DIM] + b_s[hh, n, d]

    def prepare(first, scan_first):
        pending = [] if scan_first is None else [scan_first + j for j in range(group)]
        n_slots = 8

        def fill(slot):
            for j in range(slot * len(pending) // n_slots, (slot + 1) * len(pending) // n_slots):
                scan_step(pending[j])

        chunks = [load(hh, first + j) for j in range(group) for hh in range(heads)]
        for e in chunks:
            gram(e)
        fill(0)
        inverses = _pair_inverses([e["low2"] for e in chunks], left, ahead == 0, lambda level: fill(1 + level))
        for e, t2 in zip(chunks, inverses):
            solve(e, t2)
        fill(6)
        for e in chunks:
            fold(e)
        fill(7)
        for e in chunks:
            store(e)

    for hh in range(heads):
        for d in range(2):
            st[hh, d] = s0_ref[d, hh] if has_init else jnp.zeros((HEAD_DIM, HEAD_DIM), F32)

    n_steps = n_chunks // group
    prepare(0, None)

    def step(i, carry):
        prepare(i * group, (i - 1) * group)
        return carry

    lax.fori_loop(1, n_steps, step, 0)
    for j in range(group):
        scan_step((n_steps - 1) * group + j)

    def post(i, carry):
        rows = pl.ds(pl.multiple_of(i * PRE_ROWS, PRE_ROWS), PRE_ROWS)
        for hh in range(heads):
            o_ref[rows, hh * HEAD_DIM:(hh + 1) * HEAD_DIM] = ob[hh, 0, rows, :] + ob[hh, 1, rows, :]
        return carry

    lax.fori_loop(0, t_len // PRE_ROWS, post, 0)
    if emit_state:
        for hh in range(heads):
            for d in range(2):
                sfin_ref[d, hh] = st[hh, d]


def _gdn_mixer(proj, gates, s0, layer_idx, emit_state):
    b, t, _ = proj.shape
    c = GDN_CHUNK
    n_chunks = t // c
    assert t % PRE_ROWS == 0 and 2 * c == HEAD_DIM
    as_rows = lambda g: g.reshape(b, n_chunks, c, N_HEADS).transpose(0, 1, 3, 2)
    grow = jnp.concatenate([as_rows(gates[:, :, 2 * N_HEADS:3 * N_HEADS]),
                            as_rows(gates[:, :, 3 * N_HEADS:4 * N_HEADS])], axis=-1)
    has_init = s0 is not None
    group = min(GDN_GROUP, n_chunks)
    heads = max(1, GDN_GROUP // n_chunks)
    width = heads * HEAD_DIM
    col = lambda off: pl.BlockSpec((None, t, width), lambda i, h: (i, 0, off // heads + h))
    in_specs = [col(0), col(N_HEADS), col(2 * N_HEADS),
                pl.BlockSpec((None, t, GATE_LANES), lambda i, h: (i, 0, 0)),
                pl.BlockSpec((None, n_chunks, N_HEADS, 2 * c), lambda i, h: (i, 0, 0, 0))]
    args = [proj, proj, proj, gates, grow]
    if has_init:
        in_specs.append(pl.BlockSpec((None, None, 2, heads, HEAD_DIM, HEAD_DIM),
                                     lambda i, h: (i, layer_idx, 0, h, 0, 0)))
        args.append(s0)
    out_specs = [pl.BlockSpec((None, t, width), lambda i, h: (i, 0, h))]
    out_shape = [jax.ShapeDtypeStruct((b, t, KEY_DIM), F32)]
    if emit_state:
        out_specs.append(pl.BlockSpec((None, 2, heads, HEAD_DIM, HEAD_DIM), lambda i, h: (i, 0, h, 0, 0)))
        out_shape.append(jax.ShapeDtypeStruct((b, 2, N_HEADS, HEAD_DIM, HEAD_DIM), F32))
    scratch = [
        pltpu.VMEM((heads, 2, t, HEAD_DIM), F32),
        pltpu.VMEM((heads, n_chunks, 2, HEAD_DIM, HEAD_DIM), BF16),
        pltpu.VMEM((heads, n_chunks, 2, HEAD_DIM, HEAD_DIM), F32),
        pltpu.VMEM((heads, n_chunks, 2, c, HEAD_DIM), BF16),
        pltpu.VMEM((heads, n_chunks, 2, 1, HEAD_DIM), F32),
        pltpu.VMEM((heads, 2, HEAD_DIM, HEAD_DIM), F32)]
    res = pl.pallas_call(
        functools.partial(_gdn_kernel, has_init=has_init, emit_state=emit_state, group=group),
        grid=(b, N_HEADS // heads),
        in_specs=in_specs,
        out_specs=out_specs,
        out_shape=out_shape,
        scratch_shapes=scratch,
        compiler_params=_params(2),
        name="gdn_mixer",
    )(*args)
    return res[0], (res[1] if emit_state else None)


def _hgrn_kernel(q_ref, kf_ref, ef_ref, kb_ref, eb_ref, i_ref, *rest, has_init, emit_state):
    rest = list(rest)
    s0_ref = rest.pop(0) if has_init else None
    o_ref = rest.pop(0)
    sfin_ref = rest.pop(0) if emit_state else None
    qin, kout, ktail, ftot, ob, st = rest
    t_len = q_ref.shape[0]
    c = HGRN_CHUNK

    def pre(d, rows, q):
        k = (kf_ref, kb_ref)[d][rows, :]
        e = (ef_ref, eb_ref)[d][rows, :]
        whole = _segment_pick(e, c, c - 1 if d == 0 else 0)
        k_over_e = k / e
        qin[d, rows, :] = q * e
        kout[d, rows, :] = k_over_e.astype(BF16)
        ktail[d, rows, :] = k_over_e * whole
        ftot[d, rows, :] = whole

    def pre_body(i, carry):
        rows = pl.ds(pl.multiple_of(i * HGRN_BLOCK, HGRN_BLOCK), HGRN_BLOCK)
        q = q_ref[rows, :]
        pre(0, rows, q)
        pre(1, rows, q)
        return carry

    lax.fori_loop(0, t_len // HGRN_BLOCK, pre_body, 0)

    for d in range(2):
        st[d] = s0_ref[d].T if has_init else jnp.zeros((HEAD_DIM, HEAD_DIM), F32)

    grp = HGRN_GROUP
    per = grp // c
    assert per == 4
    blk = min(HGRN_SCAN_ROWS, t_len)
    ri = lax.broadcasted_iota(jnp.int32, (grp, grp), 0)
    ci = lax.broadcasted_iota(jnp.int32, (grp, grp), 1)
    row_chunk = lax.broadcasted_iota(jnp.int32, (grp, HEAD_DIM), 0) // c

    def group_factors(f, d):
        r1, r2, r3 = (pltpu.roll(f, s * c, axis=0) for s in (1, 2, 3))
        prev, nxt = ((r1, r2, r3), (r3, r2, r1)) if d == 0 else ((r3, r2, r1), (r1, r2, r3))
        order = row_chunk if d == 0 else per - 1 - row_chunk
        g1, h1 = prev[0], nxt[0]
        g2, h2 = g1 * prev[1], h1 * nxt[1]
        g3, h3 = g2 * prev[2], h2 * nxt[2]
        before = jnp.where(order == 0, 1.0, jnp.where(order == 1, g1, jnp.where(order == 2, g2, g3)))
        after = jnp.where(order == 3, 1.0, jnp.where(order == 2, h1, jnp.where(order == 1, h2, h3)))
        whole = f[0:1] * f[c:c + 1] * f[2 * c:2 * c + 1] * f[3 * c:3 * c + 1]
        return g1, g2, before, after, whole

    def att_select(d, p1, p234):
        dist = (ri // c - ci // c) if d == 0 else (ci // c - ri // c)
        inside = (ci <= ri) if d == 0 else (ci >= ri)
        return jnp.where((dist == 0) & inside, p1,
                         jnp.where(dist == 1, p234[:grp],
                                   jnp.where(dist == 2, p234[grp:2 * grp],
                                             jnp.where(dist == 3, p234[2 * grp:], 0.0))))

    def body(i, carry):
        ctx = []
        for step in range(blk // grp):
            for d in range(2):
                g_idx = i * (blk // grp) + step
                r0 = pl.multiple_of((g_idx if d == 0 else t_len // grp - 1 - g_idx) * grp, grp)
                rows = pl.ds(r0, grp)
                e = dict(d=d, rows=rows, qi=qin[d, rows, :], ko=kout[d, rows, :], kt=ktail[d, rows, :],
                         v=i_ref[rows, :].astype(BF16))
                e["g1"], e["g2"], e["before"], e["after"], e["whole"] = group_factors(ftot[d, rows, :], d)
                ctx.append(e)
        for e in ctx:
            qi = e["qi"]
            e["p1"] = _dot(qi, e["ko"], _NT)
            e["p234"] = _dot(jnp.concatenate([qi, qi * e["g1"], qi * e["g2"]], axis=0), e["kt"], _NT)
        for e in ctx:
            e["ds"] = _dot(e["v"], e["kt"] * e["after"], _TN)
        for e in ctx:
            e["intra"] = _dot(att_select(e["d"], e["p1"], e["p234"]), e["v"])
        states = [st[0], st[1]]
        for e in ctx:
            d = e["d"]
            ob[d, e["rows"], :] = e["intra"] + _dot(e["qi"] * e["before"], states[d], _NT)
            states[d] = states[d] * e["whole"] + e["ds"]
        st[0], st[1] = states
        return carry

    lax.fori_loop(0, t_len // blk, body, 0)

    def post(i, carry):
        rows = pl.ds(pl.multiple_of(i * PRE_ROWS, PRE_ROWS), PRE_ROWS)
        o_ref[rows, :] = ob[0, rows, :] + ob[1, rows, :]
        return carry

    lax.fori_loop(0, t_len // PRE_ROWS, post, 0)
    if emit_state:
        for d in range(2):
            sfin_ref[d] = st[d].T


def _hgrn_mixer(proj, s0, layer_idx, emit_state):
    b, t, _ = proj.shape
    has_init = s0 is not None
    col = lambda off: pl.BlockSpec((None, t, HEAD_DIM), lambda i, h: (i, 0, off + h))
    in_specs = [col(j * N_HEADS) for j in range(6)]
    args = [proj] * 6
    if has_init:
        in_specs.append(pl.BlockSpec((None, None, 2, None, HEAD_DIM, HEAD_DIM),
                                     lambda i, h: (i, layer_idx, 0, h, 0, 0)))
        args.append(s0)
    out_specs = [pl.BlockSpec((None, t, HEAD_DIM), lambda i, h: (i, 0, h))]
    out_shape = [jax.ShapeDtypeStruct((b, t, KEY_DIM), F32)]
    if emit_state:
        out_specs.append(pl.BlockSpec((None, 2, None, HEAD_DIM, HEAD_DIM), lambda i, h: (i, 0, h, 0, 0)))
        out_shape.append(jax.ShapeDtypeStruct((b, 2, N_HEADS, HEAD_DIM, HEAD_DIM), F32))
    res = pl.pallas_call(
        functools.partial(_hgrn_kernel, has_init=has_init, emit_state=emit_state),
        grid=(b, N_HEADS),
        in_specs=in_specs,
        out_specs=out_specs,
        out_shape=out_shape,
        scratch_shapes=[pltpu.VMEM((2, t, HEAD_DIM), F32), pltpu.VMEM((2, t, HEAD_DIM), BF16),
                        pltpu.VMEM((2, t, HEAD_DIM), F32), pltpu.VMEM((2, t, HEAD_DIM), F32),
                        pltpu.VMEM((2, t, HEAD_DIM), F32), pltpu.VMEM((2, HEAD_DIM, HEAD_DIM), F32)],
        compiler_params=_params(2),
        name="hgrn_mixer",
    )(*args)
    return res[0], (res[1] if emit_state else None)


def _post_kernel(x_ref, o_ref, z_ref, on_ref, g_ref, gt1_ref, sh2_ref, sc2_ref, gt2_ref, wo_ref, w1_ref, w2_ref,
                 y_ref):
    rows = x_ref.shape[0]
    n_parts = 2
    part = rows // n_parts
    mixes = []
    for r in range(n_parts):
        rs = slice(r * part, (r + 1) * part)
        gated = [(_rms(o_ref[rs, h * HEAD_DIM:(h + 1) * HEAD_DIM], on_ref[...])
                  * z_ref[rs, h * HEAD_DIM:(h + 1) * HEAD_DIM]).astype(BF16) for h in range(N_HEADS)]
        mixes.append(jnp.dot(jnp.concatenate(gated, axis=1), wo_ref[...], preferred_element_type=F32))
    mix = jnp.concatenate(mixes, axis=0)
    x1 = x_ref[...] + gt1_ref[...] * _rms(mix, g_ref[1:2, :])
    hb = (_rms(x1, g_ref[2:3, :]) * (1.0 + sc2_ref[...]) + sh2_ref[...]).astype(BF16)
    ff = jnp.zeros(x1.shape, F32)
    for j in range(D_FF // 1024):
        cols = slice(j * 1024, (j + 1) * 1024)
        hid = jnp.maximum(jnp.dot(hb, w1_ref[:, cols], preferred_element_type=F32), 0.0)
        ff = ff + jnp.dot((hid * hid).astype(BF16), w2_ref[cols, :], preferred_element_type=F32)
    y_ref[...] = x1 + gt2_ref[...] * _rms(ff, g_ref[3:4, :])


def _post_mixer(x2d, o2d, proj2d, z_block, onorm_g, norm_g, mod, tiles_per_seq, w_out, w1, w2):
    rows = x2d.shape[0]
    tile = pl.BlockSpec((ROW_TILE, D_MODEL), lambda i: (i, 0))
    return pl.pallas_call(
        _post_kernel,
        grid=(rows // ROW_TILE,),
        in_specs=[tile, tile, pl.BlockSpec((ROW_TILE, D_MODEL), lambda i: (i, z_block)),
                  _const_spec((1, HEAD_DIM)), _const_spec((4, D_MODEL)),
                  _mod_spec(2, tiles_per_seq), _mod_spec(3, tiles_per_seq),
                  _mod_spec(4, tiles_per_seq), _mod_spec(5, tiles_per_seq),
                  _const_spec((D_MODEL, D_MODEL)), _const_spec((D_MODEL, D_FF)), _const_spec((D_FF, D_MODEL))],
        out_specs=tile,
        out_shape=jax.ShapeDtypeStruct((rows, D_MODEL), F32),
        compiler_params=_params(1),
        name="post_mixer",
    )(x2d, o2d, proj2d, onorm_g.reshape(1, HEAD_DIM), norm_g, mod, mod, mod, mod, w_out, w1, w2)


def _trunk(x, mod_rows, per_seq_mod, s_gdn, s_hgrn, grid_conv, emit_state, weights):
    (norm_g, gdn_main, gdn_gate, gdn_alog, gdn_dt, gdn_conv_w, gdn_onorm_g, gdn_w_out,
     hgrn_w_in, hgrn_lb_logits, hgrn_onorm_g, hgrn_w_out, mlp_w1, mlp_w2) = weights
    b, t, _ = x.shape
    tiles_per_seq = t // ROW_TILE if per_seq_mod else None
    x2d = x.reshape(b * t, D_MODEL)
    depth = norm_g.shape[0]
    fin_gdn, fin_hgrn = [], []
    for layer in range(depth):
        j = layer // 2
        mod = mod_rows[layer]
        if layer % 2 == 0:
            proj, gates = _gdn_inproj(x2d, norm_g[layer], mod, tiles_per_seq, gdn_main[j], gdn_gate[j],
                                      gdn_alog[j], gdn_dt[j], gdn_conv_w[j], GRID_W if grid_conv else t)
            o, fin = _gdn_mixer(proj.reshape(b, t, GDN_MAIN), gates.reshape(b, t, GATE_LANES), s_gdn, j, emit_state)
            fin_gdn.append(fin)
            z_block, onorm_g, w_out = 3, gdn_onorm_g[j], gdn_w_out[j]
        else:
            proj = _hgrn_inproj(x2d, norm_g[layer], mod, tiles_per_seq, hgrn_w_in[j], hgrn_lb_logits, layer)
            o, fin = _hgrn_mixer(proj.reshape(b, t, HGRN_OUT), s_hgrn, j, emit_state)
            fin_hgrn.append(fin)
            z_block, onorm_g, w_out = 6, hgrn_onorm_g[j], hgrn_w_out[j]
        x2d = _post_mixer(x2d, o.reshape(b * t, KEY_DIM), proj, z_block, onorm_g, norm_g[layer], mod, tiles_per_seq,
                          w_out, mlp_w1[layer], mlp_w2[layer])
    y = x2d.reshape(b, t, D_MODEL)
    if emit_state:
        return y, jnp.stack(fin_gdn, axis=1), jnp.stack(fin_hgrn, axis=1)
    return y, None, None


def kernel(x_prompt, x_sample, state_gdn, state_hgrn, c, c_ctx, w_ada, b_ada, norm_g, gdn_w_in, gdn_conv_w,
           gdn_a_log, gdn_dt_bias, gdn_onorm_g, gdn_w_out, hgrn_w_in, hgrn_lb_logits, hgrn_onorm_g, hgrn_w_out,
           mlp_w1, mlp_w2):
    n_dec = c.shape[0]
    n_rows = 16
    cond = jnp.concatenate([c_ctx[None, :], c, jnp.zeros((n_rows - 1 - n_dec, D_MODEL), F32)], axis=0)
    mod = _ada_mod(cond, w_ada, b_ada)
    mod_ctx = mod[:, 0:1, None, :]
    mod_smp = mod[:, 1:1 + n_dec, None, :]

    n_gdn = gdn_w_in.shape[0]
    gate_pad = GATE_LANES - 4 * N_HEADS
    gdn_gate = jnp.pad(gdn_w_in[:, :, GDN_MAIN:], ((0, 0), (0, 0), (0, gate_pad))).astype(BF16)
    lead = jnp.zeros((n_gdn, 2 * N_HEADS), F32)
    tail = jnp.zeros((n_gdn, gate_pad), F32)
    gdn_alog = jnp.concatenate([lead, gdn_a_log.reshape(n_gdn, 2 * N_HEADS), tail], axis=1)[:, None, :]
    gdn_dt = jnp.concatenate([lead, gdn_dt_bias.reshape(n_gdn, 2 * N_HEADS), tail], axis=1)[:, None, :]
    weights = (norm_g, gdn_w_in[:, :, :GDN_MAIN].astype(BF16), gdn_gate, gdn_alog, gdn_dt, gdn_conv_w,
               gdn_onorm_g, gdn_w_out.astype(BF16), hgrn_w_in.astype(BF16), hgrn_lb_logits, hgrn_onorm_g,
               hgrn_w_out.astype(BF16), mlp_w1.astype(BF16), mlp_w2.astype(BF16))

    y_prompt, new_gdn, new_hgrn = _trunk(x_prompt, mod_ctx, False, None, None, False, True, weights)
    y_sample, _, _ = _trunk(x_sample, mod_smp, True, state_gdn, state_hgrn, True, False, weights)
    return (y_prompt, y_sample, new_gdn, new_hgrn)
```

```python
import functools

import jax
import jax.numpy as jnp
from jax import lax
from jax.experimental import pallas as pl
from jax.experimental.pallas import tpu as pltpu

D_MODEL = 1024
N_HEADS = 8
HEAD_DIM = 128
KEY_DIM = N_HEADS * HEAD_DIM
CONV_W = 5
GDN_CHUNK = 64
HGRN_CHUNK = 16
GRID_W = 64
D_FF = 4 * D_MODEL
EPS = 1e-6
GDN_MAIN = 4 * KEY_DIM
GATE_LANES = 128
HGRN_PROJ = 5 * KEY_DIM
HGRN_OUT = 7 * KEY_DIM

ROW_TILE = 512
EPI_ROWS = 128
EPI_COLS = 256
PRE_ROWS = 256
HGRN_BLOCK = 256
HGRN_SCAN_ROWS = 512
HGRN_GROUP = 64
GDN_SEQ = 2
GDN_GROUP = 8
VMEM_LIMIT = 56 * 1024 * 1024

BF16 = jnp.bfloat16
F32 = jnp.float32

_NT = (((1,), (1,)), ((), ()))
_TN = (((0,), (0,)), ((), ()))


def _dot(a, b, dims=None):
    a = a.astype(BF16)
    b = b.astype(BF16)
    if dims is None:
        return jnp.dot(a, b, preferred_element_type=F32)
    return lax.dot_general(a, b, dims, preferred_element_type=F32)


def _segment_sums(x, seg):
    rows = x.shape[0]
    pos = lax.broadcasted_iota(jnp.int32, x.shape, 0) & (seg - 1)
    pre, suf = x, x
    s = 1
    while s < seg:
        pre = pre + jnp.where(pos >= s, pltpu.roll(pre, s, axis=0), 0.0)
        suf = suf + jnp.where(pos + s < seg, pltpu.roll(suf, rows - s, axis=0), 0.0)
        s *= 2
    return pre, suf


def _segment_prefix(x, seg):
    pos = lax.broadcasted_iota(jnp.int32, x.shape, 0) & (seg - 1)
    s = 1
    while s < seg:
        x = x + jnp.where(pos >= s, pltpu.roll(x, s, axis=0), 0.0)
        s *= 2
    return x


def _segment_pick(x, seg, which):
    rows, width = x.shape
    picked = x.reshape(rows // seg, seg, width)[:, which:which + 1, :]
    return jnp.broadcast_to(picked, (rows // seg, seg, width)).reshape(rows, width)


def _segment_last(x, seg):
    return _segment_pick(x, seg, seg - 1)


def _rms(x, g):
    return x * lax.rsqrt(jnp.mean(x * x, axis=-1, keepdims=True) + EPS) * g


def _silu(x):
    return x * jax.nn.sigmoid(x)


def _softplus(x):
    return jnp.maximum(x, 0.0) + jnp.log1p(jnp.exp(-jnp.abs(x)))


def _const_spec(shape):
    zeros = (0,) * len(shape)
    return pl.BlockSpec(shape, lambda *_: zeros, pipeline_mode=pl.Buffered(1))


def _mod_spec(col, tiles_per_seq):
    if tiles_per_seq is None:
        return pl.BlockSpec((None, 1, D_MODEL), lambda i: (0, 0, col))
    return pl.BlockSpec((None, 1, D_MODEL), lambda i: (i // tiles_per_seq, 0, col))


def _params(n_grid):
    return pltpu.CompilerParams(dimension_semantics=("arbitrary",) * n_grid, vmem_limit_bytes=VMEM_LIMIT)


def _ada_kernel(c_ref, w_ref, b_ref, o_ref):
    o_ref[...] = _dot(_silu(c_ref[...]), w_ref[...]) + b_ref[...]


def _ada_mod(cond, w_ada, b_ada):
    depth, _, n = w_ada.shape
    rows = cond.shape[0]
    tn = 1536
    return pl.pallas_call(
        _ada_kernel,
        grid=(depth, n // tn),
        in_specs=[pl.BlockSpec((rows, D_MODEL), lambda l, j: (0, 0)),
                  pl.BlockSpec((None, D_MODEL, tn), lambda l, j: (l, 0, j)),
                  pl.BlockSpec((None, 1, tn), lambda l, j: (l, 0, j))],
        out_specs=pl.BlockSpec((None, rows, tn), lambda l, j: (l, 0, j)),
        out_shape=jax.ShapeDtypeStruct((depth, rows, n), F32),
        compiler_params=_params(2),
        name="ada_mod",
    )(cond, w_ada, b_ada.reshape(depth, 1, n))


def _modulated(x_ref, g_ref, sh_ref, sc_ref):
    return (_rms(x_ref[...], g_ref[0:1, :]) * (1.0 + sc_ref[...]) + sh_ref[...]).astype(BF16)


def _gdn_inproj_kernel(x_ref, g_ref, sh_ref, sc_ref, w_ref, wg_ref, alog_ref, dt_ref, cw_ref, proj_ref, gate_ref,
                       *, seg):
    hb = _modulated(x_ref, g_ref, sh_ref, sc_ref)
    n_rows = hb.shape[0]
    p_rows = PRE_ROWS
    p_cols = EPI_COLS
    period = min(seg, p_rows)
    pos = lax.broadcasted_iota(jnp.int32, (period, HEAD_DIM), 0) & (seg - 1)

    def tap_weights(w):
        taps = {}
        for s in (-2, -1, 1, 2):
            valid = (pos + s >= 0) & (pos + s < seg)
            taps[s] = jnp.tile(jnp.where(valid, w[s + 2:s + 3, :], 0.0), (p_rows // period, 1))
        return taps

    def unit(x, scale):
        return x * (lax.rsqrt(jnp.sum(x * x, axis=-1, keepdims=True) + EPS) * scale)

    pieces = [(j, r) for j in range(GDN_MAIN // p_cols) for r in range(n_rows // p_rows)]

    def product(piece):
        j, r = piece
        return jnp.dot(hb[r * p_rows:(r + 1) * p_rows], w_ref[:, j * p_cols:(j + 1) * p_cols],
                       preferred_element_type=F32)

    def finish(piece, y):
        j, r = piece
        rows = slice(r * p_rows, (r + 1) * p_rows)
        kind = j * p_cols // KEY_DIM
        if kind == 3:
            proj_ref[rows, j * p_cols:(j + 1) * p_cols] = _silu(y)
            return
        for hh in range(p_cols // HEAD_DIM):
            c0 = j * p_cols + hh * HEAD_DIM
            w = cw_ref[:, c0:c0 + HEAD_DIM]
            taps = tap_weights(w)
            x = y[:, hh * HEAD_DIM:(hh + 1) * HEAD_DIM]
            acc = x * w[CONV_W // 2:CONV_W // 2 + 1, :]
            for s in (-2, -1, 1, 2):
                acc = acc + pltpu.roll(x, (-s) % p_rows, axis=0) * taps[s]
            out = _silu(acc)
            if kind == 0:
                out = unit(out, HEAD_DIM ** -0.5)
            elif kind == 1:
                out = unit(out, 1.0)
            proj_ref[rows, c0:c0 + HEAD_DIM] = out

    y_next = product(pieces[0])
    for n, piece in enumerate(pieces):
        y, y_next = y_next, (product(pieces[n + 1]) if n + 1 < len(pieces) else None)
        finish(piece, y)
    raw = jnp.dot(hb, wg_ref[...], preferred_element_type=F32)
    beta = jax.nn.sigmoid(raw)
    g = -jnp.exp(alog_ref[...]) * _softplus(raw + dt_ref[...])
    gp, gs = _segment_sums(g, GDN_CHUNK)
    lane = lax.broadcasted_iota(jnp.int32, raw.shape, 1)
    gate_ref[...] = jnp.where(lane < 2 * N_HEADS, beta,
                              jnp.where(lane < 3 * N_HEADS, gp,
                                        jnp.where(lane < 4 * N_HEADS, gs, 0.0)))


def _gdn_inproj(x2d, norm_g, mod, tiles_per_seq, w_main, w_gate, alog_row, dt_row, conv_w, seg):
    rows = x2d.shape[0]
    assert PRE_ROWS % seg == 0 and ROW_TILE % PRE_ROWS == 0
    return pl.pallas_call(
        functools.partial(_gdn_inproj_kernel, seg=seg),
        grid=(rows // ROW_TILE,),
        in_specs=[pl.BlockSpec((ROW_TILE, D_MODEL), lambda i: (i, 0)),
                  _const_spec((4, D_MODEL)),
                  _mod_spec(0, tiles_per_seq), _mod_spec(1, tiles_per_seq),
                  _const_spec((D_MODEL, GDN_MAIN)), _const_spec((D_MODEL, GATE_LANES)),
                  _const_spec((1, GATE_LANES)), _const_spec((1, GATE_LANES)),
                  _const_spec((CONV_W, 3 * KEY_DIM))],
        out_specs=[pl.BlockSpec((ROW_TILE, GDN_MAIN), lambda i: (i, 0)),
                   pl.BlockSpec((ROW_TILE, GATE_LANES), lambda i: (i, 0))],
        out_shape=[jax.ShapeDtypeStruct((rows, GDN_MAIN), F32),
                   jax.ShapeDtypeStruct((rows, GATE_LANES), F32)],
        compiler_params=_params(1),
        name="gdn_inproj",
    )(x2d, norm_g, mod, mod, w_main, w_gate, alog_row, dt_row, conv_w)


def _hgrn_inproj_kernel(x_ref, g_ref, sh_ref, sc_ref, w_ref, lbl_ref, out_ref, *, layer):
    hb = _modulated(x_ref, g_ref, sh_ref, sc_ref)
    n_rows = hb.shape[0]
    c = HGRN_CHUNK
    depth = lbl_ref.shape[0]
    logits = [lbl_ref[l] for l in range(depth)]
    top = functools.reduce(jnp.maximum, logits)
    ex = [jnp.exp(l - top) for l in logits]
    denom = functools.reduce(lambda x, y: x + y, ex)
    lb = functools.reduce(lambda x, y: x + y, ex[:layer + 1]) / denom - ex[0] / denom

    pieces = [(j, r) for j in range(HGRN_PROJ // EPI_COLS) for r in range(n_rows // EPI_ROWS)]

    def product(piece):
        j, r = piece
        return jnp.dot(hb[r * EPI_ROWS:(r + 1) * EPI_ROWS], w_ref[:, j * EPI_COLS:(j + 1) * EPI_COLS],
                       preferred_element_type=F32)

    def finish(piece, y):
        j, r = piece
        rows = slice(r * EPI_ROWS, (r + 1) * EPI_ROWS)
        kind, off = divmod(j * EPI_COLS, KEY_DIM)
        if kind in (0, 3, 4):
            dst = {0: 0, 3: 5, 4: 6}[kind] * KEY_DIM + off
            out_ref[rows, dst:dst + EPI_COLS] = y if kind == 3 else _silu(y)
            return
        d = kind - 1
        for hh in range(EPI_COLS // HEAD_DIM):
            c0 = off + hh * HEAD_DIM
            lbd = lb[d:d + 1, c0:c0 + HEAD_DIM]
            fg = lbd + (1.0 - lbd) * jax.nn.sigmoid(y[:, hh * HEAD_DIM:(hh + 1) * HEAD_DIM])
            logf = jnp.log(fg)
            run = _segment_prefix(logf, c)
            if d == 1:
                run = _segment_last(run, c) - run + logf
            out_ref[rows, (1 + 2 * d) * KEY_DIM + c0:(1 + 2 * d) * KEY_DIM + c0 + HEAD_DIM] = 1.0 - fg
            out_ref[rows, (2 + 2 * d) * KEY_DIM + c0:(2 + 2 * d) * KEY_DIM + c0 + HEAD_DIM] = jnp.exp(run)

    y_next = product(pieces[0])
    for n, piece in enumerate(pieces):
        y, y_next = y_next, (product(pieces[n + 1]) if n + 1 < len(pieces) else None)
        finish(piece, y)


def _hgrn_inproj(x2d, norm_g, mod, tiles_per_seq, w_in, lb_logits, layer):
    rows = x2d.shape[0]
    depth = lb_logits.shape[0]
    return pl.pallas_call(
        functools.partial(_hgrn_inproj_kernel, layer=layer),
        grid=(rows // ROW_TILE,),
        in_specs=[pl.BlockSpec((ROW_TILE, D_MODEL), lambda i: (i, 0)),
                  _const_spec((4, D_MODEL)),
                  _mod_spec(0, tiles_per_seq), _mod_spec(1, tiles_per_seq),
                  _const_spec((D_MODEL, HGRN_PROJ)), _const_spec((depth, 2, KEY_DIM))],
        out_specs=pl.BlockSpec((ROW_TILE, HGRN_OUT), lambda i: (i, 0)),
        out_shape=jax.ShapeDtypeStruct((rows, HGRN_OUT), F32),
        compiler_params=_params(1),
        name="hgrn_inproj",
    )(x2d, norm_g, mod, mod, w_in, lb_logits)


def _split_bf16(x):
    hi = lax.bitcast_convert_type(lax.bitcast_convert_type(x, jnp.int32) & jnp.int32(-65536), F32)
    return hi, x - hi


def _pair_inverses(lows, left, diag2, fill):
    c = lows[0].shape[0]

    def block_diag(x):
        return jnp.concatenate([jnp.where(left, x, 0.0), jnp.where(left, 0.0, x)], axis=0)

    def left_operand(hi, lo):
        return jnp.concatenate([hi.astype(BF16), lo.astype(BF16)] * 2, axis=1)

    def right_operand(hi, lo):
        bh, bl = block_diag(hi).astype(BF16), block_diag(lo).astype(BF16)
        return jnp.concatenate([bh, bh, bl, bl], axis=0)

    accs = [jnp.where(diag2, 1.0, 0.0) - x for x in lows]
    parts = [_split_bf16(x) for x in lows]
    powers = [jnp.dot(left_operand(hi, lo), right_operand(hi, lo), preferred_element_type=F32) for hi, lo in parts]
    fill(0)
    levels = c.bit_length() - 2
    for level in range(levels):
        parts = [_split_bf16(x) for x in powers]
        rhs = [right_operand(hi, lo) for hi, lo in parts]
        acc_lhs = [left_operand(*_split_bf16(a)) for a in accs]
        if level + 1 < levels:
            res = [jnp.dot(jnp.concatenate([left_operand(hi, lo), al], axis=0), r, preferred_element_type=F32)
                   for (hi, lo), al, r in zip(parts, acc_lhs, rhs)]
            fill(level + 1)
            powers = [x[:c] for x in res]
            accs = [a + x[c:] for a, x in zip(accs, res)]
        else:
            accs = [a + jnp.dot(al, r, preferred_element_type=F32) for a, al, r in zip(accs, acc_lhs, rhs)]
    return accs


def _gdn_kernel(qn, kn, vn, gcol_ref, grow_ref, *rest, has_init, emit_state, group, seq):
    rest = list(rest)
    s0_ref = rest.pop(0) if has_init else None
    o_ref = rest.pop(0)
    sfin_ref = rest.pop(0) if emit_state else None
    ob, m_s, b_s, q_s, gl_s, st = rest
    t_len = qn.shape[0]
    c = GDN_CHUNK
    n_chunks = t_len // c
    n_heads = qn.shape[1] // HEAD_DIM
    heads = n_heads // seq
    h0 = pl.program_id(1) * n_heads

    lane = lax.broadcasted_iota(jnp.int32, (c, 2 * c), 1)
    row = lax.broadcasted_iota(jnp.int32, (c, 2 * c), 0)
    left = lane < c
    ahead = jnp.where(left, row - lane, lane - c - row)
    left_row = lax.broadcasted_iota(jnp.int32, (1, 2 * c), 1) < c

    def column(tile, idx):
        return jnp.sum(jnp.where(lane == idx, tile, 0.0), axis=-1, keepdims=True)

    def direction_blocks(a0, a1):
        z0, z1 = jnp.zeros(a0.shape, BF16), jnp.zeros(a1.shape, BF16)
        return jnp.concatenate([jnp.concatenate([a0.astype(BF16), z1], axis=1),
                                jnp.concatenate([z0, a1.astype(BF16)], axis=1)], axis=0)

    def load(hh, n):
        h = h0 + hh
        cidx = [n, n_chunks - 1 - n]
        rows = [pl.ds(pl.multiple_of(ci * c, c), c) for ci in cidx]
        lanes = (slice(hh * HEAD_DIM, (hh + 1) * HEAD_DIM) if isinstance(hh, int)
                 else pl.ds(pl.multiple_of(hh * HEAD_DIM, HEAD_DIM), HEAD_DIM))
        e = dict(hh=hh, n=n, rows=rows, q=[qn[r, lanes] for r in rows], k=[kn[r, lanes] for r in rows],
                 v=[vn[r, lanes] for r in rows])
        gtile = [gcol_ref[r, :] for r in rows]
        e["beta"] = [column(gtile[d], d * N_HEADS + h) for d in range(2)]
        e["gc"] = [column(gtile[d], (2 + d) * N_HEADS + h) for d in range(2)]
        e["g_last"] = [e["gc"][0][c - 1:c, :], e["gc"][1][0:1, :]]
        gr2 = jnp.where(left_row, grow_ref[cidx[0], pl.ds(h, 1), :], grow_ref[cidx[1], pl.ds(h, 1), :])
        e["decay2"] = jnp.exp(jnp.where(ahead >= 0, jnp.where(left, e["gc"][0], e["gc"][1]) - gr2, -jnp.inf))
        return e

    def gram(e):
        k, q = e["k"], e["q"]
        kq = _dot(jnp.concatenate([jnp.concatenate(k, axis=1), jnp.concatenate(q, axis=1)], axis=0),
                  direction_blocks(k[0], k[1]), _NT)
        e["low2"] = jnp.where(ahead > 0, kq[:c] * jnp.where(left, e["beta"][0], e["beta"][1]) * e["decay2"], 0.0)
        e["qk2"] = kq[c:] * e["decay2"]

    def solve(e, t2):
        k, v, beta, gc = e["k"], e["v"], e["beta"], e["gc"]
        e["eg"] = [jnp.exp(gc[d]) for d in range(2)]
        rhs = [jnp.concatenate([v[d] * beta[d], k[d] * beta[d] * e["eg"][d]], axis=1) for d in range(2)]
        e["uw"] = jnp.dot(t2.astype(BF16), direction_blocks(*rhs), preferred_element_type=F32)

    def fold(e):
        uw = e["uw"]
        r2 = direction_blocks(uw[:, :2 * HEAD_DIM], uw[:, 2 * HEAD_DIM:])
        e["oq"] = jnp.dot(e["qk2"].astype(BF16), r2, preferred_element_type=F32)
        kt = jnp.concatenate([e["k"][d] * jnp.exp(e["g_last"][d] - e["gc"][d]) for d in range(2)], axis=0)
        e["bm"] = _dot(kt, r2, _TN)

    def store(e):
        hh, n, oq, bm = e["hh"], e["n"], e["oq"], e["bm"]
        for d in range(2):
            base = 2 * d * HEAD_DIM
            ob[hh, d, e["rows"][d], :] = oq[:, base:base + HEAD_DIM]
            q_s[hh, n, d] = (e["q"][d] * e["eg"][d] - oq[:, base + HEAD_DIM:base + 2 * HEAD_DIM]).astype(BF16)
            b_s[hh, n, d] = bm[:, base:base + HEAD_DIM]
            m_s[hh, n, d] = bm[:, base + HEAD_DIM:base + 2 * HEAD_DIM].astype(BF16)
            gl_s[hh, n, d] = jnp.broadcast_to(jnp.exp(e["g_last"][d]), (1, HEAD_DIM))

    def scan_step(base, n):
        for hh in (base + j for j in range(heads)):
            for d in range(2):
                rows = pl.ds(pl.multiple_of((n if d == 0 else n_chunks - 1 - n) * c, c), c)
                state = st[hh, d]
                ms = jnp.dot(jnp.concatenate([m_s[hh, n, d], q_s[hh, n, d]], axis=0), state.astype(BF16),
                             preferred_element_type=F32)
                ob[hh, d, rows, :] = ob[hh, d, rows, :] + ms[HEAD_DIM:]
                st[hh, d] = state * gl_s[hh, n, d] - ms[:HEAD_DIM] + b_s[hh, n, d]

    def prepare(base, first, behind):
        pending = [] if behind is None else [behind[1] + j for j in range(group)]
        n_slots = 8

        def fill(slot):
            for j in range(slot * len(pending) // n_slots, (slot + 1) * len(pending) // n_slots):
                scan_step(behind[0], pending[j])

        chunks = [load(base + hh, first + j) for j in range(group) for hh in range(heads)]
        for e in chunks:
            gram(e)
        fill(0)
        inverses = _pair_inverses([e["low2"] for e in chunks], left, ahead == 0, lambda level: fill(1 + level))
        for e, t2 in zip(chunks, inverses):
            solve(e, t2)
        fill(6)
        for e in chunks:
            fold(e)
        fill(7)
        for e in chunks:
            store(e)

    for hh in range(n_heads):
        for d in range(2):
            st[hh, d] = s0_ref[d, hh] if has_init else jnp.zeros((HEAD_DIM, HEAD_DIM), F32)

    n_steps = n_chunks // group
    prepare(0, 0, None)

    def step(u, carry):
        unit = lambda v: ((v // n_steps) * heads, (v % n_steps) * group) if seq > 1 else (0, v * group)
        prepare(*unit(u), unit(u - 1))
        return carry

    lax.fori_loop(1, seq * n_steps, step, 0)
    for j in range(group):
        scan_step((seq - 1) * heads, (n_steps - 1) * group + j)

    def post(i, carry):
        rows = pl.ds(pl.multiple_of(i * PRE_ROWS, PRE_ROWS), PRE_ROWS)
        for hh in range(n_heads):
            o_ref[rows, hh * HEAD_DIM:(hh + 1) * HEAD_DIM] = ob[hh, 0, rows, :] + ob[hh, 1, rows, :]
        return carry

    lax.fori_loop(0, t_len // PRE_ROWS, post, 0)
    if emit_state:
        for hh in range(n_heads):
            for d in range(2):
                sfin_ref[d, hh] = st[hh, d]


def _gdn_mixer(proj, gates, s0, layer_idx, emit_state):
    b, t, _ = proj.shape
    c = GDN_CHUNK
    n_chunks = t // c
    assert t % PRE_ROWS == 0 and 2 * c == HEAD_DIM
    as_rows = lambda g: g.reshape(b, n_chunks, c, N_HEADS).transpose(0, 1, 3, 2)
    grow = jnp.concatenate([as_rows(gates[:, :, 2 * N_HEADS:3 * N_HEADS]),
                            as_rows(gates[:, :, 3 * N_HEADS:4 * N_HEADS])], axis=-1)
    has_init = s0 is not None
    group = min(GDN_GROUP, n_chunks)
    lock = max(1, GDN_GROUP // n_chunks)
    heads = lock * GDN_SEQ
    width = heads * HEAD_DIM
    col = lambda off: pl.BlockSpec((None, t, width), lambda i, h: (i, 0, off // heads + h))
    in_specs = [col(0), col(N_HEADS), col(2 * N_HEADS),
                pl.BlockSpec((None, t, GATE_LANES), lambda i, h: (i, 0, 0)),
                pl.BlockSpec((None, n_chunks, N_HEADS, 2 * c), lambda i, h: (i, 0, 0, 0))]
    args = [proj, proj, proj, gates, grow]
    if has_init:
        in_specs.append(pl.BlockSpec((None, None, 2, heads, HEAD_DIM, HEAD_DIM),
                                     lambda i, h: (i, layer_idx, 0, h, 0, 0)))
        args.append(s0)
    out_specs = [pl.BlockSpec((None, t, width), lambda i, h: (i, 0, h))]
    out_shape = [jax.ShapeDtypeStruct((b, t, KEY_DIM), F32)]
    if emit_state:
        out_specs.append(pl.BlockSpec((None, 2, heads, HEAD_DIM, HEAD_DIM), lambda i, h: (i, 0, h, 0, 0)))
        out_shape.append(jax.ShapeDtypeStruct((b, 2, N_HEADS, HEAD_DIM, HEAD_DIM), F32))
    scratch = [
        pltpu.VMEM((heads, 2, t, HEAD_DIM), F32),
        pltpu.VMEM((heads, n_chunks, 2, HEAD_DIM, HEAD_DIM), BF16),
        pltpu.VMEM((heads, n_chunks, 2, HEAD_DIM, HEAD_DIM), F32),
        pltpu.VMEM((heads, n_chunks, 2, c, HEAD_DIM), BF16),
        pltpu.VMEM((heads, n_chunks, 2, 1, HEAD_DIM), F32),
        pltpu.VMEM((heads, 2, HEAD_DIM, HEAD_DIM), F32)]
    res = pl.pallas_call(
        functools.partial(_gdn_kernel, has_init=has_init, emit_state=emit_state, group=group, seq=GDN_SEQ),
        grid=(b, N_HEADS // heads),
        in_specs=in_specs,
        out_specs=out_specs,
        out_shape=out_shape,
        scratch_shapes=scratch,
        compiler_params=_params(2),
        name="gdn_mixer",
    )(*args)
    return res[0], (res[1] if emit_state else None)


def _hgrn_kernel(q_ref, kf_ref, ef_ref, kb_ref, eb_ref, i_ref, *rest, has_init, emit_state):
    rest = list(rest)
    s0_ref = rest.pop(0) if has_init else None
    o_ref = rest.pop(0)
    sfin_ref = rest.pop(0) if emit_state else None
    qin, kout, ktail, ftot, ob, st = rest
    t_len = q_ref.shape[0]
    c = HGRN_CHUNK

    def pre(d, rows, q):
        k = (kf_ref, kb_ref)[d][rows, :]
        e = (ef_ref, eb_ref)[d][rows, :]
        whole = _segment_pick(e, c, c - 1 if d == 0 else 0)
        k_over_e = k / e
        qin[d, rows, :] = q * e
        kout[d, rows, :] = k_over_e.astype(BF16)
        ktail[d, rows, :] = k_over_e * whole
        ftot[d, rows, :] = whole

    def pre_body(i, carry):
        rows = pl.ds(pl.multiple_of(i * HGRN_BLOCK, HGRN_BLOCK), HGRN_BLOCK)
        q = q_ref[rows, :]
        pre(0, rows, q)
        pre(1, rows, q)
        return carry

    lax.fori_loop(0, t_len // HGRN_BLOCK, pre_body, 0)

    for d in range(2):
        st[d] = s0_ref[d].T if has_init else jnp.zeros((HEAD_DIM, HEAD_DIM), F32)

    grp = HGRN_GROUP
    per = grp // c
    assert per == 4
    blk = min(HGRN_SCAN_ROWS, t_len)
    ri = lax.broadcasted_iota(jnp.int32, (grp, grp), 0)
    ci = lax.broadcasted_iota(jnp.int32, (grp, grp), 1)
    row_chunk = lax.broadcasted_iota(jnp.int32, (grp, HEAD_DIM), 0) // c

    def group_factors(f, d):
        r1, r2, r3 = (pltpu.roll(f, s * c, axis=0) for s in (1, 2, 3))
        prev, nxt = ((r1, r2, r3), (r3, r2, r1)) if d == 0 else ((r3, r2, r1), (r1, r2, r3))
        order = row_chunk if d == 0 else per - 1 - row_chunk
        g1, h1 = prev[0], nxt[0]
        g2, h2 = g1 * prev[1], h1 * nxt[1]
        g3, h3 = g2 * prev[2], h2 * nxt[2]
        before = jnp.where(order == 0, 1.0, jnp.where(order == 1, g1, jnp.where(order == 2, g2, g3)))
        after = jnp.where(order == 3, 1.0, jnp.where(order == 2, h1, jnp.where(order == 1, h2, h3)))
        whole = f[0:1] * f[c:c + 1] * f[2 * c:2 * c + 1] * f[3 * c:3 * c + 1]
        return g1, g2, before, after, whole

    def att_select(d, p1, p234):
        dist = (ri // c - ci // c) if d == 0 else (ci // c - ri // c)
        inside = (ci <= ri) if d == 0 else (ci >= ri)
        return jnp.where((dist == 0) & inside, p1,
                         jnp.where(dist == 1, p234[:grp],
                                   jnp.where(dist == 2, p234[grp:2 * grp],
                                             jnp.where(dist == 3, p234[2 * grp:], 0.0))))

    def body(i, carry):
        ctx = []
        for step in range(blk // grp):
            for d in range(2):
                g_idx = i * (blk // grp) + step
                r0 = pl.multiple_of((g_idx if d == 0 else t_len // grp - 1 - g_idx) * grp, grp)
                rows = pl.ds(r0, grp)
                e = dict(d=d, rows=rows, qi=qin[d, rows, :], ko=kout[d, rows, :], kt=ktail[d, rows, :],
                         v=i_ref[rows, :].astype(BF16))
                e["g1"], e["g2"], e["before"], e["after"], e["whole"] = group_factors(ftot[d, rows, :], d)
                ctx.append(e)
        for e in ctx:
            qi = e["qi"]
            e["p1"] = _dot(qi, e["ko"], _NT)
            e["p234"] = _dot(jnp.concatenate([qi, qi * e["g1"], qi * e["g2"]], axis=0), e["kt"], _NT)
        for e in ctx:
            e["ds"] = _dot(e["v"], e["kt"] * e["after"], _TN)
        for e in ctx:
            e["intra"] = _dot(att_select(e["d"], e["p1"], e["p234"]), e["v"])
        states = [st[0], st[1]]
        for e in ctx:
            d = e["d"]
            ob[d, e["rows"], :] = e["intra"] + _dot(e["qi"] * e["before"], states[d], _NT)
            states[d] = states[d] * e["whole"] + e["ds"]
        st[0], st[1] = states
        return carry

    lax.fori_loop(0, t_len // blk, body, 0)

    def post(i, carry):
        rows = pl.ds(pl.multiple_of(i * PRE_ROWS, PRE_ROWS), PRE_ROWS)
        o_ref[rows, :] = ob[0, rows, :] + ob[1, rows, :]
        return carry

    lax.fori_loop(0, t_len // PRE_ROWS, post, 0)
    if emit_state:
        for d in range(2):
            sfin_ref[d] = st[d].T


def _hgrn_mixer(proj, s0, layer_idx, emit_state):
    b, t, _ = proj.shape
    has_init = s0 is not None
    col = lambda off: pl.BlockSpec((None, t, HEAD_DIM), lambda i, h: (i, 0, off + h))
    in_specs = [col(j * N_HEADS) for j in range(6)]
    args = [proj] * 6
    if has_init:
        in_specs.append(pl.BlockSpec((None, None, 2, None, HEAD_DIM, HEAD_DIM),
                                     lambda i, h: (i, layer_idx, 0, h, 0, 0)))
        args.append(s0)
    out_specs = [pl.BlockSpec((None, t, HEAD_DIM), lambda i, h: (i, 0, h))]
    out_shape = [jax.ShapeDtypeStruct((b, t, KEY_DIM), F32)]
    if emit_state:
        out_specs.append(pl.BlockSpec((None, 2, None, HEAD_DIM, HEAD_DIM), lambda i, h: (i, 0, h, 0, 0)))
        out_shape.append(jax.ShapeDtypeStruct((b, 2, N_HEADS, HEAD_DIM, HEAD_DIM), F32))
    res = pl.pallas_call(
        functools.partial(_hgrn_kernel, has_init=has_init, emit_state=emit_state),
        grid=(b, N_HEADS),
        in_specs=in_specs,
        out_specs=out_specs,
        out_shape=out_shape,
        scratch_shapes=[pltpu.VMEM((2, t, HEAD_DIM), F32), pltpu.VMEM((2, t, HEAD_DIM), BF16),
                        pltpu.VMEM((2, t, HEAD_DIM), F32), pltpu.VMEM((2, t, HEAD_DIM), F32),
                        pltpu.VMEM((2, t, HEAD_DIM), F32), pltpu.VMEM((2, HEAD_DIM, HEAD_DIM), F32)],
        compiler_params=_params(2),
        name="hgrn_mixer",
    )(*args)
    return res[0], (res[1] if emit_state else None)


def _post_kernel(x_ref, o_ref, z_ref, on_ref, g_ref, gt1_ref, sh2_ref, sc2_ref, gt2_ref, wo_ref, w1_ref, w2_ref,
                 y_ref):
    rows = x_ref.shape[0]
    n_parts = 2
    part = rows // n_parts
    mixes = []
    for r in range(n_parts):
        rs = slice(r * part, (r + 1) * part)
        gated = [(_rms(o_ref[rs, h * HEAD_DIM:(h + 1) * HEAD_DIM], on_ref[...])
                  * z_ref[rs, h * HEAD_DIM:(h + 1) * HEAD_DIM]).astype(BF16) for h in range(N_HEADS)]
        mixes.append(jnp.dot(jnp.concatenate(gated, axis=1), wo_ref[...], preferred_element_type=F32))
    mix = jnp.concatenate(mixes, axis=0)
    x1 = x_ref[...] + gt1_ref[...] * _rms(mix, g_ref[1:2, :])
    hb = (_rms(x1, g_ref[2:3, :]) * (1.0 + sc2_ref[...]) + sh2_ref[...]).astype(BF16)
    ff = jnp.zeros(x1.shape, F32)
    for j in range(D_FF // 1024):
        cols = slice(j * 1024, (j + 1) * 1024)
        hid = jnp.maximum(jnp.dot(hb, w1_ref[:, cols], preferred_element_type=F32), 0.0)
        ff = ff + jnp.dot((hid * hid).astype(BF16), w2_ref[cols, :], preferred_element_type=F32)
    y_ref[...] = x1 + gt2_ref[...] * _rms(ff, g_ref[3:4, :])


def _post_mixer(x2d, o2d, proj2d, z_block, onorm_g, norm_g, mod, tiles_per_seq, w_out, w1, w2):
    rows = x2d.shape[0]
    tile = pl.BlockSpec((ROW_TILE, D_MODEL), lambda i: (i, 0))
    return pl.pallas_call(
        _post_kernel,
        grid=(rows // ROW_TILE,),
        in_specs=[tile, tile, pl.BlockSpec((ROW_TILE, D_MODEL), lambda i: (i, z_block)),
                  _const_spec((1, HEAD_DIM)), _const_spec((4, D_MODEL)),
                  _mod_spec(2, tiles_per_seq), _mod_spec(3, tiles_per_seq),
                  _mod_spec(4, tiles_per_seq), _mod_spec(5, tiles_per_seq),
                  _const_spec((D_MODEL, D_MODEL)), _const_spec((D_MODEL, D_FF)), _const_spec((D_FF, D_MODEL))],
        out_specs=tile,
        out_shape=jax.ShapeDtypeStruct((rows, D_MODEL), F32),
        compiler_params=_params(1),
        name="post_mixer",
    )(x2d, o2d, proj2d, onorm_g.reshape(1, HEAD_DIM), norm_g, mod, mod, mod, mod, w_out, w1, w2)


def _trunk(x, mod_rows, per_seq_mod, s_gdn, s_hgrn, grid_conv, emit_state, weights):
    (norm_g, gdn_main, gdn_gate, gdn_alog, gdn_dt, gdn_conv_w, gdn_onorm_g, gdn_w_out,
     hgrn_w_in, hgrn_lb_logits, hgrn_onorm_g, hgrn_w_out, mlp_w1, mlp_w2) = weights
    b, t, _ = x.shape
    tiles_per_seq = t // ROW_TILE if per_seq_mod else None
    x2d = x.reshape(b * t, D_MODEL)
    depth = norm_g.shape[0]
    fin_gdn, fin_hgrn = [], []
    for layer in range(depth):
        j = layer // 2
        mod = mod_rows[layer]
        if layer % 2 == 0:
            proj, gates = _gdn_inproj(x2d, norm_g[layer], mod, tiles_per_seq, gdn_main[j], gdn_gate[j],
                                      gdn_alog[j], gdn_dt[j], gdn_conv_w[j], GRID_W if grid_conv else t)
            o, fin = _gdn_mixer(proj.reshape(b, t, GDN_MAIN), gates.reshape(b, t, GATE_LANES), s_gdn, j, emit_state)
            fin_gdn.append(fin)
            z_block, onorm_g, w_out = 3, gdn_onorm_g[j], gdn_w_out[j]
        else:
            proj = _hgrn_inproj(x2d, norm_g[layer], mod, tiles_per_seq, hgrn_w_in[j], hgrn_lb_logits, layer)
            o, fin = _hgrn_mixer(proj.reshape(b, t, HGRN_OUT), s_hgrn, j, emit_state)
            fin_hgrn.append(fin)
            z_block, onorm_g, w_out = 6, hgrn_onorm_g[j], hgrn_w_out[j]
        x2d = _post_mixer(x2d, o.reshape(b * t, KEY_DIM), proj, z_block, onorm_g, norm_g[layer], mod, tiles_per_seq,
                          w_out, mlp_w1[layer], mlp_w2[layer])
    y = x2d.reshape(b, t, D_MODEL)
    if emit_state:
        return y, jnp.stack(fin_gdn, axis=1), jnp.stack(fin_hgrn, axis=1)
    return y, None, None


def kernel(x_prompt, x_sample, state_gdn, state_hgrn, c, c_ctx, w_ada, b_ada, norm_g, gdn_w_in, gdn_conv_w,
           gdn_a_log, gdn_dt_bias, gdn_onorm_g, gdn_w_out, hgrn_w_in, hgrn_lb_logits, hgrn_onorm_g, hgrn_w_out,
           mlp_w1, mlp_w2):
    n_dec = c.shape[0]
    n_rows = 16
    cond = jnp.concatenate([c_ctx[None, :], c, jnp.zeros((n_rows - 1 - n_dec, D_MODEL), F32)], axis=0)
    mod = _ada_mod(cond, w_ada, b_ada)
    mod_ctx = mod[:, 0:1, None, :]
    mod_smp = mod[:, 1:1 + n_dec, None, :]

    n_gdn = gdn_w_in.shape[0]
    gate_pad = GATE_LANES - 4 * N_HEADS
    gdn_gate = jnp.pad(gdn_w_in[:, :, GDN_MAIN:], ((0, 0), (0, 0), (0, gate_pad))).astype(BF16)
    lead = jnp.zeros((n_gdn, 2 * N_HEADS), F32)
    tail = jnp.zeros((n_gdn, gate_pad), F32)
    gdn_alog = jnp.concatenate([lead, gdn_a_log.reshape(n_gdn, 2 * N_HEADS), tail], axis=1)[:, None, :]
    gdn_dt = jnp.concatenate([lead, gdn_dt_bias.reshape(n_gdn, 2 * N_HEADS), tail], axis=1)[:, None, :]
    weights = (norm_g, gdn_w_in[:, :, :GDN_MAIN].astype(BF16), gdn_gate, gdn_alog, gdn_dt, gdn_conv_w,
               gdn_onorm_g, gdn_w_out.astype(BF16), hgrn_w_in.astype(BF16), hgrn_lb_logits, hgrn_onorm_g,
               hgrn_w_out.astype(BF16), mlp_w1.astype(BF16), mlp_w2.astype(BF16))

    y_prompt, new_gdn, new_hgrn = _trunk(x_prompt, mod_ctx, False, None, None, False, True, weights)
    y_sample, _, _ = _trunk(x_sample, mod_smp, True, state_gdn, state_hgrn, True, False, weights)
    return (y_prompt, y_sample, new_gdn, new_hgrn)
```

```python
import functools

import jax
import jax.numpy as jnp
from jax import lax
from jax.experimental import pallas as pl
from jax.experimental.pallas import tpu as pltpu

D_MODEL = 1024
N_HEADS = 8
HEAD_DIM = 128
KEY_DIM = N_HEADS * HEAD_DIM
CONV_W = 5
GDN_CHUNK = 64
HGRN_CHUNK = 16
GRID_W = 64
D_FF = 4 * D_MODEL
EPS = 1e-6
GDN_MAIN = 4 * KEY_DIM
GATE_LANES = 128
HGRN_PROJ = 5 * KEY_DIM
HGRN_OUT = 7 * KEY_DIM

ROW_TILE = 512
EPI_ROWS = 128
EPI_COLS = 256
PRE_ROWS = 256
HGRN_BLOCK = 256
HGRN_SCAN_ROWS = 512
HGRN_GROUP = 64
GDN_SEQ = 2
GDN_GROUP = 8
VMEM_LIMIT = 56 * 1024 * 1024

BF16 = jnp.bfloat16
F32 = jnp.float32

_NT = (((1,), (1,)), ((), ()))
_TN = (((0,), (0,)), ((), ()))


def _dot(a, b, dims=None):
    a = a.astype(BF16)
    b = b.astype(BF16)
    if dims is None:
        return jnp.dot(a, b, preferred_element_type=F32)
    return lax.dot_general(a, b, dims, preferred_element_type=F32)


def _segment_sums(x, seg):
    rows = x.shape[0]
    pos = lax.broadcasted_iota(jnp.int32, x.shape, 0) & (seg - 1)
    pre, suf = x, x
    s = 1
    while s < seg:
        pre = pre + jnp.where(pos >= s, pltpu.roll(pre, s, axis=0), 0.0)
        suf = suf + jnp.where(pos + s < seg, pltpu.roll(suf, rows - s, axis=0), 0.0)
        s *= 2
    return pre, suf


def _segment_prefix(x, seg):
    pos = lax.broadcasted_iota(jnp.int32, x.shape, 0) & (seg - 1)
    s = 1
    while s < seg:
        x = x + jnp.where(pos >= s, pltpu.roll(x, s, axis=0), 0.0)
        s *= 2
    return x


def _segment_pick(x, seg, which):
    rows, width = x.shape
    picked = x.reshape(rows // seg, seg, width)[:, which:which + 1, :]
    return jnp.broadcast_to(picked, (rows // seg, seg, width)).reshape(rows, width)


def _segment_last(x, seg):
    return _segment_pick(x, seg, seg - 1)


def _rms(x, g):
    return x * lax.rsqrt(jnp.mean(x * x, axis=-1, keepdims=True) + EPS) * g


def _silu(x):
    return x * jax.nn.sigmoid(x)


def _softplus(x):
    return jnp.maximum(x, 0.0) + jnp.log1p(jnp.exp(-jnp.abs(x)))


def _const_spec(shape):
    zeros = (0,) * len(shape)
    return pl.BlockSpec(shape, lambda *_: zeros, pipeline_mode=pl.Buffered(1))


def _mod_spec(col, tiles_per_seq):
    if tiles_per_seq is None:
        return pl.BlockSpec((None, 1, D_MODEL), lambda i: (0, 0, col))
    return pl.BlockSpec((None, 1, D_MODEL), lambda i: (i // tiles_per_seq, 0, col))


def _params(n_grid):
    return pltpu.CompilerParams(dimension_semantics=("arbitrary",) * n_grid, vmem_limit_bytes=VMEM_LIMIT)


def _ada_kernel(c_ref, w_ref, b_ref, o_ref):
    o_ref[...] = _dot(_silu(c_ref[...]), w_ref[...]) + b_ref[...]


def _ada_mod(cond, w_ada, b_ada):
    depth, _, n = w_ada.shape
    rows = cond.shape[0]
    tn = 1536
    return pl.pallas_call(
        _ada_kernel,
        grid=(depth, n // tn),
        in_specs=[pl.BlockSpec((rows, D_MODEL), lambda l, j: (0, 0)),
                  pl.BlockSpec((None, D_MODEL, tn), lambda l, j: (l, 0, j)),
                  pl.BlockSpec((None, 1, tn), lambda l, j: (l, 0, j))],
        out_specs=pl.BlockSpec((None, rows, tn), lambda l, j: (l, 0, j)),
        out_shape=jax.ShapeDtypeStruct((depth, rows, n), F32),
        compiler_params=_params(2),
        name="ada_mod",
    )(cond, w_ada, b_ada.reshape(depth, 1, n))


def _modulated(x_ref, g_ref, sh_ref, sc_ref, part_rows):
    parts = []
    for r in range(x_ref.shape[0] // part_rows):
        x = x_ref[r * part_rows:(r + 1) * part_rows, :]
        parts.append((_rms(x, g_ref[0:1, :]) * (1.0 + sc_ref[...]) + sh_ref[...]).astype(BF16))
    return parts


def _gdn_inproj_kernel(x_ref, g_ref, sh_ref, sc_ref, w_ref, wg_ref, alog_ref, dt_ref, cw_ref, proj_ref, gate_ref,
                       grow_ref, *, seg):
    n_rows = x_ref.shape[0]
    p_rows = PRE_ROWS
    p_cols = EPI_COLS
    hb = _modulated(x_ref, g_ref, sh_ref, sc_ref, p_rows)
    period = min(seg, p_rows)
    pos = lax.broadcasted_iota(jnp.int32, (period, HEAD_DIM), 0) & (seg - 1)

    def tap_weights(w):
        taps = {}
        for s in (-2, -1, 1, 2):
            valid = (pos + s >= 0) & (pos + s < seg)
            taps[s] = jnp.tile(jnp.where(valid, w[s + 2:s + 3, :], 0.0), (p_rows // period, 1))
        return taps

    def unit(x, scale):
        return x * (lax.rsqrt(jnp.sum(x * x, axis=-1, keepdims=True) + EPS) * scale)

    pieces = [(j, r) for j in range(GDN_MAIN // p_cols) for r in range(n_rows // p_rows)]

    def product(piece):
        j, r = piece
        return jnp.dot(hb[r], w_ref[:, j * p_cols:(j + 1) * p_cols], preferred_element_type=F32)

    def finish(piece, y):
        j, r = piece
        rows = slice(r * p_rows, (r + 1) * p_rows)
        kind = j * p_cols // KEY_DIM
        if kind == 3:
            proj_ref[rows, j * p_cols:(j + 1) * p_cols] = _silu(y)
            return
        for hh in range(p_cols // HEAD_DIM):
            c0 = j * p_cols + hh * HEAD_DIM
            w = cw_ref[:, c0:c0 + HEAD_DIM]
            taps = tap_weights(w)
            x = y[:, hh * HEAD_DIM:(hh + 1) * HEAD_DIM]
            acc = x * w[CONV_W // 2:CONV_W // 2 + 1, :]
            for s in (-2, -1, 1, 2):
                acc = acc + pltpu.roll(x, (-s) % p_rows, axis=0) * taps[s]
            out = _silu(acc)
            if kind == 0:
                out = unit(out, HEAD_DIM ** -0.5)
            elif kind == 1:
                out = unit(out, 1.0)
            proj_ref[rows, c0:c0 + HEAD_DIM] = out

    y_next = product(pieces[0])
    for n, piece in enumerate(pieces):
        y, y_next = y_next, (product(pieces[n + 1]) if n + 1 < len(pieces) else None)
        finish(piece, y)
    raw = jnp.dot(jnp.concatenate(hb, axis=0), wg_ref[...], preferred_element_type=F32)
    beta = jax.nn.sigmoid(raw)
    g = -jnp.exp(alog_ref[...]) * _softplus(raw + dt_ref[...])
    gp, gs = _segment_sums(g, GDN_CHUNK)
    lane = lax.broadcasted_iota(jnp.int32, raw.shape, 1)
    gates = jnp.where(lane < 2 * N_HEADS, beta,
                      jnp.where(lane < 3 * N_HEADS, gp, jnp.where(lane < 4 * N_HEADS, gs, 0.0)))
    gate_ref[...] = gates

    sums_t = gates.T[2 * N_HEADS:4 * N_HEADS, :]
    for m in range(n_rows // GATE_LANES):
        grow_ref[m] = sums_t[:, m * GATE_LANES:(m + 1) * GATE_LANES]


def _gdn_inproj(x2d, norm_g, mod, tiles_per_seq, w_main, w_gate, alog_row, dt_row, conv_w, seg):
    rows = x2d.shape[0]
    assert PRE_ROWS % seg == 0 and ROW_TILE % PRE_ROWS == 0
    return pl.pallas_call(
        functools.partial(_gdn_inproj_kernel, seg=seg),
        grid=(rows // ROW_TILE,),
        in_specs=[pl.BlockSpec((ROW_TILE, D_MODEL), lambda i: (i, 0)),
                  _const_spec((4, D_MODEL)),
                  _mod_spec(0, tiles_per_seq), _mod_spec(1, tiles_per_seq),
                  _const_spec((D_MODEL, GDN_MAIN)), _const_spec((D_MODEL, GATE_LANES)),
                  _const_spec((1, GATE_LANES)), _const_spec((1, GATE_LANES)),
                  _const_spec((CONV_W, 3 * KEY_DIM))],
        out_specs=[pl.BlockSpec((ROW_TILE, GDN_MAIN), lambda i: (i, 0)),
                   pl.BlockSpec((ROW_TILE, GATE_LANES), lambda i: (i, 0)),
                   pl.BlockSpec((ROW_TILE // GATE_LANES, 2 * N_HEADS, GATE_LANES), lambda i: (i, 0, 0))],
        out_shape=[jax.ShapeDtypeStruct((rows, GDN_MAIN), F32),
                   jax.ShapeDtypeStruct((rows, GATE_LANES), F32),
                   jax.ShapeDtypeStruct((rows // GATE_LANES, 2 * N_HEADS, GATE_LANES), F32)],
        compiler_params=_params(1),
        name="gdn_inproj",
    )(x2d, norm_g, mod, mod, w_main, w_gate, alog_row, dt_row, conv_w)


def _hgrn_inproj_kernel(x_ref, g_ref, sh_ref, sc_ref, w_ref, lbl_ref, out_ref, *, layer):
    n_rows = x_ref.shape[0]
    hb = _modulated(x_ref, g_ref, sh_ref, sc_ref, EPI_ROWS)
    c = HGRN_CHUNK
    depth = lbl_ref.shape[0]
    logits = [lbl_ref[l] for l in range(depth)]
    top = functools.reduce(jnp.maximum, logits)
    ex = [jnp.exp(l - top) for l in logits]
    denom = functools.reduce(lambda x, y: x + y, ex)
    lb = functools.reduce(lambda x, y: x + y, ex[:layer + 1]) / denom - ex[0] / denom

    pieces = [(j, r) for j in range(HGRN_PROJ // EPI_COLS) for r in range(n_rows // EPI_ROWS)]

    def product(piece):
        j, r = piece
        return jnp.dot(hb[r], w_ref[:, j * EPI_COLS:(j + 1) * EPI_COLS], preferred_element_type=F32)

    def finish(piece, y):
        j, r = piece
        rows = slice(r * EPI_ROWS, (r + 1) * EPI_ROWS)
        kind, off = divmod(j * EPI_COLS, KEY_DIM)
        if kind in (0, 3, 4):
            dst = {0: 0, 3: 5, 4: 6}[kind] * KEY_DIM + off
            out_ref[rows, dst:dst + EPI_COLS] = y if kind == 3 else _silu(y)
            return
        d = kind - 1
        for hh in range(EPI_COLS // HEAD_DIM):
            c0 = off + hh * HEAD_DIM
            lbd = lb[d:d + 1, c0:c0 + HEAD_DIM]
            fg = lbd + (1.0 - lbd) * jax.nn.sigmoid(y[:, hh * HEAD_DIM:(hh + 1) * HEAD_DIM])
            logf = jnp.log(fg)
            run = _segment_prefix(logf, c)
            if d == 1:
                run = _segment_last(run, c) - run + logf
            out_ref[rows, (1 + 2 * d) * KEY_DIM + c0:(1 + 2 * d) * KEY_DIM + c0 + HEAD_DIM] = 1.0 - fg
            out_ref[rows, (2 + 2 * d) * KEY_DIM + c0:(2 + 2 * d) * KEY_DIM + c0 + HEAD_DIM] = jnp.exp(run)

    y_next = product(pieces[0])
    for n, piece in enumerate(pieces):
        y, y_next = y_next, (product(pieces[n + 1]) if n + 1 < len(pieces) else None)
        finish(piece, y)


def _hgrn_inproj(x2d, norm_g, mod, tiles_per_seq, w_in, lb_logits, layer):
    rows = x2d.shape[0]
    depth = lb_logits.shape[0]
    return pl.pallas_call(
        functools.partial(_hgrn_inproj_kernel, layer=layer),
        grid=(rows // ROW_TILE,),
        in_specs=[pl.BlockSpec((ROW_TILE, D_MODEL), lambda i: (i, 0)),
                  _const_spec((4, D_MODEL)),
                  _mod_spec(0, tiles_per_seq), _mod_spec(1, tiles_per_seq),
                  _const_spec((D_MODEL, HGRN_PROJ)), _const_spec((depth, 2, KEY_DIM))],
        out_specs=pl.BlockSpec((ROW_TILE, HGRN_OUT), lambda i: (i, 0)),
        out_shape=jax.ShapeDtypeStruct((rows, HGRN_OUT), F32),
        compiler_params=_params(1),
        name="hgrn_inproj",
    )(x2d, norm_g, mod, mod, w_in, lb_logits)


def _split_bf16(x):
    hi = lax.bitcast_convert_type(lax.bitcast_convert_type(x, jnp.int32) & jnp.int32(-65536), F32)
    return hi, x - hi


def _pair_inverses(lows, left, diag2, fill):
    c = lows[0].shape[0]

    def block_diag(x):
        return jnp.concatenate([jnp.where(left, x, 0.0), jnp.where(left, 0.0, x)], axis=0)

    def left_operand(hi, lo):
        return jnp.concatenate([hi.astype(BF16), lo.astype(BF16)] * 2, axis=1)

    def right_operand(hi, lo):
        bh, bl = block_diag(hi).astype(BF16), block_diag(lo).astype(BF16)
        return jnp.concatenate([bh, bh, bl, bl], axis=0)

    accs = [jnp.where(diag2, 1.0, 0.0) - x for x in lows]
    parts = [_split_bf16(x) for x in lows]
    powers = [jnp.dot(left_operand(hi, lo), right_operand(hi, lo), preferred_element_type=F32) for hi, lo in parts]
    fill(0)
    levels = c.bit_length() - 2
    for level in range(levels):
        parts = [_split_bf16(x) for x in powers]
        rhs = [right_operand(hi, lo) for hi, lo in parts]
        acc_lhs = [left_operand(*_split_bf16(a)) for a in accs]
        if level + 1 < levels:
            res = [jnp.dot(jnp.concatenate([left_operand(hi, lo), al], axis=0), r, preferred_element_type=F32)
                   for (hi, lo), al, r in zip(parts, acc_lhs, rhs)]
            fill(level + 1)
            powers = [x[:c] for x in res]
            accs = [a + x[c:] for a, x in zip(accs, res)]
        else:
            accs = [a + jnp.dot(al, r, preferred_element_type=F32) for a, al, r in zip(accs, acc_lhs, rhs)]
    return accs


def _gdn_kernel(qn, kn, vn, gcol_ref, grow_ref, *rest, has_init, emit_state, group, seq):
    rest = list(rest)
    s0_ref = rest.pop(0) if has_init else None
    o_ref = rest.pop(0)
    sfin_ref = rest.pop(0) if emit_state else None
    ob, m_s, b_s, q_s, gl_s, st = rest
    t_len = qn.shape[0]
    c = GDN_CHUNK
    n_chunks = t_len // c
    n_heads = qn.shape[1] // HEAD_DIM
    heads = n_heads // seq
    h0 = pl.program_id(1) * n_heads

    lane = lax.broadcasted_iota(jnp.int32, (c, 2 * c), 1)
    row = lax.broadcasted_iota(jnp.int32, (c, 2 * c), 0)
    left = lane < c
    ahead = jnp.where(left, row - lane, lane - c - row)
    left_row = lax.broadcasted_iota(jnp.int32, (1, 2 * c), 1) < c

    def column(tile, idx):
        return jnp.sum(jnp.where(lane == idx, tile, 0.0), axis=-1, keepdims=True)

    def direction_blocks(a0, a1):
        z0, z1 = jnp.zeros(a0.shape, BF16), jnp.zeros(a1.shape, BF16)
        return jnp.concatenate([jnp.concatenate([a0.astype(BF16), z1], axis=1),
                                jnp.concatenate([z0, a1.astype(BF16)], axis=1)], axis=0)

    def load(hh, n):
        h = h0 + hh
        cidx = [n, n_chunks - 1 - n]
        rows = [pl.ds(pl.multiple_of(ci * c, c), c) for ci in cidx]
        lanes = (slice(hh * HEAD_DIM, (hh + 1) * HEAD_DIM) if isinstance(hh, int)
                 else pl.ds(pl.multiple_of(hh * HEAD_DIM, HEAD_DIM), HEAD_DIM))
        e = dict(hh=hh, n=n, rows=rows, q=[qn[r, lanes] for r in rows], k=[kn[r, lanes] for r in rows],
                 v=[vn[r, lanes] for r in rows])
        gtile = [gcol_ref[r, :] for r in rows]
        e["beta"] = [column(gtile[d], d * N_HEADS + h) for d in range(2)]
        e["gc"] = [column(gtile[d], (2 + d) * N_HEADS + h) for d in range(2)]
        e["g_last"] = [e["gc"][0][c - 1:c, :], e["gc"][1][0:1, :]]
        def window(d, ci):
            w = grow_ref[ci // 2, pl.ds(d * N_HEADS + h, 1), :]
            return jnp.where(ci % 2 == d, w, pltpu.roll(w, c, axis=1))
        gr2 = jnp.where(left_row, window(0, cidx[0]), window(1, cidx[1]))
        e["decay2"] = jnp.exp(jnp.where(ahead >= 0, jnp.where(left, e["gc"][0], e["gc"][1]) - gr2, -jnp.inf))
        return e

    def gram(e):
        k, q = e["k"], e["q"]
        kq = _dot(jnp.concatenate([jnp.concatenate(k, axis=1), jnp.concatenate(q, axis=1)], axis=0),
                  direction_blocks(k[0], k[1]), _NT)
        e["low2"] = jnp.where(ahead > 0, kq[:c] * jnp.where(left, e["beta"][0], e["beta"][1]) * e["decay2"], 0.0)
        e["qk2"] = kq[c:] * e["decay2"]

    def solve(e, t2):
        k, v, beta, gc = e["k"], e["v"], e["beta"], e["gc"]
        e["eg"] = [jnp.exp(gc[d]) for d in range(2)]
        rhs = [jnp.concatenate([v[d] * beta[d], k[d] * beta[d] * e["eg"][d]], axis=1) for d in range(2)]
        e["uw"] = jnp.dot(t2.astype(BF16), direction_blocks(*rhs), preferred_element_type=F32)

    def fold(e):
        uw = e["uw"]
        r2 = direction_blocks(uw[:, :2 * HEAD_DIM], uw[:, 2 * HEAD_DIM:])
        e["oq"] = jnp.dot(e["qk2"].astype(BF16), r2, preferred_element_type=F32)
        kt = jnp.concatenate([e["k"][d] * jnp.exp(e["g_last"][d] - e["gc"][d]) for d in range(2)], axis=0)
        e["bm"] = _dot(kt, r2, _TN)

    def store(e):
        hh, n, oq, bm = e["hh"], e["n"], e["oq"], e["bm"]
        for d in range(2):
            base = 2 * d * HEAD_DIM
            ob[hh, d, e["rows"][d], :] = oq[:, base:base + HEAD_DIM]
            q_s[hh, n, d] = (e["q"][d] * e["eg"][d] - oq[:, base + HEAD_DIM:base + 2 * HEAD_DIM]).astype(BF16)
            b_s[hh, n, d] = bm[:, base:base + HEAD_DIM]
            m_s[hh, n, d] = bm[:, base + HEAD_DIM:base + 2 * HEAD_DIM].astype(BF16)
            gl_s[hh, n, d] = jnp.broadcast_to(jnp.exp(e["g_last"][d]), (1, HEAD_DIM))

    def scan_step(base, n):
        for hh in (base + j for j in range(heads)):
            for d in range(2):
                rows = pl.ds(pl.multiple_of((n if d == 0 else n_chunks - 1 - n) * c, c), c)
                state = st[hh, d]
                ms = jnp.dot(jnp.concatenate([m_s[hh, n, d], q_s[hh, n, d]], axis=0), state.astype(BF16),
                             preferred_element_type=F32)
                ob[hh, d, rows, :] = ob[hh, d, rows, :] + ms[HEAD_DIM:]
                st[hh, d] = state * gl_s[hh, n, d] - ms[:HEAD_DIM] + b_s[hh, n, d]

    def prepare(base, first, behind):
        pending = [] if behind is None else [behind[1] + j for j in range(group)]
        n_slots = 8

        def fill(slot):
            for j in range(slot * len(pending) // n_slots, (slot + 1) * len(pending) // n_slots):
                scan_step(behind[0], pending[j])

        chunks = [load(base + hh, first + j) for j in range(group) for hh in range(heads)]
        for e in chunks:
            gram(e)
        fill(0)
        inverses = _pair_inverses([e["low2"] for e in chunks], left, ahead == 0, lambda level: fill(1 + level))
        for e, t2 in zip(chunks, inverses):
            solve(e, t2)
        fill(6)
        for e in chunks:
            fold(e)
        fill(7)
        for e in chunks:
            store(e)

    for hh in range(n_heads):
        for d in range(2):
            st[hh, d] = s0_ref[d, hh] if has_init else jnp.zeros((HEAD_DIM, HEAD_DIM), F32)

    n_steps = n_chunks // group
    prepare(0, 0, None)

    def step(u, carry):
        unit = lambda v: ((v // n_steps) * heads, (v % n_steps) * group) if seq > 1 else (0, v * group)
        prepare(*unit(u), unit(u - 1))
        return carry

    lax.fori_loop(1, seq * n_steps, step, 0)
    for j in range(group):
        scan_step((seq - 1) * heads, (n_steps - 1) * group + j)

    def post(i, carry):
        rows = pl.ds(pl.multiple_of(i * PRE_ROWS, PRE_ROWS), PRE_ROWS)
        for hh in range(n_heads):
            o_ref[rows, hh * HEAD_DIM:(hh + 1) * HEAD_DIM] = ob[hh, 0, rows, :] + ob[hh, 1, rows, :]
        return carry

    lax.fori_loop(0, t_len // PRE_ROWS, post, 0)
    if emit_state:
        for hh in range(n_heads):
            for d in range(2):
                sfin_ref[d, hh] = st[hh, d]


def _gdn_mixer(proj, gates, grow, s0, layer_idx, emit_state):
    b, t, _ = proj.shape
    c = GDN_CHUNK
    n_chunks = t // c
    assert t % PRE_ROWS == 0 and 2 * c == HEAD_DIM
    has_init = s0 is not None
    group = min(GDN_GROUP, n_chunks)
    lock = max(1, GDN_GROUP // n_chunks)
    heads = lock * GDN_SEQ
    width = heads * HEAD_DIM
    col = lambda off: pl.BlockSpec((None, t, width), lambda i, h: (i, 0, off // heads + h))
    in_specs = [col(0), col(N_HEADS), col(2 * N_HEADS),
                pl.BlockSpec((None, t, GATE_LANES), lambda i, h: (i, 0, 0)),
                pl.BlockSpec((t // GATE_LANES, 2 * N_HEADS, GATE_LANES), lambda i, h: (i, 0, 0))]
    args = [proj, proj, proj, gates, grow]
    if has_init:
        in_specs.append(pl.BlockSpec((None, None, 2, heads, HEAD_DIM, HEAD_DIM),
                                     lambda i, h: (i, layer_idx, 0, h, 0, 0)))
        args.append(s0)
    out_specs = [pl.BlockSpec((None, t, width), lambda i, h: (i, 0, h))]
    out_shape = [jax.ShapeDtypeStruct((b, t, KEY_DIM), F32)]
    if emit_state:
        out_specs.append(pl.BlockSpec((None, 2, heads, HEAD_DIM, HEAD_DIM), lambda i, h: (i, 0, h, 0, 0)))
        out_shape.append(jax.ShapeDtypeStruct((b, 2, N_HEADS, HEAD_DIM, HEAD_DIM), F32))
    scratch = [
        pltpu.VMEM((heads, 2, t, HEAD_DIM), F32),
        pltpu.VMEM((heads, n_chunks, 2, HEAD_DIM, HEAD_DIM), BF16),
        pltpu.VMEM((heads, n_chunks, 2, HEAD_DIM, HEAD_DIM), F32),
        pltpu.VMEM((heads, n_chunks, 2, c, HEAD_DIM), BF16),
        pltpu.VMEM((heads, n_chunks, 2, 1, HEAD_DIM), F32),
        pltpu.VMEM((heads, 2, HEAD_DIM, HEAD_DIM), F32)]
    res = pl.pallas_call(
        functools.partial(_gdn_kernel, has_init=has_init, emit_state=emit_state, group=group, seq=GDN_SEQ),
        grid=(b, N_HEADS // heads),
        in_specs=in_specs,
        out_specs=out_specs,
        out_shape=out_shape,
        scratch_shapes=scratch,
        compiler_params=_params(2),
        name="gdn_mixer",
    )(*args)
    return res[0], (res[1] if emit_state else None)


def _hgrn_kernel(q_ref, kf_ref, ef_ref, kb_ref, eb_ref, i_ref, *rest, has_init, emit_state):
    rest = list(rest)
    s0_ref = rest.pop(0) if has_init else None
    o_ref = rest.pop(0)
    sfin_ref = rest.pop(0) if emit_state else None
    qin, kout, ktail, ftot, ob, st = rest
    t_len = q_ref.shape[0]
    c = HGRN_CHUNK

    def pre(d, rows, q):
        k = (kf_ref, kb_ref)[d][rows, :]
        e = (ef_ref, eb_ref)[d][rows, :]
        whole = _segment_pick(e, c, c - 1 if d == 0 else 0)
        k_over_e = k / e
        qin[d, rows, :] = q * e
        kout[d, rows, :] = k_over_e.astype(BF16)
        ktail[d, rows, :] = k_over_e * whole
        ftot[d, rows, :] = whole

    def pre_body(i, carry):
        rows = pl.ds(pl.multiple_of(i * HGRN_BLOCK, HGRN_BLOCK), HGRN_BLOCK)
        q = q_ref[rows, :]
        pre(0, rows, q)
        pre(1, rows, q)
        return carry

    lax.fori_loop(0, t_len // HGRN_BLOCK, pre_body, 0)

    for d in range(2):
        st[d] = s0_ref[d].T if has_init else jnp.zeros((HEAD_DIM, HEAD_DIM), F32)

    grp = HGRN_GROUP
    per = grp // c
    assert per == 4
    blk = min(HGRN_SCAN_ROWS, t_len)
    ri = lax.broadcasted_iota(jnp.int32, (grp, grp), 0)
    ci = lax.broadcasted_iota(jnp.int32, (grp, grp), 1)
    row_chunk = lax.broadcasted_iota(jnp.int32, (grp, HEAD_DIM), 0) // c

    def group_factors(f, d):
        r1, r2, r3 = (pltpu.roll(f, s * c, axis=0) for s in (1, 2, 3))
        prev, nxt = ((r1, r2, r3), (r3, r2, r1)) if d == 0 else ((r3, r2, r1), (r1, r2, r3))
        order = row_chunk if d == 0 else per - 1 - row_chunk
        g1, h1 = prev[0], nxt[0]
        g2, h2 = g1 * prev[1], h1 * nxt[1]
        g3, h3 = g2 * prev[2], h2 * nxt[2]
        before = jnp.where(order == 0, 1.0, jnp.where(order == 1, g1, jnp.where(order == 2, g2, g3)))
        after = jnp.where(order == 3, 1.0, jnp.where(order == 2, h1, jnp.where(order == 1, h2, h3)))
        whole = f[0:1] * f[c:c + 1] * f[2 * c:2 * c + 1] * f[3 * c:3 * c + 1]
        return g1, g2, before, after, whole

    def att_select(d, p1, p234):
        dist = (ri // c - ci // c) if d == 0 else (ci // c - ri // c)
        inside = (ci <= ri) if d == 0 else (ci >= ri)
        return jnp.where((dist == 0) & inside, p1,
                         jnp.where(dist == 1, p234[:grp],
                                   jnp.where(dist == 2, p234[grp:2 * grp],
                                             jnp.where(dist == 3, p234[2 * grp:], 0.0))))

    def body(i, carry):
        ctx = []
        for step in range(blk // grp):
            for d in range(2):
                g_idx = i * (blk // grp) + step
                r0 = pl.multiple_of((g_idx if d == 0 else t_len // grp - 1 - g_idx) * grp, grp)
                rows = pl.ds(r0, grp)
                e = dict(d=d, rows=rows, qi=qin[d, rows, :], ko=kout[d, rows, :], kt=ktail[d, rows, :],
                         v=i_ref[rows, :].astype(BF16))
                e["g1"], e["g2"], e["before"], e["after"], e["whole"] = group_factors(ftot[d, rows, :], d)
                ctx.append(e)
        for e in ctx:
            qi = e["qi"]
            e["p1"] = _dot(qi, e["ko"], _NT)
            e["p234"] = _dot(jnp.concatenate([qi, qi * e["g1"], qi * e["g2"]], axis=0), e["kt"], _NT)
        for e in ctx:
            e["ds"] = _dot(e["v"], e["kt"] * e["after"], _TN)
        for e in ctx:
            e["intra"] = _dot(att_select(e["d"], e["p1"], e["p234"]), e["v"])
        states = [st[0], st[1]]
        for e in ctx:
            d = e["d"]
            ob[d, e["rows"], :] = e["intra"] + _dot(e["qi"] * e["before"], states[d], _NT)
            states[d] = states[d] * e["whole"] + e["ds"]
        st[0], st[1] = states
        return carry

    lax.fori_loop(0, t_len // blk, body, 0)

    def post(i, carry):
        rows = pl.ds(pl.multiple_of(i * PRE_ROWS, PRE_ROWS), PRE_ROWS)
        o_ref[rows, :] = ob[0, rows, :] + ob[1, rows, :]
        return carry

    lax.fori_loop(0, t_len // PRE_ROWS, post, 0)
    if emit_state:
        for d in range(2):
            sfin_ref[d] = st[d].T


def _hgrn_mixer(proj, s0, layer_idx, emit_state):
    b, t, _ = proj.shape
    has_init = s0 is not None
    col = lambda off: pl.BlockSpec((None, t, HEAD_DIM), lambda i, h: (i, 0, off + h))
    in_specs = [col(j * N_HEADS) for j in range(6)]
    args = [proj] * 6
    if has_init:
        in_specs.append(pl.BlockSpec((None, None, 2, None, HEAD_DIM, HEAD_DIM),
                                     lambda i, h: (i, layer_idx, 0, h, 0, 0)))
        args.append(s0)
    out_specs = [pl.BlockSpec((None, t, HEAD_DIM), lambda i, h: (i, 0, h))]
    out_shape = [jax.ShapeDtypeStruct((b, t, KEY_DIM), F32)]
    if emit_state:
        out_specs.append(pl.BlockSpec((None, 2, None, HEAD_DIM, HEAD_DIM), lambda i, h: (i, 0, h, 0, 0)))
        out_shape.append(jax.ShapeDtypeStruct((b, 2, N_HEADS, HEAD_DIM, HEAD_DIM), F32))
    res = pl.pallas_call(
        functools.partial(_hgrn_kernel, has_init=has_init, emit_state=emit_state),
        grid=(b, N_HEADS),
        in_specs=in_specs,
        out_specs=out_specs,
        out_shape=out_shape,
        scratch_shapes=[pltpu.VMEM((2, t, HEAD_DIM), F32), pltpu.VMEM((2, t, HEAD_DIM), BF16),
                        pltpu.VMEM((2, t, HEAD_DIM), F32), pltpu.VMEM((2, t, HEAD_DIM), F32),
                        pltpu.VMEM((2, t, HEAD_DIM), F32), pltpu.VMEM((2, HEAD_DIM, HEAD_DIM), F32)],
        compiler_params=_params(2),
        name="hgrn_mixer",
    )(*args)
    return res[0], (res[1] if emit_state else None)


def _post_kernel(x_ref, o_ref, z_ref, on_ref, g_ref, gt1_ref, sh2_ref, sc2_ref, gt2_ref, wo_ref, w1_ref, w2_ref,
                 y_ref):
    rows = x_ref.shape[0]
    n_parts = 2
    part = rows // n_parts
    mixes = []
    for r in range(n_parts):
        rs = slice(r * part, (r + 1) * part)
        gated = [(_rms(o_ref[rs, h * HEAD_DIM:(h + 1) * HEAD_DIM], on_ref[...])
                  * z_ref[rs, h * HEAD_DIM:(h + 1) * HEAD_DIM]).astype(BF16) for h in range(N_HEADS)]
        mixes.append(jnp.dot(jnp.concatenate(gated, axis=1), wo_ref[...], preferred_element_type=F32))
    mix = jnp.concatenate(mixes, axis=0)
    x1 = x_ref[...] + gt1_ref[...] * _rms(mix, g_ref[1:2, :])
    hb = (_rms(x1, g_ref[2:3, :]) * (1.0 + sc2_ref[...]) + sh2_ref[...]).astype(BF16)
    ff = jnp.zeros(x1.shape, F32)
    for j in range(D_FF // 1024):
        cols = slice(j * 1024, (j + 1) * 1024)
        hid = jnp.maximum(jnp.dot(hb, w1_ref[:, cols], preferred_element_type=F32), 0.0)
        ff = ff + jnp.dot((hid * hid).astype(BF16), w2_ref[cols, :], preferred_element_type=F32)
    y_ref[...] = x1 + gt2_ref[...] * _rms(ff, g_ref[3:4, :])


def _post_mixer(x2d, o2d, proj2d, z_block, onorm_g, norm_g, mod, tiles_per_seq, w_out, w1, w2):
    rows = x2d.shape[0]
    tile = pl.BlockSpec((ROW_TILE, D_MODEL), lambda i: (i, 0))
    return pl.pallas_call(
        _post_kernel,
        grid=(rows // ROW_TILE,),
        in_specs=[tile, tile, pl.BlockSpec((ROW_TILE, D_MODEL), lambda i: (i, z_block)),
                  _const_spec((1, HEAD_DIM)), _const_spec((4, D_MODEL)),
                  _mod_spec(2, tiles_per_seq), _mod_spec(3, tiles_per_seq),
                  _mod_spec(4, tiles_per_seq), _mod_spec(5, tiles_per_seq),
                  _const_spec((D_MODEL, D_MODEL)), _const_spec((D_MODEL, D_FF)), _const_spec((D_FF, D_MODEL))],
        out_specs=tile,
        out_shape=jax.ShapeDtypeStruct((rows, D_MODEL), F32),
        compiler_params=_params(1),
        name="post_mixer",
    )(x2d, o2d, proj2d, onorm_g.reshape(1, HEAD_DIM), norm_g, mod, mod, mod, mod, w_out, w1, w2)


def _trunk(x, mod_rows, per_seq_mod, s_gdn, s_hgrn, grid_conv, emit_state, weights):
    (norm_g, gdn_main, gdn_gate, gdn_alog, gdn_dt, gdn_conv_w, gdn_onorm_g, gdn_w_out,
     hgrn_w_in, hgrn_lb_logits, hgrn_onorm_g, hgrn_w_out, mlp_w1, mlp_w2) = weights
    b, t, _ = x.shape
    tiles_per_seq = t // ROW_TILE if per_seq_mod else None
    x2d = x.reshape(b * t, D_MODEL)
    depth = norm_g.shape[0]
    fin_gdn, fin_hgrn = [], []
    for layer in range(depth):
        j = layer // 2
        mod = mod_rows[layer]
        if layer % 2 == 0:
            proj, gates, grow = _gdn_inproj(x2d, norm_g[layer], mod, tiles_per_seq, gdn_main[j], gdn_gate[j],
                                            gdn_alog[j], gdn_dt[j], gdn_conv_w[j], GRID_W if grid_conv else t)
            o, fin = _gdn_mixer(proj.reshape(b, t, GDN_MAIN), gates.reshape(b, t, GATE_LANES), grow, s_gdn, j,
                                emit_state)
            fin_gdn.append(fin)
            z_block, onorm_g, w_out = 3, gdn_onorm_g[j], gdn_w_out[j]
        else:
            proj = _hgrn_inproj(x2d, norm_g[layer], mod, tiles_per_seq, hgrn_w_in[j], hgrn_lb_logits, layer)
            o, fin = _hgrn_mixer(proj.reshape(b, t, HGRN_OUT), s_hgrn, j, emit_state)
            fin_hgrn.append(fin)
            z_block, onorm_g, w_out = 6, hgrn_onorm_g[j], hgrn_w_out[j]
        x2d = _post_mixer(x2d, o.reshape(b * t, KEY_DIM), proj, z_block, onorm_g, norm_g[layer], mod, tiles_per_seq,
                          w_out, mlp_w1[layer], mlp_w2[layer])
    y = x2d.reshape(b, t, D_MODEL)
    if emit_state:
        return y, jnp.stack(fin_gdn, axis=1), jnp.stack(fin_hgrn, axis=1)
    return y, None, None


def kernel(x_prompt, x_sample, state_gdn, state_hgrn, c, c_ctx, w_ada, b_ada, norm_g, gdn_w_in, gdn_conv_w,
           gdn_a_log, gdn_dt_bias, gdn_onorm_g, gdn_w_out, hgrn_w_in, hgrn_lb_logits, hgrn_onorm_g, hgrn_w_out,
           mlp_w1, mlp_w2):
    n_dec = c.shape[0]
    n_rows = 16
    cond = jnp.concatenate([c_ctx[None, :], c, jnp.zeros((n_rows - 1 - n_dec, D_MODEL), F32)], axis=0)
    mod = _ada_mod(cond, w_ada, b_ada)
    mod_ctx = mod[:, 0:1, None, :]
    mod_smp = mod[:, 1:1 + n_dec, None, :]

    n_gdn = gdn_w_in.shape[0]
    gate_pad = GATE_LANES - 4 * N_HEADS
    gdn_gate = jnp.pad(gdn_w_in[:, :, GDN_MAIN:], ((0, 0), (0, 0), (0, gate_pad))).astype(BF16)
    lead = jnp.zeros((n_gdn, 2 * N_HEADS), F32)
    tail = jnp.zeros((n_gdn, gate_pad), F32)
    gdn_alog = jnp.concatenate([lead, gdn_a_log.reshape(n_gdn, 2 * N_HEADS), tail], axis=1)[:, None, :]
    gdn_dt = jnp.concatenate([lead, gdn_dt_bias.reshape(n_gdn, 2 * N_HEADS), tail], axis=1)[:, None, :]
    weights = (norm_g, gdn_w_in[:, :, :GDN_MAIN].astype(BF16), gdn_gate, gdn_alog, gdn_dt, gdn_conv_w,
               gdn_onorm_g, gdn_w_out.astype(BF16), hgrn_w_in.astype(BF16), hgrn_lb_logits, hgrn_onorm_g,
               hgrn_w_out.astype(BF16), mlp_w1.astype(BF16), mlp_w2.astype(BF16))

    y_prompt, new_gdn, new_hgrn = _trunk(x_prompt, mod_ctx, False, None, None, False, True, weights)
    y_sample, _, _ = _trunk(x_sample, mod_smp, True, state_gdn, state_hgrn, True, False, weights)
    return (y_prompt, y_sample, new_gdn, new_hgrn)
```

```python
import functools

import jax
import jax.numpy as jnp
from jax import lax
from jax.experimental import pallas as pl
from jax.experimental.pallas import tpu as pltpu

D_MODEL = 1024
N_HEADS = 8
HEAD_DIM = 128
KEY_DIM = N_HEADS * HEAD_DIM
CONV_W = 5
GDN_CHUNK = 64
HGRN_CHUNK = 16
GRID_W = 64
D_FF = 4 * D_MODEL
EPS = 1e-6
GDN_MAIN = 4 * KEY_DIM
GATE_LANES = 128
HGRN_PROJ = 5 * KEY_DIM
HGRN_OUT = 7 * KEY_DIM

ROW_TILE = 512
EPI_ROWS = 128
EPI_COLS = 256
PRE_ROWS = 256
HGRN_BLOCK = 256
HGRN_SCAN_ROWS = 1024
HGRN_GROUP = 64
GDN_SEQ = 2
GDN_GROUP = 8
VMEM_LIMIT = 56 * 1024 * 1024

BF16 = jnp.bfloat16
F32 = jnp.float32

_NT = (((1,), (1,)), ((), ()))
_TN = (((0,), (0,)), ((), ()))


def _dot(a, b, dims=None):
    a = a.astype(BF16)
    b = b.astype(BF16)
    if dims is None:
        return jnp.dot(a, b, preferred_element_type=F32)
    return lax.dot_general(a, b, dims, preferred_element_type=F32)


def _segment_sums(x, seg):
    rows = x.shape[0]
    pos = lax.broadcasted_iota(jnp.int32, x.shape, 0) & (seg - 1)
    pre, suf = x, x
    s = 1
    while s < seg:
        pre = pre + jnp.where(pos >= s, pltpu.roll(pre, s, axis=0), 0.0)
        suf = suf + jnp.where(pos + s < seg, pltpu.roll(suf, rows - s, axis=0), 0.0)
        s *= 2
    return pre, suf


def _segment_prefix(x, seg):
    pos = lax.broadcasted_iota(jnp.int32, x.shape, 0) & (seg - 1)
    s = 1
    while s < seg:
        x = x + jnp.where(pos >= s, pltpu.roll(x, s, axis=0), 0.0)
        s *= 2
    return x


def _segment_pick(x, seg, which):
    rows, width = x.shape
    picked = x.reshape(rows // seg, seg, width)[:, which:which + 1, :]
    return jnp.broadcast_to(picked, (rows // seg, seg, width)).reshape(rows, width)


def _segment_last(x, seg):
    return _segment_pick(x, seg, seg - 1)


def _rms(x, g):
    return x * lax.rsqrt(jnp.mean(x * x, axis=-1, keepdims=True) + EPS) * g


def _silu(x):
    return x * jax.nn.sigmoid(x)


def _softplus(x):
    return jnp.maximum(x, 0.0) + jnp.log1p(jnp.exp(-jnp.abs(x)))


def _const_spec(shape):
    zeros = (0,) * len(shape)
    return pl.BlockSpec(shape, lambda *_: zeros, pipeline_mode=pl.Buffered(1))


def _mod_spec(col, tiles_per_seq):
    if tiles_per_seq is None:
        return pl.BlockSpec((None, 1, D_MODEL), lambda i: (0, 0, col))
    return pl.BlockSpec((None, 1, D_MODEL), lambda i: (i // tiles_per_seq, 0, col))


def _params(n_grid):
    return pltpu.CompilerParams(dimension_semantics=("arbitrary",) * n_grid, vmem_limit_bytes=VMEM_LIMIT)


def _ada_kernel(c_ref, w_ref, b_ref, o_ref):
    o_ref[...] = _dot(_silu(c_ref[...]), w_ref[...]) + b_ref[...]


def _ada_mod(cond, w_ada, b_ada):
    depth, _, n = w_ada.shape
    rows = cond.shape[0]
    tn = 1536
    return pl.pallas_call(
        _ada_kernel,
        grid=(depth, n // tn),
        in_specs=[pl.BlockSpec((rows, D_MODEL), lambda l, j: (0, 0)),
                  pl.BlockSpec((None, D_MODEL, tn), lambda l, j: (l, 0, j)),
                  pl.BlockSpec((None, 1, tn), lambda l, j: (l, 0, j))],
        out_specs=pl.BlockSpec((None, rows, tn), lambda l, j: (l, 0, j)),
        out_shape=jax.ShapeDtypeStruct((depth, rows, n), F32),
        compiler_params=_params(2),
        name="ada_mod",
    )(cond, w_ada, b_ada.reshape(depth, 1, n))


def _modulated(x_ref, g_ref, sh_ref, sc_ref, part_rows):
    parts = []
    for r in range(x_ref.shape[0] // part_rows):
        x = x_ref[r * part_rows:(r + 1) * part_rows, :]
        parts.append((_rms(x, g_ref[0:1, :]) * (1.0 + sc_ref[...]) + sh_ref[...]).astype(BF16))
    return parts


def _gdn_inproj_kernel(x_ref, g_ref, sh_ref, sc_ref, w_ref, wg_ref, alog_ref, dt_ref, cw_ref, proj_ref, gate_ref,
                       grow_ref, *, seg):
    n_rows = x_ref.shape[0]
    p_rows = PRE_ROWS
    p_cols = EPI_COLS
    hb = _modulated(x_ref, g_ref, sh_ref, sc_ref, p_rows)
    period = min(seg, p_rows)
    pos = lax.broadcasted_iota(jnp.int32, (period, HEAD_DIM), 0) & (seg - 1)

    def tap_weights(w):
        taps = {}
        for s in (-2, -1, 1, 2):
            valid = (pos + s >= 0) & (pos + s < seg)
            taps[s] = jnp.tile(jnp.where(valid, w[s + 2:s + 3, :], 0.0), (p_rows // period, 1))
        return taps

    def unit(x, scale):
        return x * (lax.rsqrt(jnp.sum(x * x, axis=-1, keepdims=True) + EPS) * scale)

    pieces = [(j, r) for j in range(GDN_MAIN // p_cols) for r in range(n_rows // p_rows)]

    def product(piece):
        j, r = piece
        return jnp.dot(hb[r], w_ref[:, j * p_cols:(j + 1) * p_cols], preferred_element_type=F32)

    def finish(piece, y):
        j, r = piece
        rows = slice(r * p_rows, (r + 1) * p_rows)
        kind = j * p_cols // KEY_DIM
        if kind == 3:
            proj_ref[rows, j * p_cols:(j + 1) * p_cols] = _silu(y)
            return
        for hh in range(p_cols // HEAD_DIM):
            c0 = j * p_cols + hh * HEAD_DIM
            w = cw_ref[:, c0:c0 + HEAD_DIM]
            taps = tap_weights(w)
            x = y[:, hh * HEAD_DIM:(hh + 1) * HEAD_DIM]
            acc = x * w[CONV_W // 2:CONV_W // 2 + 1, :]
            for s in (-2, -1, 1, 2):
                acc = acc + pltpu.roll(x, (-s) % p_rows, axis=0) * taps[s]
            out = _silu(acc)
            if kind == 0:
                out = unit(out, HEAD_DIM ** -0.5)
            elif kind == 1:
                out = unit(out, 1.0)
            proj_ref[rows, c0:c0 + HEAD_DIM] = out

    y_next = product(pieces[0])
    for n, piece in enumerate(pieces):
        y, y_next = y_next, (product(pieces[n + 1]) if n + 1 < len(pieces) else None)
        finish(piece, y)
    raw = jnp.dot(jnp.concatenate(hb, axis=0), wg_ref[...], preferred_element_type=F32)
    beta = jax.nn.sigmoid(raw)
    g = -jnp.exp(alog_ref[...]) * _softplus(raw + dt_ref[...])
    gp, gs = _segment_sums(g, GDN_CHUNK)
    lane = lax.broadcasted_iota(jnp.int32, raw.shape, 1)
    gates = jnp.where(lane < 2 * N_HEADS, beta,
                      jnp.where(lane < 3 * N_HEADS, gp, jnp.where(lane < 4 * N_HEADS, gs, 0.0)))
    gate_ref[...] = gates

    sums_t = gates.T[2 * N_HEADS:4 * N_HEADS, :]
    for m in range(n_rows // GATE_LANES):
        grow_ref[m] = sums_t[:, m * GATE_LANES:(m + 1) * GATE_LANES]


def _gdn_inproj(x2d, norm_g, mod, tiles_per_seq, w_main, w_gate, alog_row, dt_row, conv_w, seg):
    rows = x2d.shape[0]
    assert PRE_ROWS % seg == 0 and ROW_TILE % PRE_ROWS == 0
    return pl.pallas_call(
        functools.partial(_gdn_inproj_kernel, seg=seg),
        grid=(rows // ROW_TILE,),
        in_specs=[pl.BlockSpec((ROW_TILE, D_MODEL), lambda i: (i, 0)),
                  _const_spec((4, D_MODEL)),
                  _mod_spec(0, tiles_per_seq), _mod_spec(1, tiles_per_seq),
                  _const_spec((D_MODEL, GDN_MAIN)), _const_spec((D_MODEL, GATE_LANES)),
                  _const_spec((1, GATE_LANES)), _const_spec((1, GATE_LANES)),
                  _const_spec((CONV_W, 3 * KEY_DIM))],
        out_specs=[pl.BlockSpec((ROW_TILE, GDN_MAIN), lambda i: (i, 0)),
                   pl.BlockSpec((ROW_TILE, GATE_LANES), lambda i: (i, 0)),
                   pl.BlockSpec((ROW_TILE // GATE_LANES, 2 * N_HEADS, GATE_LANES), lambda i: (i, 0, 0))],
        out_shape=[jax.ShapeDtypeStruct((rows, GDN_MAIN), F32),
                   jax.ShapeDtypeStruct((rows, GATE_LANES), F32),
                   jax.ShapeDtypeStruct((rows // GATE_LANES, 2 * N_HEADS, GATE_LANES), F32)],
        compiler_params=_params(1),
        name="gdn_inproj",
    )(x2d, norm_g, mod, mod, w_main, w_gate, alog_row, dt_row, conv_w)


def _hgrn_inproj_kernel(x_ref, g_ref, sh_ref, sc_ref, w_ref, lbl_ref, out_ref, *, layer):
    n_rows = x_ref.shape[0]
    hb = _modulated(x_ref, g_ref, sh_ref, sc_ref, EPI_ROWS)
    c = HGRN_CHUNK
    depth = lbl_ref.shape[0]
    logits = [lbl_ref[l] for l in range(depth)]
    top = functools.reduce(jnp.maximum, logits)
    ex = [jnp.exp(l - top) for l in logits]
    denom = functools.reduce(lambda x, y: x + y, ex)
    lb = functools.reduce(lambda x, y: x + y, ex[:layer + 1]) / denom - ex[0] / denom

    pieces = [(j, r) for j in range(HGRN_PROJ // EPI_COLS) for r in range(n_rows // EPI_ROWS)]

    def product(piece):
        j, r = piece
        return jnp.dot(hb[r], w_ref[:, j * EPI_COLS:(j + 1) * EPI_COLS], preferred_element_type=F32)

    def finish(piece, y):
        j, r = piece
        rows = slice(r * EPI_ROWS, (r + 1) * EPI_ROWS)
        kind, off = divmod(j * EPI_COLS, KEY_DIM)
        if kind in (0, 3, 4):
            dst = {0: 0, 3: 5, 4: 6}[kind] * KEY_DIM + off
            out_ref[rows, dst:dst + EPI_COLS] = y if kind == 3 else _silu(y)
            return
        d = kind - 1
        for hh in range(EPI_COLS // HEAD_DIM):
            c0 = off + hh * HEAD_DIM
            lbd = lb[d:d + 1, c0:c0 + HEAD_DIM]
            fg = lbd + (1.0 - lbd) * jax.nn.sigmoid(y[:, hh * HEAD_DIM:(hh + 1) * HEAD_DIM])
            logf = jnp.log(fg)
            run = _segment_prefix(logf, c)
            if d == 1:
                run = _segment_last(run, c) - run + logf
            out_ref[rows, (1 + 2 * d) * KEY_DIM + c0:(1 + 2 * d) * KEY_DIM + c0 + HEAD_DIM] = 1.0 - fg
            out_ref[rows, (2 + 2 * d) * KEY_DIM + c0:(2 + 2 * d) * KEY_DIM + c0 + HEAD_DIM] = jnp.exp(run)

    y_next = product(pieces[0])
    for n, piece in enumerate(pieces):
        y, y_next = y_next, (product(pieces[n + 1]) if n + 1 < len(pieces) else None)
        finish(piece, y)


def _hgrn_inproj(x2d, norm_g, mod, tiles_per_seq, w_in, lb_logits, layer):
    rows = x2d.shape[0]
    depth = lb_logits.shape[0]
    return pl.pallas_call(
        functools.partial(_hgrn_inproj_kernel, layer=layer),
        grid=(rows // ROW_TILE,),
        in_specs=[pl.BlockSpec((ROW_TILE, D_MODEL), lambda i: (i, 0)),
                  _const_spec((4, D_MODEL)),
                  _mod_spec(0, tiles_per_seq), _mod_spec(1, tiles_per_seq),
                  _const_spec((D_MODEL, HGRN_PROJ)), _const_spec((depth, 2, KEY_DIM))],
        out_specs=pl.BlockSpec((ROW_TILE, HGRN_OUT), lambda i: (i, 0)),
        out_shape=jax.ShapeDtypeStruct((rows, HGRN_OUT), F32),
        compiler_params=_params(1),
        name="hgrn_inproj",
    )(x2d, norm_g, mod, mod, w_in, lb_logits)


def _split_bf16(x):
    hi = lax.bitcast_convert_type(lax.bitcast_convert_type(x, jnp.int32) & jnp.int32(-65536), F32)
    return hi, x - hi


def _pair_inverses(lows, left, diag2, fill):
    c = lows[0].shape[0]

    def block_diag(x):
        return jnp.concatenate([jnp.where(left, x, 0.0), jnp.where(left, 0.0, x)], axis=0)

    def left_operand(hi, lo):
        return jnp.concatenate([hi.astype(BF16), lo.astype(BF16)] * 2, axis=1)

    def right_operand(hi, lo):
        bh, bl = block_diag(hi).astype(BF16), block_diag(lo).astype(BF16)
        return jnp.concatenate([bh, bh, bl, bl], axis=0)

    accs = [jnp.where(diag2, 1.0, 0.0) - x for x in lows]
    parts = [_split_bf16(x) for x in lows]
    powers = [jnp.dot(left_operand(hi, lo), right_operand(hi, lo), preferred_element_type=F32) for hi, lo in parts]
    fill(0)
    levels = c.bit_length() - 2
    for level in range(levels):
        parts = [_split_bf16(x) for x in powers]
        rhs = [right_operand(hi, lo) for hi, lo in parts]
        acc_lhs = [left_operand(*_split_bf16(a)) for a in accs]
        if level + 1 < levels:
            res = [jnp.dot(jnp.concatenate([left_operand(hi, lo), al], axis=0), r, preferred_element_type=F32)
                   for (hi, lo), al, r in zip(parts, acc_lhs, rhs)]
            fill(level + 1)
            powers = [x[:c] for x in res]
            accs = [a + x[c:] for a, x in zip(accs, res)]
        else:
            accs = [a + jnp.dot(al, r, preferred_element_type=F32) for a, al, r in zip(accs, acc_lhs, rhs)]
    return accs


def _gdn_kernel(qn, kn, vn, gcol_ref, grow_ref, *rest, has_init, emit_state, group, seq):
    rest = list(rest)
    s0_ref = rest.pop(0) if has_init else None
    o_ref = rest.pop(0)
    sfin_ref = rest.pop(0) if emit_state else None
    ob, m_s, b_s, q_s, gl_s, st = rest
    t_len = qn.shape[0]
    c = GDN_CHUNK
    n_chunks = t_len // c
    n_heads = qn.shape[1] // HEAD_DIM
    heads = n_heads // seq
    h0 = pl.program_id(1) * n_heads

    lane = lax.broadcasted_iota(jnp.int32, (c, 2 * c), 1)
    row = lax.broadcasted_iota(jnp.int32, (c, 2 * c), 0)
    left = lane < c
    ahead = jnp.where(left, row - lane, lane - c - row)
    left_row = lax.broadcasted_iota(jnp.int32, (1, 2 * c), 1) < c

    def column(tile, idx):
        return jnp.sum(jnp.where(lane == idx, tile, 0.0), axis=-1, keepdims=True)

    def direction_blocks(a0, a1):
        z0, z1 = jnp.zeros(a0.shape, BF16), jnp.zeros(a1.shape, BF16)
        return jnp.concatenate([jnp.concatenate([a0.astype(BF16), z1], axis=1),
                                jnp.concatenate([z0, a1.astype(BF16)], axis=1)], axis=0)

    def load(hh, n):
        h = h0 + hh
        cidx = [n, n_chunks - 1 - n]
        rows = [pl.ds(pl.multiple_of(ci * c, c), c) for ci in cidx]
        lanes = (slice(hh * HEAD_DIM, (hh + 1) * HEAD_DIM) if isinstance(hh, int)
                 else pl.ds(pl.multiple_of(hh * HEAD_DIM, HEAD_DIM), HEAD_DIM))
        e = dict(hh=hh, n=n, rows=rows, q=[qn[r, lanes] for r in rows], k=[kn[r, lanes] for r in rows],
                 v=[vn[r, lanes] for r in rows])
        gtile = [gcol_ref[r, :] for r in rows]
        e["beta"] = [column(gtile[d], d * N_HEADS + h) for d in range(2)]
        e["gc"] = [column(gtile[d], (2 + d) * N_HEADS + h) for d in range(2)]
        e["g_last"] = [e["gc"][0][c - 1:c, :], e["gc"][1][0:1, :]]
        def window(d, ci):
            w = grow_ref[ci // 2, pl.ds(d * N_HEADS + h, 1), :]
            return jnp.where(ci % 2 == d, w, pltpu.roll(w, c, axis=1))
        gr2 = jnp.where(left_row, window(0, cidx[0]), window(1, cidx[1]))
        e["decay2"] = jnp.exp(jnp.where(ahead >= 0, jnp.where(left, e["gc"][0], e["gc"][1]) - gr2, -jnp.inf))
        return e

    def gram(e):
        k, q = e["k"], e["q"]
        kq = _dot(jnp.concatenate([jnp.concatenate(k, axis=1), jnp.concatenate(q, axis=1)], axis=0),
                  direction_blocks(k[0], k[1]), _NT)
        e["low2"] = jnp.where(ahead > 0, kq[:c] * jnp.where(left, e["beta"][0], e["beta"][1]) * e["decay2"], 0.0)
        e["qk2"] = kq[c:] * e["decay2"]

    def solve(e, t2):
        k, v, beta, gc = e["k"], e["v"], e["beta"], e["gc"]
        e["eg"] = [jnp.exp(gc[d]) for d in range(2)]
        rhs = [jnp.concatenate([v[d] * beta[d], k[d] * beta[d] * e["eg"][d]], axis=1) for d in range(2)]
        e["uw"] = jnp.dot(t2.astype(BF16), direction_blocks(*rhs), preferred_element_type=F32)

    def fold(e):
        uw = e["uw"]
        r2 = direction_blocks(uw[:, :2 * HEAD_DIM], uw[:, 2 * HEAD_DIM:])
        e["oq"] = jnp.dot(e["qk2"].astype(BF16), r2, preferred_element_type=F32)
        kt = jnp.concatenate([e["k"][d] * jnp.exp(e["g_last"][d] - e["gc"][d]) for d in range(2)], axis=0)
        e["bm"] = _dot(kt, r2, _TN)

    def store(e):
        hh, n, oq, bm = e["hh"], e["n"], e["oq"], e["bm"]
        for d in range(2):
            base = 2 * d * HEAD_DIM
            ob[hh, d, e["rows"][d], :] = oq[:, base:base + HEAD_DIM]
            q_s[hh, n, d] = (e["q"][d] * e["eg"][d] - oq[:, base + HEAD_DIM:base + 2 * HEAD_DIM]).astype(BF16)
            b_s[hh, n, d] = bm[:, base:base + HEAD_DIM]
            m_s[hh, n, d] = bm[:, base + HEAD_DIM:base + 2 * HEAD_DIM].astype(BF16)
            gl_s[hh, n, d] = jnp.broadcast_to(jnp.exp(e["g_last"][d]), (1, HEAD_DIM))

    def scan_step(base, n):
        for hh in (base + j for j in range(heads)):
            for d in range(2):
                rows = pl.ds(pl.multiple_of((n if d == 0 else n_chunks - 1 - n) * c, c), c)
                state = st[hh, d]
                ms = jnp.dot(jnp.concatenate([m_s[hh, n, d], q_s[hh, n, d]], axis=0), state.astype(BF16),
                             preferred_element_type=F32)
                ob[hh, d, rows, :] = ob[hh, d, rows, :] + ms[HEAD_DIM:]
                st[hh, d] = state * gl_s[hh, n, d] - ms[:HEAD_DIM] + b_s[hh, n, d]

    def prepare(base, first, behind):
        pending = [] if behind is None else [behind[1] + j for j in range(group)]
        n_slots = 8

        def fill(slot):
            for j in range(slot * len(pending) // n_slots, (slot + 1) * len(pending) // n_slots):
                scan_step(behind[0], pending[j])

        chunks = [load(base + hh, first + j) for j in range(group) for hh in range(heads)]
        for e in chunks:
            gram(e)
        fill(0)
        inverses = _pair_inverses([e["low2"] for e in chunks], left, ahead == 0, lambda level: fill(1 + level))
        for e, t2 in zip(chunks, inverses):
            solve(e, t2)
        fill(6)
        for e in chunks:
            fold(e)
        fill(7)
        for e in chunks:
            store(e)

    for hh in range(n_heads):
        for d in range(2):
            st[hh, d] = s0_ref[d, hh] if has_init else jnp.zeros((HEAD_DIM, HEAD_DIM), F32)

    n_steps = n_chunks // group
    prepare(0, 0, None)

    def step(u, carry):
        unit = lambda v: ((v // n_steps) * heads, (v % n_steps) * group) if seq > 1 else (0, v * group)
        prepare(*unit(u), unit(u - 1))
        return carry

    lax.fori_loop(1, seq * n_steps, step, 0)
    for j in range(group):
        scan_step((seq - 1) * heads, (n_steps - 1) * group + j)

    def post(i, carry):
        rows = pl.ds(pl.multiple_of(i * PRE_ROWS, PRE_ROWS), PRE_ROWS)
        for hh in range(n_heads):
            o_ref[rows, hh * HEAD_DIM:(hh + 1) * HEAD_DIM] = ob[hh, 0, rows, :] + ob[hh, 1, rows, :]
        return carry

    lax.fori_loop(0, t_len // PRE_ROWS, post, 0)
    if emit_state:
        for hh in range(n_heads):
            for d in range(2):
                sfin_ref[d, hh] = st[hh, d]


def _gdn_mixer(proj, gates, grow, s0, layer_idx, emit_state):
    b, t, _ = proj.shape
    c = GDN_CHUNK
    n_chunks = t // c
    assert t % PRE_ROWS == 0 and 2 * c == HEAD_DIM
    has_init = s0 is not None
    group = min(GDN_GROUP, n_chunks)
    lock = max(1, GDN_GROUP // n_chunks)
    seq = GDN_SEQ if lock == 1 else N_HEADS // lock
    heads = lock * seq
    width = heads * HEAD_DIM
    col = lambda off: pl.BlockSpec((None, t, width), lambda i, h: (i, 0, off // heads + h))
    in_specs = [col(0), col(N_HEADS), col(2 * N_HEADS),
                pl.BlockSpec((None, t, GATE_LANES), lambda i, h: (i, 0, 0)),
                pl.BlockSpec((t // GATE_LANES, 2 * N_HEADS, GATE_LANES), lambda i, h: (i, 0, 0))]
    args = [proj, proj, proj, gates, grow]
    if has_init:
        in_specs.append(pl.BlockSpec((None, None, 2, heads, HEAD_DIM, HEAD_DIM),
                                     lambda i, h: (i, layer_idx, 0, h, 0, 0)))
        args.append(s0)
    out_specs = [pl.BlockSpec((None, t, width), lambda i, h: (i, 0, h))]
    out_shape = [jax.ShapeDtypeStruct((b, t, KEY_DIM), F32)]
    if emit_state:
        out_specs.append(pl.BlockSpec((None, 2, heads, HEAD_DIM, HEAD_DIM), lambda i, h: (i, 0, h, 0, 0)))
        out_shape.append(jax.ShapeDtypeStruct((b, 2, N_HEADS, HEAD_DIM, HEAD_DIM), F32))
    scratch = [
        pltpu.VMEM((heads, 2, t, HEAD_DIM), F32),
        pltpu.VMEM((heads, n_chunks, 2, HEAD_DIM, HEAD_DIM), BF16),
        pltpu.VMEM((heads, n_chunks, 2, HEAD_DIM, HEAD_DIM), F32),
        pltpu.VMEM((heads, n_chunks, 2, c, HEAD_DIM), BF16),
        pltpu.VMEM((heads, n_chunks, 2, 1, HEAD_DIM), F32),
        pltpu.VMEM((heads, 2, HEAD_DIM, HEAD_DIM), F32)]
    res = pl.pallas_call(
        functools.partial(_gdn_kernel, has_init=has_init, emit_state=emit_state, group=group, seq=seq),
        grid=(b, N_HEADS // heads),
        in_specs=in_specs,
        out_specs=out_specs,
        out_shape=out_shape,
        scratch_shapes=scratch,
        compiler_params=_params(2),
        name="gdn_mixer",
    )(*args)
    return res[0], (res[1] if emit_state else None)


def _hgrn_kernel(q_ref, kf_ref, ef_ref, kb_ref, eb_ref, i_ref, *rest, has_init, emit_state):
    rest = list(rest)
    s0_ref = rest.pop(0) if has_init else None
    o_ref = rest.pop(0)
    sfin_ref = rest.pop(0) if emit_state else None
    qin, kout, ktail, ftot, ob, st = rest
    t_len = q_ref.shape[0]
    c = HGRN_CHUNK

    def pre(d, rows, q):
        k = (kf_ref, kb_ref)[d][rows, :]
        e = (ef_ref, eb_ref)[d][rows, :]
        whole = _segment_pick(e, c, c - 1 if d == 0 else 0)
        k_over_e = k / e
        qin[d, rows, :] = q * e
        kout[d, rows, :] = k_over_e.astype(BF16)
        ktail[d, rows, :] = k_over_e * whole
        ftot[d, rows, :] = whole

    def pre_body(i, carry):
        rows = pl.ds(pl.multiple_of(i * HGRN_BLOCK, HGRN_BLOCK), HGRN_BLOCK)
        q = q_ref[rows, :]
        pre(0, rows, q)
        pre(1, rows, q)
        return carry

    lax.fori_loop(0, t_len // HGRN_BLOCK, pre_body, 0)

    for d in range(2):
        st[d] = s0_ref[d].T if has_init else jnp.zeros((HEAD_DIM, HEAD_DIM), F32)

    grp = HGRN_GROUP
    per = grp // c
    assert per == 4
    blk = min(HGRN_SCAN_ROWS, t_len)
    ri = lax.broadcasted_iota(jnp.int32, (grp, grp), 0)
    ci = lax.broadcasted_iota(jnp.int32, (grp, grp), 1)
    row_chunk = lax.broadcasted_iota(jnp.int32, (grp, HEAD_DIM), 0) // c

    def group_factors(f, d):
        r1, r2, r3 = (pltpu.roll(f, s * c, axis=0) for s in (1, 2, 3))
        prev, nxt = ((r1, r2, r3), (r3, r2, r1)) if d == 0 else ((r3, r2, r1), (r1, r2, r3))
        order = row_chunk if d == 0 else per - 1 - row_chunk
        g1, h1 = prev[0], nxt[0]
        g2, h2 = g1 * prev[1], h1 * nxt[1]
        g3, h3 = g2 * prev[2], h2 * nxt[2]
        before = jnp.where(order == 0, 1.0, jnp.where(order == 1, g1, jnp.where(order == 2, g2, g3)))
        after = jnp.where(order == 3, 1.0, jnp.where(order == 2, h1, jnp.where(order == 1, h2, h3)))
        whole = f[0:1] * f[c:c + 1] * f[2 * c:2 * c + 1] * f[3 * c:3 * c + 1]
        return g1, g2, before, after, whole

    def att_select(d, p1, p234):
        dist = (ri // c - ci // c) if d == 0 else (ci // c - ri // c)
        inside = (ci <= ri) if d == 0 else (ci >= ri)
        return jnp.where((dist == 0) & inside, p1,
                         jnp.where(dist == 1, p234[:grp],
                                   jnp.where(dist == 2, p234[grp:2 * grp],
                                             jnp.where(dist == 3, p234[2 * grp:], 0.0))))

    def body(i, carry):
        ctx = []
        for step in range(blk // grp):
            for d in range(2):
                g_idx = i * (blk // grp) + step
                r0 = pl.multiple_of((g_idx if d == 0 else t_len // grp - 1 - g_idx) * grp, grp)
                rows = pl.ds(r0, grp)
                e = dict(d=d, rows=rows, qi=qin[d, rows, :], ko=kout[d, rows, :], kt=ktail[d, rows, :],
                         v=i_ref[rows, :].astype(BF16))
                e["g1"], e["g2"], e["before"], e["after"], e["whole"] = group_factors(ftot[d, rows, :], d)
                ctx.append(e)
        for e in ctx:
            qi = e["qi"]
            e["p1"] = _dot(qi, e["ko"], _NT)
            e["p234"] = _dot(jnp.concatenate([qi, qi * e["g1"], qi * e["g2"]], axis=0), e["kt"], _NT)
        for e in ctx:
            e["ds"] = _dot(e["v"], e["kt"] * e["after"], _TN)
        for e in ctx:
            e["intra"] = _dot(att_select(e["d"], e["p1"], e["p234"]), e["v"])
        states = [st[0], st[1]]
        for e in ctx:
            d = e["d"]
            ob[d, e["rows"], :] = e["intra"] + _dot(e["qi"] * e["before"], states[d], _NT)
            states[d] = states[d] * e["whole"] + e["ds"]
        st[0], st[1] = states
        return carry

    lax.fori_loop(0, t_len // blk, body, 0)

    def post(i, carry):
        rows = pl.ds(pl.multiple_of(i * PRE_ROWS, PRE_ROWS), PRE_ROWS)
        o_ref[rows, :] = ob[0, rows, :] + ob[1, rows, :]
        return carry

    lax.fori_loop(0, t_len // PRE_ROWS, post, 0)
    if emit_state:
        for d in range(2):
            sfin_ref[d] = st[d].T


def _hgrn_mixer(proj, s0, layer_idx, emit_state):
    b, t, _ = proj.shape
    has_init = s0 is not None
    col = lambda off: pl.BlockSpec((None, t, HEAD_DIM), lambda i, h: (i, 0, off + h))
    in_specs = [col(j * N_HEADS) for j in range(6)]
    args = [proj] * 6
    if has_init:
        in_specs.append(pl.BlockSpec((None, None, 2, None, HEAD_DIM, HEAD_DIM),
                                     lambda i, h: (i, layer_idx, 0, h, 0, 0)))
        args.append(s0)
    out_specs = [pl.BlockSpec((None, t, HEAD_DIM), lambda i, h: (i, 0, h))]
    out_shape = [jax.ShapeDtypeStruct((b, t, KEY_DIM), F32)]
    if emit_state:
        out_specs.append(pl.BlockSpec((None, 2, None, HEAD_DIM, HEAD_DIM), lambda i, h: (i, 0, h, 0, 0)))
        out_shape.append(jax.ShapeDtypeStruct((b, 2, N_HEADS, HEAD_DIM, HEAD_DIM), F32))
    res = pl.pallas_call(
        functools.partial(_hgrn_kernel, has_init=has_init, emit_state=emit_state),
        grid=(b, N_HEADS),
        in_specs=in_specs,
        out_specs=out_specs,
        out_shape=out_shape,
        scratch_shapes=[pltpu.VMEM((2, t, HEAD_DIM), F32), pltpu.VMEM((2, t, HEAD_DIM), BF16),
                        pltpu.VMEM((2, t, HEAD_DIM), F32), pltpu.VMEM((2, t, HEAD_DIM), F32),
                        pltpu.VMEM((2, t, HEAD_DIM), F32), pltpu.VMEM((2, HEAD_DIM, HEAD_DIM), F32)],
        compiler_params=_params(2),
        name="hgrn_mixer",
    )(*args)
    return res[0], (res[1] if emit_state else None)


def _post_kernel(x_ref, o_ref, z_ref, on_ref, g_ref, gt1_ref, sh2_ref, sc2_ref, gt2_ref, wo_ref, w1_ref, w2_ref,
                 y_ref):
    rows = x_ref.shape[0]
    n_parts = 2
    part = rows // n_parts
    mixes = []
    for r in range(n_parts):
        rs = slice(r * part, (r + 1) * part)
        gated = [(_rms(o_ref[rs, h * HEAD_DIM:(h + 1) * HEAD_DIM], on_ref[...])
                  * z_ref[rs, h * HEAD_DIM:(h + 1) * HEAD_DIM]).astype(BF16) for h in range(N_HEADS)]
        mixes.append(jnp.dot(jnp.concatenate(gated, axis=1), wo_ref[...], preferred_element_type=F32))
    mix = jnp.concatenate(mixes, axis=0)
    x1 = x_ref[...] + gt1_ref[...] * _rms(mix, g_ref[1:2, :])
    hb = (_rms(x1, g_ref[2:3, :]) * (1.0 + sc2_ref[...]) + sh2_ref[...]).astype(BF16)
    ff = jnp.zeros(x1.shape, F32)
    for j in range(D_FF // 1024):
        cols = slice(j * 1024, (j + 1) * 1024)
        hid = jnp.maximum(jnp.dot(hb, w1_ref[:, cols], preferred_element_type=F32), 0.0)
        ff = ff + jnp.dot((hid * hid).astype(BF16), w2_ref[cols, :], preferred_element_type=F32)
    y_ref[...] = x1 + gt2_ref[...] * _rms(ff, g_ref[3:4, :])


def _post_mixer(x2d, o2d, proj2d, z_block, onorm_g, norm_g, mod, tiles_per_seq, w_out, w1, w2):
    rows = x2d.shape[0]
    tile = pl.BlockSpec((ROW_TILE, D_MODEL), lambda i: (i, 0))
    return pl.pallas_call(
        _post_kernel,
        grid=(rows // ROW_TILE,),
        in_specs=[tile, tile, pl.BlockSpec((ROW_TILE, D_MODEL), lambda i: (i, z_block)),
                  _const_spec((1, HEAD_DIM)), _const_spec((4, D_MODEL)),
                  _mod_spec(2, tiles_per_seq), _mod_spec(3, tiles_per_seq),
                  _mod_spec(4, tiles_per_seq), _mod_spec(5, tiles_per_seq),
                  _const_spec((D_MODEL, D_MODEL)), _const_spec((D_MODEL, D_FF)), _const_spec((D_FF, D_MODEL))],
        out_specs=tile,
        out_shape=jax.ShapeDtypeStruct((rows, D_MODEL), F32),
        compiler_params=_params(1),
        name="post_mixer",
    )(x2d, o2d, proj2d, onorm_g.reshape(1, HEAD_DIM), norm_g, mod, mod, mod, mod, w_out, w1, w2)


def _trunk(x, mod_rows, per_seq_mod, s_gdn, s_hgrn, grid_conv, emit_state, weights):
    (norm_g, gdn_main, gdn_gate, gdn_alog, gdn_dt, gdn_conv_w, gdn_onorm_g, gdn_w_out,
     hgrn_w_in, hgrn_lb_logits, hgrn_onorm_g, hgrn_w_out, mlp_w1, mlp_w2) = weights
    b, t, _ = x.shape
    tiles_per_seq = t // ROW_TILE if per_seq_mod else None
    x2d = x.reshape(b * t, D_MODEL)
    depth = norm_g.shape[0]
    fin_gdn, fin_hgrn = [], []
    for layer in range(depth):
        j = layer // 2
        mod = mod_rows[layer]
        if layer % 2 == 0:
            proj, gates, grow = _gdn_inproj(x2d, norm_g[layer], mod, tiles_per_seq, gdn_main[j], gdn_gate[j],
                                            gdn_alog[j], gdn_dt[j], gdn_conv_w[j], GRID_W if grid_conv else t)
            o, fin = _gdn_mixer(proj.reshape(b, t, GDN_MAIN), gates.reshape(b, t, GATE_LANES), grow, s_gdn, j,
                                emit_state)
            fin_gdn.append(fin)
            z_block, onorm_g, w_out = 3, gdn_onorm_g[j], gdn_w_out[j]
        else:
            proj = _hgrn_inproj(x2d, norm_g[layer], mod, tiles_per_seq, hgrn_w_in[j], hgrn_lb_logits, layer)
            o, fin = _hgrn_mixer(proj.reshape(b, t, HGRN_OUT), s_hgrn, j, emit_state)
            fin_hgrn.append(fin)
            z_block, onorm_g, w_out = 6, hgrn_onorm_g[j], hgrn_w_out[j]
        x2d = _post_mixer(x2d, o.reshape(b * t, KEY_DIM), proj, z_block, onorm_g, norm_g[layer], mod, tiles_per_seq,
                          w_out, mlp_w1[layer], mlp_w2[layer])
    y = x2d.reshape(b, t, D_MODEL)
    if emit_state:
        return y, jnp.stack(fin_gdn, axis=1), jnp.stack(fin_hgrn, axis=1)
    return y, None, None


def kernel(x_prompt, x_sample, state_gdn, state_hgrn, c, c_ctx, w_ada, b_ada, norm_g, gdn_w_in, gdn_conv_w,
           gdn_a_log, gdn_dt_bias, gdn_onorm_g, gdn_w_out, hgrn_w_in, hgrn_lb_logits, hgrn_onorm_g, hgrn_w_out,
           mlp_w1, mlp_w2):
    n_dec = c.shape[0]
    n_rows = 16
    cond = jnp.concatenate([c_ctx[None, :], c, jnp.zeros((n_rows - 1 - n_dec, D_MODEL), F32)], axis=0)
    mod = _ada_mod(cond, w_ada, b_ada)
    mod_ctx = mod[:, 0:1, None, :]
    mod_smp = mod[:, 1:1 + n_dec, None, :]

    n_gdn = gdn_w_in.shape[0]
    gate_pad = GATE_LANES - 4 * N_HEADS
    gdn_gate = jnp.pad(gdn_w_in[:, :, GDN_MAIN:], ((0, 0), (0, 0), (0, gate_pad))).astype(BF16)
    lead = jnp.zeros((n_gdn, 2 * N_HEADS), F32)
    tail = jnp.zeros((n_gdn, gate_pad), F32)
    gdn_alog = jnp.concatenate([lead, gdn_a_log.reshape(n_gdn, 2 * N_HEADS), tail], axis=1)[:, None, :]
    gdn_dt = jnp.concatenate([lead, gdn_dt_bias.reshape(n_gdn, 2 * N_HEADS), tail], axis=1)[:, None, :]
    weights = (norm_g, gdn_w_in[:, :, :GDN_MAIN].astype(BF16), gdn_gate, gdn_alog, gdn_dt, gdn_conv_w,
               gdn_onorm_g, gdn_w_out.astype(BF16), hgrn_w_in.astype(BF16), hgrn_lb_logits, hgrn_onorm_g,
               hgrn_w_out.astype(BF16), mlp_w1.astype(BF16), mlp_w2.astype(BF16))

    y_prompt, new_gdn, new_hgrn = _trunk(x_prompt, mod_ctx, False, None, None, False, True, weights)
    y_sample, _, _ = _trunk(x_sample, mod_smp, True, state_gdn, state_hgrn, True, False, weights)
    return (y_prompt, y_sample, new_gdn, new_hgrn)
```

```python
import functools

import jax
import jax.numpy as jnp
from jax import lax
from jax.experimental import pallas as pl
from jax.experimental.pallas import tpu as pltpu

D_MODEL = 1024
N_HEADS = 8
HEAD_DIM = 128
KEY_DIM = N_HEADS * HEAD_DIM
CONV_W = 5
GDN_CHUNK = 64
HGRN_CHUNK = 16
GRID_W = 64
D_FF = 4 * D_MODEL
EPS = 1e-6
GDN_MAIN = 4 * KEY_DIM
GATE_LANES = 128
HGRN_PROJ = 5 * KEY_DIM
HGRN_OUT = 7 * KEY_DIM

ROW_TILE = 512
EPI_ROWS = 128
EPI_COLS = 256
PRE_ROWS = 256
HGRN_BLOCK = 256
HGRN_SCAN_ROWS = 1024
HGRN_GROUP = 64
GDN_SEQ = 2
GDN_GROUP = 8
VMEM_LIMIT = 56 * 1024 * 1024

BF16 = jnp.bfloat16
F32 = jnp.float32

_NT = (((1,), (1,)), ((), ()))
_TN = (((0,), (0,)), ((), ()))


def _dot(a, b, dims=None):
    a = a.astype(BF16)
    b = b.astype(BF16)
    if dims is None:
        return jnp.dot(a, b, preferred_element_type=F32)
    return lax.dot_general(a, b, dims, preferred_element_type=F32)


def _segment_sums(x, seg):
    rows = x.shape[0]
    pos = lax.broadcasted_iota(jnp.int32, x.shape, 0) & (seg - 1)
    pre, suf = x, x
    s = 1
    while s < seg:
        pre = pre + jnp.where(pos >= s, pltpu.roll(pre, s, axis=0), 0.0)
        suf = suf + jnp.where(pos + s < seg, pltpu.roll(suf, rows - s, axis=0), 0.0)
        s *= 2
    return pre, suf


def _segment_prefix(x, seg):
    pos = lax.broadcasted_iota(jnp.int32, x.shape, 0) & (seg - 1)
    s = 1
    while s < seg:
        x = x + jnp.where(pos >= s, pltpu.roll(x, s, axis=0), 0.0)
        s *= 2
    return x


def _segment_pick(x, seg, which):
    rows, width = x.shape
    picked = x.reshape(rows // seg, seg, width)[:, which:which + 1, :]
    return jnp.broadcast_to(picked, (rows // seg, seg, width)).reshape(rows, width)


def _segment_last(x, seg):
    return _segment_pick(x, seg, seg - 1)


def _rms(x, g):
    return x * lax.rsqrt(jnp.mean(x * x, axis=-1, keepdims=True) + EPS) * g


def _silu(x):
    return x * jax.nn.sigmoid(x)


def _softplus(x):
    return jnp.maximum(x, 0.0) + jnp.log1p(jnp.exp(-jnp.abs(x)))


def _const_spec(shape):
    zeros = (0,) * len(shape)
    return pl.BlockSpec(shape, lambda *_: zeros, pipeline_mode=pl.Buffered(1))


def _mod_spec(col, tiles_per_seq):
    if tiles_per_seq is None:
        return pl.BlockSpec((None, 1, D_MODEL), lambda i: (0, 0, col))
    return pl.BlockSpec((None, 1, D_MODEL), lambda i: (i // tiles_per_seq, 0, col))


def _params(n_grid):
    return pltpu.CompilerParams(dimension_semantics=("arbitrary",) * n_grid, vmem_limit_bytes=VMEM_LIMIT)


def _ada_kernel(c_ref, w_ref, b_ref, o_ref):
    o_ref[...] = _dot(_silu(c_ref[...]), w_ref[...]) + b_ref[...]


def _ada_mod(cond, w_ada, b_ada):
    depth, _, n = w_ada.shape
    rows = cond.shape[0]
    tn = 1536
    return pl.pallas_call(
        _ada_kernel,
        grid=(depth, n // tn),
        in_specs=[pl.BlockSpec((rows, D_MODEL), lambda l, j: (0, 0)),
                  pl.BlockSpec((None, D_MODEL, tn), lambda l, j: (l, 0, j)),
                  pl.BlockSpec((None, 1, tn), lambda l, j: (l, 0, j))],
        out_specs=pl.BlockSpec((None, rows, tn), lambda l, j: (l, 0, j)),
        out_shape=jax.ShapeDtypeStruct((depth, rows, n), F32),
        compiler_params=_params(2),
        name="ada_mod",
    )(cond, w_ada, b_ada.reshape(depth, 1, n))


def _modulated(x_ref, g_ref, sh_ref, sc_ref, part_rows):
    parts = []
    for r in range(x_ref.shape[0] // part_rows):
        x = x_ref[r * part_rows:(r + 1) * part_rows, :]
        parts.append((_rms(x, g_ref[0:1, :]) * (1.0 + sc_ref[...]) + sh_ref[...]).astype(BF16))
    return parts


def _gdn_inproj_kernel(x_ref, g_ref, sh_ref, sc_ref, w_ref, wg_ref, alog_ref, dt_ref, cw_ref, proj_ref, gate_ref,
                       grow_ref, *, seg):
    n_rows = x_ref.shape[0]
    p_rows = PRE_ROWS
    p_cols = EPI_COLS
    hb = _modulated(x_ref, g_ref, sh_ref, sc_ref, p_rows)
    period = min(seg, p_rows)
    pos = lax.broadcasted_iota(jnp.int32, (period, HEAD_DIM), 0) & (seg - 1)

    def tap_weights(w):
        taps = {}
        for s in (-2, -1, 1, 2):
            valid = (pos + s >= 0) & (pos + s < seg)
            taps[s] = jnp.tile(jnp.where(valid, w[s + 2:s + 3, :], 0.0), (p_rows // period, 1))
        return taps

    def unit(x, scale):
        return x * (lax.rsqrt(jnp.sum(x * x, axis=-1, keepdims=True) + EPS) * scale)

    pieces = [(j, r) for j in range(GDN_MAIN // p_cols) for r in range(n_rows // p_rows)]

    def product(piece):
        j, r = piece
        return jnp.dot(hb[r], w_ref[:, j * p_cols:(j + 1) * p_cols], preferred_element_type=F32)

    def finish(piece, y):
        j, r = piece
        rows = slice(r * p_rows, (r + 1) * p_rows)
        kind = j * p_cols // KEY_DIM
        if kind == 3:
            proj_ref[rows, j * p_cols:(j + 1) * p_cols] = _silu(y)
            return
        for hh in range(p_cols // HEAD_DIM):
            c0 = j * p_cols + hh * HEAD_DIM
            w = cw_ref[:, c0:c0 + HEAD_DIM]
            taps = tap_weights(w)
            x = y[:, hh * HEAD_DIM:(hh + 1) * HEAD_DIM]
            acc = x * w[CONV_W // 2:CONV_W // 2 + 1, :]
            for s in (-2, -1, 1, 2):
                acc = acc + pltpu.roll(x, (-s) % p_rows, axis=0) * taps[s]
            out = _silu(acc)
            if kind == 0:
                out = unit(out, HEAD_DIM ** -0.5)
            elif kind == 1:
                out = unit(out, 1.0)
            proj_ref[rows, c0:c0 + HEAD_DIM] = out

    y_next = product(pieces[0])
    for n, piece in enumerate(pieces):
        y, y_next = y_next, (product(pieces[n + 1]) if n + 1 < len(pieces) else None)
        finish(piece, y)
    raw = jnp.dot(jnp.concatenate(hb, axis=0), wg_ref[...], preferred_element_type=F32)
    beta = jax.nn.sigmoid(raw)
    g = -jnp.exp(alog_ref[...]) * _softplus(raw + dt_ref[...])
    gp, gs = _segment_sums(g, GDN_CHUNK)
    lane = lax.broadcasted_iota(jnp.int32, raw.shape, 1)
    gates = jnp.where(lane < 2 * N_HEADS, beta,
                      jnp.where(lane < 3 * N_HEADS, gp, jnp.where(lane < 4 * N_HEADS, gs, 0.0)))
    gate_ref[...] = gates

    sums_t = gates.T[2 * N_HEADS:4 * N_HEADS, :]
    for m in range(n_rows // GATE_LANES):
        grow_ref[m] = sums_t[:, m * GATE_LANES:(m + 1) * GATE_LANES]


def _gdn_inproj(x2d, norm_g, mod, tiles_per_seq, w_main, w_gate, alog_row, dt_row, conv_w, seg):
    rows = x2d.shape[0]
    assert PRE_ROWS % seg == 0 and ROW_TILE % PRE_ROWS == 0
    return pl.pallas_call(
        functools.partial(_gdn_inproj_kernel, seg=seg),
        grid=(rows // ROW_TILE,),
        in_specs=[pl.BlockSpec((ROW_TILE, D_MODEL), lambda i: (i, 0)),
                  _const_spec((4, D_MODEL)),
                  _mod_spec(0, tiles_per_seq), _mod_spec(1, tiles_per_seq),
                  _const_spec((D_MODEL, GDN_MAIN)), _const_spec((D_MODEL, GATE_LANES)),
                  _const_spec((1, GATE_LANES)), _const_spec((1, GATE_LANES)),
                  _const_spec((CONV_W, 3 * KEY_DIM))],
        out_specs=[pl.BlockSpec((ROW_TILE, GDN_MAIN), lambda i: (i, 0)),
                   pl.BlockSpec((ROW_TILE, GATE_LANES), lambda i: (i, 0)),
                   pl.BlockSpec((ROW_TILE // GATE_LANES, 2 * N_HEADS, GATE_LANES), lambda i: (i, 0, 0))],
        out_shape=[jax.ShapeDtypeStruct((rows, GDN_MAIN), F32),
                   jax.ShapeDtypeStruct((rows, GATE_LANES), F32),
                   jax.ShapeDtypeStruct((rows // GATE_LANES, 2 * N_HEADS, GATE_LANES), F32)],
        compiler_params=_params(1),
        name="gdn_inproj",
    )(x2d, norm_g, mod, mod, w_main, w_gate, alog_row, dt_row, conv_w)


def _hgrn_inproj_kernel(x_ref, g_ref, sh_ref, sc_ref, w_ref, lbl_ref, out_ref, *, layer):
    n_rows = x_ref.shape[0]
    hb = _modulated(x_ref, g_ref, sh_ref, sc_ref, EPI_ROWS)
    c = HGRN_CHUNK
    depth = lbl_ref.shape[0]
    logits = [lbl_ref[l] for l in range(depth)]
    top = functools.reduce(jnp.maximum, logits)
    ex = [jnp.exp(l - top) for l in logits]
    denom = functools.reduce(lambda x, y: x + y, ex)
    lb = functools.reduce(lambda x, y: x + y, ex[:layer + 1]) / denom - ex[0] / denom

    pieces = [(j, r) for j in range(HGRN_PROJ // EPI_COLS) for r in range(n_rows // EPI_ROWS)]

    def product(piece):
        j, r = piece
        return jnp.dot(hb[r], w_ref[:, j * EPI_COLS:(j + 1) * EPI_COLS], preferred_element_type=F32)

    def finish(piece, y):
        j, r = piece
        rows = slice(r * EPI_ROWS, (r + 1) * EPI_ROWS)
        kind, off = divmod(j * EPI_COLS, KEY_DIM)
        if kind in (0, 3, 4):
            dst = {0: 0, 3: 5, 4: 6}[kind] * KEY_DIM + off
            out_ref[rows, dst:dst + EPI_COLS] = y if kind == 3 else _silu(y)
            return
        d = kind - 1
        for hh in range(EPI_COLS // HEAD_DIM):
            c0 = off + hh * HEAD_DIM
            lbd = lb[d:d + 1, c0:c0 + HEAD_DIM]
            fg = lbd + (1.0 - lbd) * jax.nn.sigmoid(y[:, hh * HEAD_DIM:(hh + 1) * HEAD_DIM])
            logf = jnp.log(fg)
            run = _segment_prefix(logf, c)
            if d == 1:
                run = _segment_last(run, c) - run + logf
            out_ref[rows, (1 + 2 * d) * KEY_DIM + c0:(1 + 2 * d) * KEY_DIM + c0 + HEAD_DIM] = 1.0 - fg
            out_ref[rows, (2 + 2 * d) * KEY_DIM + c0:(2 + 2 * d) * KEY_DIM + c0 + HEAD_DIM] = jnp.exp(run)

    y_next = product(pieces[0])
    for n, piece in enumerate(pieces):
        y, y_next = y_next, (product(pieces[n + 1]) if n + 1 < len(pieces) else None)
        finish(piece, y)


def _hgrn_inproj(x2d, norm_g, mod, tiles_per_seq, w_in, lb_logits, layer):
    rows = x2d.shape[0]
    depth = lb_logits.shape[0]
    return pl.pallas_call(
        functools.partial(_hgrn_inproj_kernel, layer=layer),
        grid=(rows // ROW_TILE,),
        in_specs=[pl.BlockSpec((ROW_TILE, D_MODEL), lambda i: (i, 0)),
                  _const_spec((4, D_MODEL)),
                  _mod_spec(0, tiles_per_seq), _mod_spec(1, tiles_per_seq),
                  _const_spec((D_MODEL, HGRN_PROJ)), _const_spec((depth, 2, KEY_DIM))],
        out_specs=pl.BlockSpec((ROW_TILE, HGRN_OUT), lambda i: (i, 0)),
        out_shape=jax.ShapeDtypeStruct((rows, HGRN_OUT), F32),
        compiler_params=_params(1),
        name="hgrn_inproj",
    )(x2d, norm_g, mod, mod, w_in, lb_logits)


def _split_bf16(x):
    hi = lax.bitcast_convert_type(lax.bitcast_convert_type(x, jnp.int32) & jnp.int32(-65536), F32)
    return hi, x - hi


def _pair_inverses(lows, left, diag2, fill):
    c = lows[0].shape[0]

    def block_diag(x):
        return jnp.concatenate([jnp.where(left, x, 0.0), jnp.where(left, 0.0, x)], axis=0)

    def left_operand(hi, lo):
        return jnp.concatenate([hi.astype(BF16), lo.astype(BF16)] * 2, axis=1)

    def right_operand(hi, lo):
        bh, bl = block_diag(hi).astype(BF16), block_diag(lo).astype(BF16)
        return jnp.concatenate([bh, bh, bl, bl], axis=0)

    accs = [jnp.where(diag2, 1.0, 0.0) - x for x in lows]
    parts = [_split_bf16(x) for x in lows]
    powers = [jnp.dot(left_operand(hi, lo), right_operand(hi, lo), preferred_element_type=F32) for hi, lo in parts]
    fill(0)
    levels = c.bit_length() - 2
    for level in range(levels):
        parts = [_split_bf16(x) for x in powers]
        rhs = [right_operand(hi, lo) for hi, lo in parts]
        acc_lhs = [left_operand(*_split_bf16(a)) for a in accs]
        if level + 1 < levels:
            res = [jnp.dot(jnp.concatenate([left_operand(hi, lo), al], axis=0), r, preferred_element_type=F32)
                   for (hi, lo), al, r in zip(parts, acc_lhs, rhs)]
            fill(level + 1)
            powers = [x[:c] for x in res]
            accs = [a + x[c:] for a, x in zip(accs, res)]
        else:
            accs = [a + jnp.dot(al, r, preferred_element_type=F32) for a, al, r in zip(accs, acc_lhs, rhs)]
    return accs


def _gdn_kernel(qn, kn, vn, gcol_ref, grow_ref, *rest, has_init, emit_state, group, seq):
    rest = list(rest)
    s0_ref = rest.pop(0) if has_init else None
    o_ref = rest.pop(0)
    sfin_ref = rest.pop(0) if emit_state else None
    ob, m_s, b_s, q_s, gl_s, st = rest
    t_len = qn.shape[0]
    c = GDN_CHUNK
    n_chunks = t_len // c
    n_heads = qn.shape[1] // HEAD_DIM
    heads = n_heads // seq
    h0 = pl.program_id(1) * n_heads

    lane = lax.broadcasted_iota(jnp.int32, (c, 2 * c), 1)
    row = lax.broadcasted_iota(jnp.int32, (c, 2 * c), 0)
    left = lane < c
    ahead = jnp.where(left, row - lane, lane - c - row)
    left_row = lax.broadcasted_iota(jnp.int32, (1, 2 * c), 1) < c

    def column(tile, idx):
        return jnp.sum(jnp.where(lane == idx, tile, 0.0), axis=-1, keepdims=True)

    def direction_blocks(a0, a1):
        z0, z1 = jnp.zeros(a0.shape, BF16), jnp.zeros(a1.shape, BF16)
        return jnp.concatenate([jnp.concatenate([a0.astype(BF16), z1], axis=1),
                                jnp.concatenate([z0, a1.astype(BF16)], axis=1)], axis=0)

    def load(hh, n):
        h = h0 + hh
        cidx = [n, n_chunks - 1 - n]
        rows = [pl.ds(pl.multiple_of(ci * c, c), c) for ci in cidx]
        lanes = (slice(hh * HEAD_DIM, (hh + 1) * HEAD_DIM) if isinstance(hh, int)
                 else pl.ds(pl.multiple_of(hh * HEAD_DIM, HEAD_DIM), HEAD_DIM))
        e = dict(hh=hh, n=n, rows=rows, q=[qn[r, lanes] for r in rows], k=[kn[r, lanes] for r in rows],
                 v=[vn[r, lanes] for r in rows])
        gtile = [gcol_ref[r, :] for r in rows]
        e["beta"] = [column(gtile[d], d * N_HEADS + h) for d in range(2)]
        e["gc"] = [column(gtile[d], (2 + d) * N_HEADS + h) for d in range(2)]
        e["g_last"] = [e["gc"][0][c - 1:c, :], e["gc"][1][0:1, :]]
        def window(d, ci):
            w = grow_ref[ci // 2, pl.ds(d * N_HEADS + h, 1), :]
            return jnp.where(ci % 2 == d, w, pltpu.roll(w, c, axis=1))
        gr2 = jnp.where(left_row, window(0, cidx[0]), window(1, cidx[1]))
        e["decay2"] = jnp.exp(jnp.where(ahead >= 0, jnp.where(left, e["gc"][0], e["gc"][1]) - gr2, -jnp.inf))
        return e

    def gram(e):
        k, q = e["k"], e["q"]
        kq = _dot(jnp.concatenate([jnp.concatenate(k, axis=1), jnp.concatenate(q, axis=1)], axis=0),
                  direction_blocks(k[0], k[1]), _NT)
        e["low2"] = jnp.where(ahead > 0, kq[:c] * jnp.where(left, e["beta"][0], e["beta"][1]) * e["decay2"], 0.0)
        e["qk2"] = kq[c:] * e["decay2"]

    def solve(e, t2):
        k, v, beta, gc = e["k"], e["v"], e["beta"], e["gc"]
        e["eg"] = [jnp.exp(gc[d]) for d in range(2)]
        rhs = [jnp.concatenate([v[d] * beta[d], k[d] * beta[d] * e["eg"][d]], axis=1) for d in range(2)]
        e["uw"] = jnp.dot(t2.astype(BF16), direction_blocks(*rhs), preferred_element_type=F32)

    def fold(e):
        uw = e["uw"]
        r2 = direction_blocks(uw[:, :2 * HEAD_DIM], uw[:, 2 * HEAD_DIM:])
        e["oq"] = jnp.dot(e["qk2"].astype(BF16), r2, preferred_element_type=F32)
        kt = jnp.concatenate([e["k"][d] * jnp.exp(e["g_last"][d] - e["gc"][d]) for d in range(2)], axis=0)
        e["bm"] = _dot(kt, r2, _TN)

    def store(e):
        hh, n, oq, bm = e["hh"], e["n"], e["oq"], e["bm"]
        for d in range(2):
            base = 2 * d * HEAD_DIM
            ob[hh, d, e["rows"][d], :] = oq[:, base:base + HEAD_DIM]
            q_s[hh, n, d] = (e["q"][d] * e["eg"][d] - oq[:, base + HEAD_DIM:base + 2 * HEAD_DIM]).astype(BF16)
            b_s[hh, n, d] = bm[:, base:base + HEAD_DIM]
            m_s[hh, n, d] = bm[:, base + HEAD_DIM:base + 2 * HEAD_DIM].astype(BF16)
            gl_s[hh, n, d] = jnp.broadcast_to(jnp.exp(e["g_last"][d]), (1, HEAD_DIM))

    def scan_step(base, n):
        for hh in (base + j for j in range(heads)):
            for d in range(2):
                rows = pl.ds(pl.multiple_of((n if d == 0 else n_chunks - 1 - n) * c, c), c)
                state = st[hh, d]
                ms = jnp.dot(jnp.concatenate([m_s[hh, n, d], q_s[hh, n, d]], axis=0), state.astype(BF16),
                             preferred_element_type=F32)
                ob[hh, d, rows, :] = ob[hh, d, rows, :] + ms[HEAD_DIM:]
                st[hh, d] = state * gl_s[hh, n, d] - ms[:HEAD_DIM] + b_s[hh, n, d]

    def prepare(base, first, behind):
        pending = [] if behind is None else [behind[1] + j for j in range(group)]
        n_slots = 8

        def fill(slot):
            for j in range(slot * len(pending) // n_slots, (slot + 1) * len(pending) // n_slots):
                scan_step(behind[0], pending[j])

        chunks = [load(base + hh, first + j) for j in range(group) for hh in range(heads)]
        for e in chunks:
            gram(e)
        fill(0)
        inverses = _pair_inverses([e["low2"] for e in chunks], left, ahead == 0, lambda level: fill(1 + level))
        for e, t2 in zip(chunks, inverses):
            solve(e, t2)
        fill(6)
        for e in chunks:
            fold(e)
        fill(7)
        for e in chunks:
            store(e)

    for hh in range(n_heads):
        for d in range(2):
            st[hh, d] = s0_ref[d, hh] if has_init else jnp.zeros((HEAD_DIM, HEAD_DIM), F32)

    n_steps = n_chunks // group
    prepare(0, 0, None)

    def step(u, carry):
        unit = lambda v: ((v // n_steps) * heads, (v % n_steps) * group) if seq > 1 else (0, v * group)
        prepare(*unit(u), unit(u - 1))
        return carry

    lax.fori_loop(1, seq * n_steps, step, 0)
    for j in range(group):
        scan_step((seq - 1) * heads, (n_steps - 1) * group + j)

    def post(i, carry):
        rows = pl.ds(pl.multiple_of(i * PRE_ROWS, PRE_ROWS), PRE_ROWS)
        for hh in range(n_heads):
            o_ref[rows, hh * HEAD_DIM:(hh + 1) * HEAD_DIM] = ob[hh, 0, rows, :] + ob[hh, 1, rows, :]
        return carry

    lax.fori_loop(0, t_len // PRE_ROWS, post, 0)
    if emit_state:
        for hh in range(n_heads):
            for d in range(2):
                sfin_ref[d, hh] = st[hh, d]


def _gdn_mixer(proj, gates, grow, s0, layer_idx, emit_state):
    b, t, _ = proj.shape
    c = GDN_CHUNK
    n_chunks = t // c
    assert t % PRE_ROWS == 0 and 2 * c == HEAD_DIM
    has_init = s0 is not None
    group = min(GDN_GROUP, n_chunks)
    lock = max(1, GDN_GROUP // n_chunks)
    seq = GDN_SEQ
    heads = lock * seq
    width = heads * HEAD_DIM
    col = lambda off: pl.BlockSpec((None, t, width), lambda i, h: (i, 0, off // heads + h))
    in_specs = [col(0), col(N_HEADS), col(2 * N_HEADS),
                pl.BlockSpec((None, t, GATE_LANES), lambda i, h: (i, 0, 0)),
                pl.BlockSpec((t // GATE_LANES, 2 * N_HEADS, GATE_LANES), lambda i, h: (i, 0, 0))]
    args = [proj, proj, proj, gates, grow]
    if has_init:
        in_specs.append(pl.BlockSpec((None, None, 2, heads, HEAD_DIM, HEAD_DIM),
                                     lambda i, h: (i, layer_idx, 0, h, 0, 0)))
        args.append(s0)
    out_specs = [pl.BlockSpec((None, t, width), lambda i, h: (i, 0, h))]
    out_shape = [jax.ShapeDtypeStruct((b, t, KEY_DIM), F32)]
    if emit_state:
        out_specs.append(pl.BlockSpec((None, 2, heads, HEAD_DIM, HEAD_DIM), lambda i, h: (i, 0, h, 0, 0)))
        out_shape.append(jax.ShapeDtypeStruct((b, 2, N_HEADS, HEAD_DIM, HEAD_DIM), F32))
    scratch = [
        pltpu.VMEM((heads, 2, t, HEAD_DIM), F32),
        pltpu.VMEM((heads, n_chunks, 2, HEAD_DIM, HEAD_DIM), BF16),
        pltpu.VMEM((heads, n_chunks, 2, HEAD_DIM, HEAD_DIM), F32),
        pltpu.VMEM((heads, n_chunks, 2, c, HEAD_DIM), BF16),
        pltpu.VMEM((heads, n_chunks, 2, 1, HEAD_DIM), F32),
        pltpu.VMEM((heads, 2, HEAD_DIM, HEAD_DIM), F32)]
    res = pl.pallas_call(
        functools.partial(_gdn_kernel, has_init=has_init, emit_state=emit_state, group=group, seq=seq),
        grid=(b, N_HEADS // heads),
        in_specs=in_specs,
        out_specs=out_specs,
        out_shape=out_shape,
        scratch_shapes=scratch,
        compiler_params=_params(2),
        name="gdn_mixer",
    )(*args)
    return res[0], (res[1] if emit_state else None)


def _hgrn_kernel(q_ref, kf_ref, ef_ref, kb_ref, eb_ref, i_ref, *rest, has_init, emit_state):
    rest = list(rest)
    s0_ref = rest.pop(0) if has_init else None
    o_ref = rest.pop(0)
    sfin_ref = rest.pop(0) if emit_state else None
    qin, kout, ktail, ftot, ob, st = rest
    t_len = q_ref.shape[0]
    c = HGRN_CHUNK

    def pre(d, rows, q):
        k = (kf_ref, kb_ref)[d][rows, :]
        e = (ef_ref, eb_ref)[d][rows, :]
        whole = _segment_pick(e, c, c - 1 if d == 0 else 0)
        k_over_e = k / e
        qin[d, rows, :] = q * e
        kout[d, rows, :] = k_over_e.astype(BF16)
        ktail[d, rows, :] = k_over_e * whole
        ftot[d, rows, :] = whole

    def pre_body(i, carry):
        rows = pl.ds(pl.multiple_of(i * HGRN_BLOCK, HGRN_BLOCK), HGRN_BLOCK)
        q = q_ref[rows, :]
        pre(0, rows, q)
        pre(1, rows, q)
        return carry

    lax.fori_loop(0, t_len // HGRN_BLOCK, pre_body, 0)

    for d in range(2):
        st[d] = s0_ref[d].T if has_init else jnp.zeros((HEAD_DIM, HEAD_DIM), F32)

    grp = HGRN_GROUP
    per = grp // c
    assert per == 4
    blk = min(HGRN_SCAN_ROWS, t_len)
    ri = lax.broadcasted_iota(jnp.int32, (grp, grp), 0)
    ci = lax.broadcasted_iota(jnp.int32, (grp, grp), 1)
    row_chunk = lax.broadcasted_iota(jnp.int32, (grp, HEAD_DIM), 0) // c

    def group_factors(f, d):
        r1, r2, r3 = (pltpu.roll(f, s * c, axis=0) for s in (1, 2, 3))
        prev, nxt = ((r1, r2, r3), (r3, r2, r1)) if d == 0 else ((r3, r2, r1), (r1, r2, r3))
        order = row_chunk if d == 0 else per - 1 - row_chunk
        g1, h1 = prev[0], nxt[0]
        g2, h2 = g1 * prev[1], h1 * nxt[1]
        g3, h3 = g2 * prev[2], h2 * nxt[2]
        before = jnp.where(order == 0, 1.0, jnp.where(order == 1, g1, jnp.where(order == 2, g2, g3)))
        after = jnp.where(order == 3, 1.0, jnp.where(order == 2, h1, jnp.where(order == 1, h2, h3)))
        whole = f[0:1] * f[c:c + 1] * f[2 * c:2 * c + 1] * f[3 * c:3 * c + 1]
        return g1, g2, before, after, whole

    def att_select(d, p1, p234):
        dist = (ri // c - ci // c) if d == 0 else (ci // c - ri // c)
        inside = (ci <= ri) if d == 0 else (ci >= ri)
        return jnp.where((dist == 0) & inside, p1,
                         jnp.where(dist == 1, p234[:grp],
                                   jnp.where(dist == 2, p234[grp:2 * grp],
                                             jnp.where(dist == 3, p234[2 * grp:], 0.0))))

    def body(i, carry):
        ctx = []
        for step in range(blk // grp):
            for d in range(2):
                g_idx = i * (blk // grp) + step
                r0 = pl.multiple_of((g_idx if d == 0 else t_len // grp - 1 - g_idx) * grp, grp)
                rows = pl.ds(r0, grp)
                e = dict(d=d, rows=rows, qi=qin[d, rows, :], ko=kout[d, rows, :], kt=ktail[d, rows, :],
                         v=i_ref[rows, :].astype(BF16))
                e["g1"], e["g2"], e["before"], e["after"], e["whole"] = group_factors(ftot[d, rows, :], d)
                ctx.append(e)
        for e in ctx:
            qi = e["qi"]
            e["p1"] = _dot(qi, e["ko"], _NT)
            e["p234"] = _dot(jnp.concatenate([qi, qi * e["g1"], qi * e["g2"]], axis=0), e["kt"], _NT)
        for e in ctx:
            e["ds"] = _dot(e["v"], e["kt"] * e["after"], _TN)
        for e in ctx:
            e["intra"] = _dot(att_select(e["d"], e["p1"], e["p234"]), e["v"])
        states = [st[0], st[1]]
        for e in ctx:
            d = e["d"]
            ob[d, e["rows"], :] = e["intra"] + _dot(e["qi"] * e["before"], states[d], _NT)
            states[d] = states[d] * e["whole"] + e["ds"]
        st[0], st[1] = states
        return carry

    lax.fori_loop(0, t_len // blk, body, 0)

    def post(i, carry):
        rows = pl.ds(pl.multiple_of(i * PRE_ROWS, PRE_ROWS), PRE_ROWS)
        o_ref[rows, :] = ob[0, rows, :] + ob[1, rows, :]
        return carry

    lax.fori_loop(0, t_len // PRE_ROWS, post, 0)
    if emit_state:
        for d in range(2):
            sfin_ref[d] = st[d].T


def _hgrn_mixer(proj, s0, layer_idx, emit_state):
    b, t, _ = proj.shape
    has_init = s0 is not None
    col = lambda off: pl.BlockSpec((None, t, HEAD_DIM), lambda i, h: (i, 0, off + h))
    in_specs = [col(j * N_HEADS) for j in range(6)]
    args = [proj] * 6
    if has_init:
        in_specs.append(pl.BlockSpec((None, None, 2, None, HEAD_DIM, HEAD_DIM),
                                     lambda i, h: (i, layer_idx, 0, h, 0, 0)))
        args.append(s0)
    out_specs = [pl.BlockSpec((None, t, HEAD_DIM), lambda i, h: (i, 0, h))]
    out_shape = [jax.ShapeDtypeStruct((b, t, KEY_DIM), F32)]
    if emit_state:
        out_specs.append(pl.BlockSpec((None, 2, None, HEAD_DIM, HEAD_DIM), lambda i, h: (i, 0, h, 0, 0)))
        out_shape.append(jax.ShapeDtypeStruct((b, 2, N_HEADS, HEAD_DIM, HEAD_DIM), F32))
    res = pl.pallas_call(
        functools.partial(_hgrn_kernel, has_init=has_init, emit_state=emit_state),
        grid=(b, N_HEADS),
        in_specs=in_specs,
        out_specs=out_specs,
        out_shape=out_shape,
        scratch_shapes=[pltpu.VMEM((2, t, HEAD_DIM), F32), pltpu.VMEM((2, t, HEAD_DIM), BF16),
                        pltpu.VMEM((2, t, HEAD_DIM), F32), pltpu.VMEM((2, t, HEAD_DIM), F32),
                        pltpu.VMEM((2, t, HEAD_DIM), F32), pltpu.VMEM((2, HEAD_DIM, HEAD_DIM), F32)],
        compiler_params=_params(2),
        name="hgrn_mixer",
    )(*args)
    return res[0], (res[1] if emit_state else None)


def _post_kernel(x_ref, o_ref, z_ref, on_ref, g_ref, gt1_ref, sh2_ref, sc2_ref, gt2_ref, wo_ref, w1_ref, w2_ref,
                 y_ref):
    rows = x_ref.shape[0]
    n_parts = 2
    part = rows // n_parts
    mixes = []
    for r in range(n_parts):
        rs = slice(r * part, (r + 1) * part)
        gated = [(_rms(o_ref[rs, h * HEAD_DIM:(h + 1) * HEAD_DIM], on_ref[...])
                  * z_ref[rs, h * HEAD_DIM:(h + 1) * HEAD_DIM]).astype(BF16) for h in range(N_HEADS)]
        mixes.append(jnp.dot(jnp.concatenate(gated, axis=1), wo_ref[...], preferred_element_type=F32))
    mix = jnp.concatenate(mixes, axis=0)
    x1 = x_ref[...] + gt1_ref[...] * _rms(mix, g_ref[1:2, :])
    hb = (_rms(x1, g_ref[2:3, :]) * (1.0 + sc2_ref[...]) + sh2_ref[...]).astype(BF16)
    ff = jnp.zeros(x1.shape, F32)
    for j in range(D_FF // 1024):
        cols = slice(j * 1024, (j + 1) * 1024)
        hid = jnp.maximum(jnp.dot(hb, w1_ref[:, cols], preferred_element_type=F32), 0.0)
        ff = ff + jnp.dot((hid * hid).astype(BF16), w2_ref[cols, :], preferred_element_type=F32)
    y_ref[...] = x1 + gt2_ref[...] * _rms(ff, g_ref[3:4, :])


def _post_mixer(x2d, o2d, proj2d, z_block, onorm_g, norm_g, mod, tiles_per_seq, w_out, w1, w2):
    rows = x2d.shape[0]
    tile = pl.BlockSpec((ROW_TILE, D_MODEL), lambda i: (i, 0))
    return pl.pallas_call(
        _post_kernel,
        grid=(rows // ROW_TILE,),
        in_specs=[tile, tile, pl.BlockSpec((ROW_TILE, D_MODEL), lambda i: (i, z_block)),
                  _const_spec((1, HEAD_DIM)), _const_spec((4, D_MODEL)),
                  _mod_spec(2, tiles_per_seq), _mod_spec(3, tiles_per_seq),
                  _mod_spec(4, tiles_per_seq), _mod_spec(5, tiles_per_seq),
                  _const_spec((D_MODEL, D_MODEL)), _const_spec((D_MODEL, D_FF)), _const_spec((D_FF, D_MODEL))],
        out_specs=tile,
        out_shape=jax.ShapeDtypeStruct((rows, D_MODEL), F32),
        compiler_params=_params(1),
        name="post_mixer",
    )(x2d, o2d, proj2d, onorm_g.reshape(1, HEAD_DIM), norm_g, mod, mod, mod, mod, w_out, w1, w2)


def _trunk(x, mod_rows, per_seq_mod, s_gdn, s_hgrn, grid_conv, emit_state, weights):
    (norm_g, gdn_main, gdn_gate, gdn_alog, gdn_dt, gdn_conv_w, gdn_onorm_g, gdn_w_out,
     hgrn_w_in, hgrn_lb_logits, hgrn_onorm_g, hgrn_w_out, mlp_w1, mlp_w2) = weights
    b, t, _ = x.shape
    tiles_per_seq = t // ROW_TILE if per_seq_mod else None
    x2d = x.reshape(b * t, D_MODEL)
    depth = norm_g.shape[0]
    fin_gdn, fin_hgrn = [], []
    for layer in range(depth):
        j = layer // 2
        mod = mod_rows[layer]
        if layer % 2 == 0:
            proj, gates, grow = _gdn_inproj(x2d, norm_g[layer], mod, tiles_per_seq, gdn_main[j], gdn_gate[j],
                                            gdn_alog[j], gdn_dt[j], gdn_conv_w[j], GRID_W if grid_conv else t)
            o, fin = _gdn_mixer(proj.reshape(b, t, GDN_MAIN), gates.reshape(b, t, GATE_LANES), grow, s_gdn, j,
                                emit_state)
            fin_gdn.append(fin)
            z_block, onorm_g, w_out = 3, gdn_onorm_g[j], gdn_w_out[j]
        else:
            proj = _hgrn_inproj(x2d, norm_g[layer], mod, tiles_per_seq, hgrn_w_in[j], hgrn_lb_logits, layer)
            o, fin = _hgrn_mixer(proj.reshape(b, t, HGRN_OUT), s_hgrn, j, emit_state)
            fin_hgrn.append(fin)
            z_block, onorm_g, w_out = 6, hgrn_onorm_g[j], hgrn_w_out[j]
        x2d = _post_mixer(x2d, o.reshape(b * t, KEY_DIM), proj, z_block, onorm_g, norm_g[layer], mod, tiles_per_seq,
                          w_out, mlp_w1[layer], mlp_w2[layer])
    y = x2d.reshape(b, t, D_MODEL)
    if emit_state:
        return y, jnp.stack(fin_gdn, axis=1), jnp.stack(fin_hgrn, axis=1)
    return y, None, None


def kernel(x_prompt, x_sample, state_gdn, state_hgrn, c, c_ctx, w_ada, b_ada, norm_g, gdn_w_in, gdn_conv_w,
           gdn_a_log, gdn_dt_bias, gdn_onorm_g, gdn_w_out, hgrn_w_in, hgrn_lb_logits, hgrn_onorm_g, hgrn_w_out,
           mlp_w1, mlp_w2):
    n_dec = c.shape[0]
    n_rows = 16
    cond = jnp.concatenate([c_ctx[None, :], c, jnp.zeros((n_rows - 1 - n_dec, D_MODEL), F32)], axis=0)
    mod = _ada_mod(cond, w_ada, b_ada)
    mod_ctx = mod[:, 0:1, None, :]
    mod_smp = mod[:, 1:1 + n_dec, None, :]

    n_gdn = gdn_w_in.shape[0]
    gate_pad = GATE_LANES - 4 * N_HEADS
    gdn_gate = jnp.pad(gdn_w_in[:, :, GDN_MAIN:], ((0, 0), (0, 0), (0, gate_pad))).astype(BF16)
    lead = jnp.zeros((n_gdn, 2 * N_HEADS), F32)
    tail = jnp.zeros((n_gdn, gate_pad), F32)
    gdn_alog = jnp.concatenate([lead, gdn_a_log.reshape(n_gdn, 2 * N_HEADS), tail], axis=1)[:, None, :]
    gdn_dt = jnp.concatenate([lead, gdn_dt_bias.reshape(n_gdn, 2 * N_HEADS), tail], axis=1)[:, None, :]
    weights = (norm_g, gdn_w_in[:, :, :GDN_MAIN].astype(BF16), gdn_gate, gdn_alog, gdn_dt, gdn_conv_w,
               gdn_onorm_g, gdn_w_out.astype(BF16), hgrn_w_in.astype(BF16), hgrn_lb_logits, hgrn_onorm_g,
               hgrn_w_out.astype(BF16), mlp_w1.astype(BF16), mlp_w2.astype(BF16))

    y_prompt, new_gdn, new_hgrn = _trunk(x_prompt, mod_ctx, False, None, None, False, True, weights)
    y_sample, _, _ = _trunk(x_sample, mod_smp, True, state_gdn, state_hgrn, True, False, weights)
    return (y_prompt, y_sample, new_gdn, new_hgrn)
```

```python
import functools

import jax
import jax.numpy as jnp
from jax import lax
from jax.experimental import pallas as pl
from jax.experimental.pallas import tpu as pltpu

D_MODEL = 1024
N_HEADS = 8
HEAD_DIM = 128
KEY_DIM = N_HEADS * HEAD_DIM
CONV_W = 5
GDN_CHUNK = 64
HGRN_CHUNK = 16
GRID_W = 64
D_FF = 4 * D_MODEL
EPS = 1e-6
GDN_MAIN = 4 * KEY_DIM
GATE_LANES = 128
HGRN_PROJ = 5 * KEY_DIM
HGRN_OUT = 7 * KEY_DIM

ROW_TILE = 512
EPI_ROWS = 128
EPI_COLS = 256
PRE_ROWS = 256
HGRN_BLOCK = 256
HGRN_SCAN_ROWS = 1024
HGRN_GROUP = 64
GDN_SEQ = 2
GDN_GROUP = 8
VMEM_LIMIT = 56 * 1024 * 1024

BF16 = jnp.bfloat16
F32 = jnp.float32

_NT = (((1,), (1,)), ((), ()))
_TN = (((0,), (0,)), ((), ()))


def _dot(a, b, dims=None):
    a = a.astype(BF16)
    b = b.astype(BF16)
    if dims is None:
        return jnp.dot(a, b, preferred_element_type=F32)
    return lax.dot_general(a, b, dims, preferred_element_type=F32)


def _segment_sums(x, seg):
    rows = x.shape[0]
    pos = lax.broadcasted_iota(jnp.int32, x.shape, 0) & (seg - 1)
    pre, suf = x, x
    s = 1
    while s < seg:
        pre = pre + jnp.where(pos >= s, pltpu.roll(pre, s, axis=0), 0.0)
        suf = suf + jnp.where(pos + s < seg, pltpu.roll(suf, rows - s, axis=0), 0.0)
        s *= 2
    return pre, suf


def _segment_prefix(x, seg):
    pos = lax.broadcasted_iota(jnp.int32, x.shape, 0) & (seg - 1)
    s = 1
    while s < seg:
        x = x + jnp.where(pos >= s, pltpu.roll(x, s, axis=0), 0.0)
        s *= 2
    return x


def _segment_pick(x, seg, which):
    rows, width = x.shape
    picked = x.reshape(rows // seg, seg, width)[:, which:which + 1, :]
    return jnp.broadcast_to(picked, (rows // seg, seg, width)).reshape(rows, width)


def _segment_last(x, seg):
    return _segment_pick(x, seg, seg - 1)


def _rms(x, g):
    return x * lax.rsqrt(jnp.mean(x * x, axis=-1, keepdims=True) + EPS) * g


def _silu(x):
    return x * jax.nn.sigmoid(x)


def _softplus(x):
    return jnp.maximum(x, 0.0) + jnp.log1p(jnp.exp(-jnp.abs(x)))


def _const_spec(shape):
    zeros = (0,) * len(shape)
    return pl.BlockSpec(shape, lambda *_: zeros, pipeline_mode=pl.Buffered(1))


def _mod_spec(col, tiles_per_seq):
    if tiles_per_seq is None:
        return pl.BlockSpec((None, 1, D_MODEL), lambda i: (0, 0, col))
    return pl.BlockSpec((None, 1, D_MODEL), lambda i: (i // tiles_per_seq, 0, col))


def _params(n_grid):
    return pltpu.CompilerParams(dimension_semantics=("arbitrary",) * n_grid, vmem_limit_bytes=VMEM_LIMIT)


def _ada_kernel(c_ref, w_ref, b_ref, o_ref):
    o_ref[...] = _dot(_silu(c_ref[...]), w_ref[...]) + b_ref[...]


def _ada_mod(cond, w_ada, b_ada):
    depth, _, n = w_ada.shape
    rows = cond.shape[0]
    tn = 1536
    return pl.pallas_call(
        _ada_kernel,
        grid=(depth, n // tn),
        in_specs=[pl.BlockSpec((rows, D_MODEL), lambda l, j: (0, 0)),
                  pl.BlockSpec((None, D_MODEL, tn), lambda l, j: (l, 0, j)),
                  pl.BlockSpec((None, 1, tn), lambda l, j: (l, 0, j))],
        out_specs=pl.BlockSpec((None, rows, tn), lambda l, j: (l, 0, j)),
        out_shape=jax.ShapeDtypeStruct((depth, rows, n), F32),
        compiler_params=_params(2),
        name="ada_mod",
    )(cond, w_ada, b_ada.reshape(depth, 1, n))


def _modulated(x_ref, g_ref, sh_ref, sc_ref, part_rows):
    parts = []
    for r in range(x_ref.shape[0] // part_rows):
        x = x_ref[r * part_rows:(r + 1) * part_rows, :]
        parts.append((_rms(x, g_ref[0:1, :]) * (1.0 + sc_ref[...]) + sh_ref[...]).astype(BF16))
    return parts


def _gdn_inproj_kernel(x_ref, g_ref, sh_ref, sc_ref, w_ref, wg_ref, alog_ref, dt_ref, cw_ref, proj_ref, gate_ref,
                       grow_ref, *, seg):
    n_rows = x_ref.shape[0]
    p_rows = PRE_ROWS
    p_cols = 2 * EPI_COLS
    hb = _modulated(x_ref, g_ref, sh_ref, sc_ref, p_rows)
    period = min(seg, p_rows)
    pos = lax.broadcasted_iota(jnp.int32, (period, HEAD_DIM), 0) & (seg - 1)

    def tap_weights(w):
        taps = {}
        for s in (-2, -1, 1, 2):
            valid = (pos + s >= 0) & (pos + s < seg)
            taps[s] = jnp.tile(jnp.where(valid, w[s + 2:s + 3, :], 0.0), (p_rows // period, 1))
        return taps

    def unit(x, scale):
        return x * (lax.rsqrt(jnp.sum(x * x, axis=-1, keepdims=True) + EPS) * scale)

    pieces = [(j, r) for j in range(GDN_MAIN // p_cols) for r in range(n_rows // p_rows)]

    def product(piece):
        j, r = piece
        return jnp.dot(hb[r], w_ref[:, j * p_cols:(j + 1) * p_cols], preferred_element_type=F32)

    def finish(piece, y):
        j, r = piece
        rows = slice(r * p_rows, (r + 1) * p_rows)
        kind = j * p_cols // KEY_DIM
        if kind == 3:
            proj_ref[rows, j * p_cols:(j + 1) * p_cols] = _silu(y)
            return
        for hh in range(p_cols // HEAD_DIM):
            c0 = j * p_cols + hh * HEAD_DIM
            w = cw_ref[:, c0:c0 + HEAD_DIM]
            taps = tap_weights(w)
            x = y[:, hh * HEAD_DIM:(hh + 1) * HEAD_DIM]
            acc = x * w[CONV_W // 2:CONV_W // 2 + 1, :]
            for s in (-2, -1, 1, 2):
                acc = acc + pltpu.roll(x, (-s) % p_rows, axis=0) * taps[s]
            out = _silu(acc)
            if kind == 0:
                out = unit(out, HEAD_DIM ** -0.5)
            elif kind == 1:
                out = unit(out, 1.0)
            proj_ref[rows, c0:c0 + HEAD_DIM] = out

    y_next = product(pieces[0])
    for n, piece in enumerate(pieces):
        y, y_next = y_next, (product(pieces[n + 1]) if n + 1 < len(pieces) else None)
        finish(piece, y)
    raw = jnp.dot(jnp.concatenate(hb, axis=0), wg_ref[...], preferred_element_type=F32)
    beta = jax.nn.sigmoid(raw)
    g = -jnp.exp(alog_ref[...]) * _softplus(raw + dt_ref[...])
    gp, gs = _segment_sums(g, GDN_CHUNK)
    lane = lax.broadcasted_iota(jnp.int32, raw.shape, 1)
    gates = jnp.where(lane < 2 * N_HEADS, beta,
                      jnp.where(lane < 3 * N_HEADS, gp, jnp.where(lane < 4 * N_HEADS, gs, 0.0)))
    gate_ref[...] = gates

    sums_t = gates.T[2 * N_HEADS:4 * N_HEADS, :]
    for m in range(n_rows // GATE_LANES):
        grow_ref[m] = sums_t[:, m * GATE_LANES:(m + 1) * GATE_LANES]


def _gdn_inproj(x2d, norm_g, mod, tiles_per_seq, w_main, w_gate, alog_row, dt_row, conv_w, seg):
    rows = x2d.shape[0]
    assert PRE_ROWS % seg == 0 and ROW_TILE % PRE_ROWS == 0
    return pl.pallas_call(
        functools.partial(_gdn_inproj_kernel, seg=seg),
        grid=(rows // ROW_TILE,),
        in_specs=[pl.BlockSpec((ROW_TILE, D_MODEL), lambda i: (i, 0)),
                  _const_spec((4, D_MODEL)),
                  _mod_spec(0, tiles_per_seq), _mod_spec(1, tiles_per_seq),
                  _const_spec((D_MODEL, GDN_MAIN)), _const_spec((D_MODEL, GATE_LANES)),
                  _const_spec((1, GATE_LANES)), _const_spec((1, GATE_LANES)),
                  _const_spec((CONV_W, 3 * KEY_DIM))],
        out_specs=[pl.BlockSpec((ROW_TILE, GDN_MAIN), lambda i: (i, 0)),
                   pl.BlockSpec((ROW_TILE, GATE_LANES), lambda i: (i, 0)),
                   pl.BlockSpec((ROW_TILE // GATE_LANES, 2 * N_HEADS, GATE_LANES), lambda i: (i, 0, 0))],
        out_shape=[jax.ShapeDtypeStruct((rows, GDN_MAIN), F32),
                   jax.ShapeDtypeStruct((rows, GATE_LANES), F32),
                   jax.ShapeDtypeStruct((rows // GATE_LANES, 2 * N_HEADS, GATE_LANES), F32)],
        compiler_params=_params(1),
        name="gdn_inproj",
    )(x2d, norm_g, mod, mod, w_main, w_gate, alog_row, dt_row, conv_w)


def _hgrn_inproj_kernel(x_ref, g_ref, sh_ref, sc_ref, w_ref, lbl_ref, out_ref, *, layer):
    n_rows = x_ref.shape[0]
    hb = _modulated(x_ref, g_ref, sh_ref, sc_ref, EPI_ROWS)
    c = HGRN_CHUNK
    depth = lbl_ref.shape[0]
    logits = [lbl_ref[l] for l in range(depth)]
    top = functools.reduce(jnp.maximum, logits)
    ex = [jnp.exp(l - top) for l in logits]
    denom = functools.reduce(lambda x, y: x + y, ex)
    lb = functools.reduce(lambda x, y: x + y, ex[:layer + 1]) / denom - ex[0] / denom

    pieces = [(j, r) for j in range(HGRN_PROJ // EPI_COLS) for r in range(n_rows // EPI_ROWS)]

    def product(piece):
        j, r = piece
        return jnp.dot(hb[r], w_ref[:, j * EPI_COLS:(j + 1) * EPI_COLS], preferred_element_type=F32)

    def finish(piece, y):
        j, r = piece
        rows = slice(r * EPI_ROWS, (r + 1) * EPI_ROWS)
        kind, off = divmod(j * EPI_COLS, KEY_DIM)
        if kind in (0, 3, 4):
            dst = {0: 0, 3: 5, 4: 6}[kind] * KEY_DIM + off
            out_ref[rows, dst:dst + EPI_COLS] = y if kind == 3 else _silu(y)
            return
        d = kind - 1
        for hh in range(EPI_COLS // HEAD_DIM):
            c0 = off + hh * HEAD_DIM
            lbd = lb[d:d + 1, c0:c0 + HEAD_DIM]
            fg = lbd + (1.0 - lbd) * jax.nn.sigmoid(y[:, hh * HEAD_DIM:(hh + 1) * HEAD_DIM])
            logf = jnp.log(fg)
            run = _segment_prefix(logf, c)
            if d == 1:
                run = _segment_last(run, c) - run + logf
            out_ref[rows, (1 + 2 * d) * KEY_DIM + c0:(1 + 2 * d) * KEY_DIM + c0 + HEAD_DIM] = 1.0 - fg
            out_ref[rows, (2 + 2 * d) * KEY_DIM + c0:(2 + 2 * d) * KEY_DIM + c0 + HEAD_DIM] = jnp.exp(run)

    y_next = product(pieces[0])
    for n, piece in enumerate(pieces):
        y, y_next = y_next, (product(pieces[n + 1]) if n + 1 < len(pieces) else None)
        finish(piece, y)


def _hgrn_inproj(x2d, norm_g, mod, tiles_per_seq, w_in, lb_logits, layer):
    rows = x2d.shape[0]
    depth = lb_logits.shape[0]
    return pl.pallas_call(
        functools.partial(_hgrn_inproj_kernel, layer=layer),
        grid=(rows // ROW_TILE,),
        in_specs=[pl.BlockSpec((ROW_TILE, D_MODEL), lambda i: (i, 0)),
                  _const_spec((4, D_MODEL)),
                  _mod_spec(0, tiles_per_seq), _mod_spec(1, tiles_per_seq),
                  _const_spec((D_MODEL, HGRN_PROJ)), _const_spec((depth, 2, KEY_DIM))],
        out_specs=pl.BlockSpec((ROW_TILE, HGRN_OUT), lambda i: (i, 0)),
        out_shape=jax.ShapeDtypeStruct((rows, HGRN_OUT), F32),
        compiler_params=_params(1),
        name="hgrn_inproj",
    )(x2d, norm_g, mod, mod, w_in, lb_logits)


def _split_bf16(x):
    hi = lax.bitcast_convert_type(lax.bitcast_convert_type(x, jnp.int32) & jnp.int32(-65536), F32)
    return hi, x - hi


def _pair_inverses(lows, left, diag2, fill):
    c = lows[0].shape[0]

    def block_diag(x):
        return jnp.concatenate([jnp.where(left, x, 0.0), jnp.where(left, 0.0, x)], axis=0)

    def left_operand(hi, lo):
        return jnp.concatenate([hi.astype(BF16), lo.astype(BF16)] * 2, axis=1)

    def right_operand(hi, lo):
        bh, bl = block_diag(hi).astype(BF16), block_diag(lo).astype(BF16)
        return jnp.concatenate([bh, bh, bl, bl], axis=0)

    accs = [jnp.where(diag2, 1.0, 0.0) - x for x in lows]
    parts = [_split_bf16(x) for x in lows]
    powers = [jnp.dot(left_operand(hi, lo), right_operand(hi, lo), preferred_element_type=F32) for hi, lo in parts]
    fill(0)
    levels = c.bit_length() - 2
    for level in range(levels):
        parts = [_split_bf16(x) for x in powers]
        rhs = [right_operand(hi, lo) for hi, lo in parts]
        acc_lhs = [left_operand(*_split_bf16(a)) for a in accs]
        if level + 1 < levels:
            res = [jnp.dot(jnp.concatenate([left_operand(hi, lo), al], axis=0), r, preferred_element_type=F32)
                   for (hi, lo), al, r in zip(parts, acc_lhs, rhs)]
            fill(level + 1)
            powers = [x[:c] for x in res]
            accs = [a + x[c:] for a, x in zip(accs, res)]
        else:
            accs = [a + jnp.dot(al, r, preferred_element_type=F32) for a, al, r in zip(accs, acc_lhs, rhs)]
    return accs


def _gdn_kernel(qn, kn, vn, gcol_ref, grow_ref, *rest, has_init, emit_state, group, seq):
    rest = list(rest)
    s0_ref = rest.pop(0) if has_init else None
    o_ref = rest.pop(0)
    sfin_ref = rest.pop(0) if emit_state else None
    ob, m_s, b_s, q_s, gl_s, st = rest
    t_len = qn.shape[0]
    c = GDN_CHUNK
    n_chunks = t_len // c
    n_heads = qn.shape[1] // HEAD_DIM
    heads = n_heads // seq
    h0 = pl.program_id(1) * n_heads

    lane = lax.broadcasted_iota(jnp.int32, (c, 2 * c), 1)
    row = lax.broadcasted_iota(jnp.int32, (c, 2 * c), 0)
    left = lane < c
    ahead = jnp.where(left, row - lane, lane - c - row)
    left_row = lax.broadcasted_iota(jnp.int32, (1, 2 * c), 1) < c

    def column(tile, idx):
        return jnp.sum(jnp.where(lane == idx, tile, 0.0), axis=-1, keepdims=True)

    def direction_blocks(a0, a1):
        z0, z1 = jnp.zeros(a0.shape, BF16), jnp.zeros(a1.shape, BF16)
        return jnp.concatenate([jnp.concatenate([a0.astype(BF16), z1], axis=1),
                                jnp.concatenate([z0, a1.astype(BF16)], axis=1)], axis=0)

    def load(hh, n):
        h = h0 + hh
        cidx = [n, n_chunks - 1 - n]
        rows = [pl.ds(pl.multiple_of(ci * c, c), c) for ci in cidx]
        lanes = (slice(hh * HEAD_DIM, (hh + 1) * HEAD_DIM) if isinstance(hh, int)
                 else pl.ds(pl.multiple_of(hh * HEAD_DIM, HEAD_DIM), HEAD_DIM))
        e = dict(hh=hh, n=n, rows=rows, q=[qn[r, lanes] for r in rows], k=[kn[r, lanes] for r in rows],
                 v=[vn[r, lanes] for r in rows])
        gtile = [gcol_ref[r, :] for r in rows]
        e["beta"] = [column(gtile[d], d * N_HEADS + h) for d in range(2)]
        e["gc"] = [column(gtile[d], (2 + d) * N_HEADS + h) for d in range(2)]
        e["g_last"] = [e["gc"][0][c - 1:c, :], e["gc"][1][0:1, :]]
        def window(d, ci):
            w = grow_ref[ci // 2, pl.ds(d * N_HEADS + h, 1), :]
            return jnp.where(ci % 2 == d, w, pltpu.roll(w, c, axis=1))
        gr2 = jnp.where(left_row, window(0, cidx[0]), window(1, cidx[1]))
        e["decay2"] = jnp.exp(jnp.where(ahead >= 0, jnp.where(left, e["gc"][0], e["gc"][1]) - gr2, -jnp.inf))
        return e

    def gram(e):
        k, q = e["k"], e["q"]
        kq = _dot(jnp.concatenate([jnp.concatenate(k, axis=1), jnp.concatenate(q, axis=1)], axis=0),
                  direction_blocks(k[0], k[1]), _NT)
        e["low2"] = jnp.where(ahead > 0, kq[:c] * jnp.where(left, e["beta"][0], e["beta"][1]) * e["decay2"], 0.0)
        e["qk2"] = kq[c:] * e["decay2"]

    def solve(e, t2):
        k, v, beta, gc = e["k"], e["v"], e["beta"], e["gc"]
        e["eg"] = [jnp.exp(gc[d]) for d in range(2)]
        rhs = [jnp.concatenate([v[d] * beta[d], k[d] * beta[d] * e["eg"][d]], axis=1) for d in range(2)]
        e["uw"] = jnp.dot(t2.astype(BF16), direction_blocks(*rhs), preferred_element_type=F32)

    def fold(e):
        uw = e["uw"]
        r2 = direction_blocks(uw[:, :2 * HEAD_DIM], uw[:, 2 * HEAD_DIM:])
        e["oq"] = jnp.dot(e["qk2"].astype(BF16), r2, preferred_element_type=F32)
        kt = jnp.concatenate([e["k"][d] * jnp.exp(e["g_last"][d] - e["gc"][d]) for d in range(2)], axis=0)
        e["bm"] = _dot(kt, r2, _TN)

    def store(e):
        hh, n, oq, bm = e["hh"], e["n"], e["oq"], e["bm"]
        for d in range(2):
            base = 2 * d * HEAD_DIM
            ob[hh, d, e["rows"][d], :] = oq[:, base:base + HEAD_DIM]
            q_s[hh, n, d] = (e["q"][d] * e["eg"][d] - oq[:, base + HEAD_DIM:base + 2 * HEAD_DIM]).astype(BF16)
            b_s[hh, n, d] = bm[:, base:base + HEAD_DIM]
            m_s[hh, n, d] = bm[:, base + HEAD_DIM:base + 2 * HEAD_DIM].astype(BF16)
            gl_s[hh, n, d] = jnp.broadcast_to(jnp.exp(e["g_last"][d]), (1, HEAD_DIM))

    def scan_step(base, n):
        for hh in (base + j for j in range(heads)):
            for d in range(2):
                rows = pl.ds(pl.multiple_of((n if d == 0 else n_chunks - 1 - n) * c, c), c)
                state = st[hh, d]
                ms = jnp.dot(jnp.concatenate([m_s[hh, n, d], q_s[hh, n, d]], axis=0), state.astype(BF16),
                             preferred_element_type=F32)
                ob[hh, d, rows, :] = ob[hh, d, rows, :] + ms[HEAD_DIM:]
                st[hh, d] = state * gl_s[hh, n, d] - ms[:HEAD_DIM] + b_s[hh, n, d]

    def prepare(base, first, behind):
        pending = [] if behind is None else [behind[1] + j for j in range(group)]
        n_slots = 8

        def fill(slot):
            for j in range(slot * len(pending) // n_slots, (slot + 1) * len(pending) // n_slots):
                scan_step(behind[0], pending[j])

        chunks = [load(base + hh, first + j) for j in range(group) for hh in range(heads)]
        for e in chunks:
            gram(e)
        fill(0)
        inverses = _pair_inverses([e["low2"] for e in chunks], left, ahead == 0, lambda level: fill(1 + level))
        for e, t2 in zip(chunks, inverses):
            solve(e, t2)
        fill(6)
        for e in chunks:
            fold(e)
        fill(7)
        for e in chunks:
            store(e)

    for hh in range(n_heads):
        for d in range(2):
            st[hh, d] = s0_ref[d, hh] if has_init else jnp.zeros((HEAD_DIM, HEAD_DIM), F32)

    n_steps = n_chunks // group
    prepare(0, 0, None)

    def step(u, carry):
        unit = lambda v: ((v // n_steps) * heads, (v % n_steps) * group) if seq > 1 else (0, v * group)
        prepare(*unit(u), unit(u - 1))
        return carry

    lax.fori_loop(1, seq * n_steps, step, 0)
    for j in range(group):
        scan_step((seq - 1) * heads, (n_steps - 1) * group + j)

    def post(i, carry):
        rows = pl.ds(pl.multiple_of(i * PRE_ROWS, PRE_ROWS), PRE_ROWS)
        for hh in range(n_heads):
            o_ref[rows, hh * HEAD_DIM:(hh + 1) * HEAD_DIM] = ob[hh, 0, rows, :] + ob[hh, 1, rows, :]
        return carry

    lax.fori_loop(0, t_len // PRE_ROWS, post, 0)
    if emit_state:
        for hh in range(n_heads):
            for d in range(2):
                sfin_ref[d, hh] = st[hh, d]


def _gdn_mixer(proj, gates, grow, s0, layer_idx, emit_state):
    b, t, _ = proj.shape
    c = GDN_CHUNK
    n_chunks = t // c
    assert t % PRE_ROWS == 0 and 2 * c == HEAD_DIM
    has_init = s0 is not None
    group = min(GDN_GROUP, n_chunks)
    lock = max(1, GDN_GROUP // n_chunks)
    seq = GDN_SEQ
    heads = lock * seq
    width = heads * HEAD_DIM
    col = lambda off: pl.BlockSpec((None, t, width), lambda i, h: (i, 0, off // heads + h))
    in_specs = [col(0), col(N_HEADS), col(2 * N_HEADS),
                pl.BlockSpec((None, t, GATE_LANES), lambda i, h: (i, 0, 0)),
                pl.BlockSpec((t // GATE_LANES, 2 * N_HEADS, GATE_LANES), lambda i, h: (i, 0, 0))]
    args = [proj, proj, proj, gates, grow]
    if has_init:
        in_specs.append(pl.BlockSpec((None, None, 2, heads, HEAD_DIM, HEAD_DIM),
                                     lambda i, h: (i, layer_idx, 0, h, 0, 0)))
        args.append(s0)
    out_specs = [pl.BlockSpec((None, t, width), lambda i, h: (i, 0, h))]
    out_shape = [jax.ShapeDtypeStruct((b, t, KEY_DIM), F32)]
    if emit_state:
        out_specs.append(pl.BlockSpec((None, 2, heads, HEAD_DIM, HEAD_DIM), lambda i, h: (i, 0, h, 0, 0)))
        out_shape.append(jax.ShapeDtypeStruct((b, 2, N_HEADS, HEAD_DIM, HEAD_DIM), F32))
    scratch = [
        pltpu.VMEM((heads, 2, t, HEAD_DIM), F32),
        pltpu.VMEM((heads, n_chunks, 2, HEAD_DIM, HEAD_DIM), BF16),
        pltpu.VMEM((heads, n_chunks, 2, HEAD_DIM, HEAD_DIM), F32),
        pltpu.VMEM((heads, n_chunks, 2, c, HEAD_DIM), BF16),
        pltpu.VMEM((heads, n_chunks, 2, 1, HEAD_DIM), F32),
        pltpu.VMEM((heads, 2, HEAD_DIM, HEAD_DIM), F32)]
    res = pl.pallas_call(
        functools.partial(_gdn_kernel, has_init=has_init, emit_state=emit_state, group=group, seq=seq),
        grid=(b, N_HEADS // heads),
        in_specs=in_specs,
        out_specs=out_specs,
        out_shape=out_shape,
        scratch_shapes=scratch,
        compiler_params=_params(2),
        name="gdn_mixer",
    )(*args)
    return res[0], (res[1] if emit_state else None)


def _hgrn_kernel(q_ref, kf_ref, ef_ref, kb_ref, eb_ref, i_ref, *rest, has_init, emit_state):
    rest = list(rest)
    s0_ref = rest.pop(0) if has_init else None
    o_ref = rest.pop(0)
    sfin_ref = rest.pop(0) if emit_state else None
    qin, kout, ktail, ftot, ob, st = rest
    t_len = q_ref.shape[0]
    c = HGRN_CHUNK

    def pre(d, rows, q):
        k = (kf_ref, kb_ref)[d][rows, :]
        e = (ef_ref, eb_ref)[d][rows, :]
        whole = _segment_pick(e, c, c - 1 if d == 0 else 0)
        k_over_e = k / e
        qin[d, rows, :] = q * e
        kout[d, rows, :] = k_over_e.astype(BF16)
        ktail[d, rows, :] = k_over_e * whole
        ftot[d, rows, :] = whole

    def pre_body(i, carry):
        rows = pl.ds(pl.multiple_of(i * HGRN_BLOCK, HGRN_BLOCK), HGRN_BLOCK)
        q = q_ref[rows, :]
        pre(0, rows, q)
        pre(1, rows, q)
        return carry

    lax.fori_loop(0, t_len // HGRN_BLOCK, pre_body, 0)

    for d in range(2):
        st[d] = s0_ref[d].T if has_init else jnp.zeros((HEAD_DIM, HEAD_DIM), F32)

    grp = HGRN_GROUP
    per = grp // c
    assert per == 4
    blk = min(HGRN_SCAN_ROWS, t_len)
    ri = lax.broadcasted_iota(jnp.int32, (grp, grp), 0)
    ci = lax.broadcasted_iota(jnp.int32, (grp, grp), 1)
    row_chunk = lax.broadcasted_iota(jnp.int32, (grp, HEAD_DIM), 0) // c

    def group_factors(f, d):
        r1, r2, r3 = (pltpu.roll(f, s * c, axis=0) for s in (1, 2, 3))
        prev, nxt = ((r1, r2, r3), (r3, r2, r1)) if d == 0 else ((r3, r2, r1), (r1, r2, r3))
        order = row_chunk if d == 0 else per - 1 - row_chunk
        g1, h1 = prev[0], nxt[0]
        g2, h2 = g1 * prev[1], h1 * nxt[1]
        g3, h3 = g2 * prev[2], h2 * nxt[2]
        before = jnp.where(order == 0, 1.0, jnp.where(order == 1, g1, jnp.where(order == 2, g2, g3)))
        after = jnp.where(order == 3, 1.0, jnp.where(order == 2, h1, jnp.where(order == 1, h2, h3)))
        whole = f[0:1] * f[c:c + 1] * f[2 * c:2 * c + 1] * f[3 * c:3 * c + 1]
        return g1, g2, before, after, whole

    def att_select(d, p1, p234):
        dist = (ri // c - ci // c) if d == 0 else (ci // c - ri // c)
        inside = (ci <= ri) if d == 0 else (ci >= ri)
        return jnp.where((dist == 0) & inside, p1,
                         jnp.where(dist == 1, p234[:grp],
                                   jnp.where(dist == 2, p234[grp:2 * grp],
                                             jnp.where(dist == 3, p234[2 * grp:], 0.0))))

    def body(i, carry):
        ctx = []
        for step in range(blk // grp):
            for d in range(2):
                g_idx = i * (blk // grp) + step
                r0 = pl.multiple_of((g_idx if d == 0 else t_len // grp - 1 - g_idx) * grp, grp)
                rows = pl.ds(r0, grp)
                e = dict(d=d, rows=rows, qi=qin[d, rows, :], ko=kout[d, rows, :], kt=ktail[d, rows, :],
                         v=i_ref[rows, :].astype(BF16))
                e["g1"], e["g2"], e["before"], e["after"], e["whole"] = group_factors(ftot[d, rows, :], d)
                ctx.append(e)
        for e in ctx:
            qi = e["qi"]
            e["p1"] = _dot(qi, e["ko"], _NT)
            e["p234"] = _dot(jnp.concatenate([qi, qi * e["g1"], qi * e["g2"]], axis=0), e["kt"], _NT)
        for e in ctx:
            e["ds"] = _dot(e["v"], e["kt"] * e["after"], _TN)
        for e in ctx:
            e["intra"] = _dot(att_select(e["d"], e["p1"], e["p234"]), e["v"])
        states = [st[0], st[1]]
        for e in ctx:
            d = e["d"]
            ob[d, e["rows"], :] = e["intra"] + _dot(e["qi"] * e["before"], states[d], _NT)
            states[d] = states[d] * e["whole"] + e["ds"]
        st[0], st[1] = states
        return carry

    lax.fori_loop(0, t_len // blk, body, 0)

    def post(i, carry):
        rows = pl.ds(pl.multiple_of(i * PRE_ROWS, PRE_ROWS), PRE_ROWS)
        o_ref[rows, :] = ob[0, rows, :] + ob[1, rows, :]
        return carry

    lax.fori_loop(0, t_len // PRE_ROWS, post, 0)
    if emit_state:
        for d in range(2):
            sfin_ref[d] = st[d].T


def _hgrn_mixer(proj, s0, layer_idx, emit_state):
    b, t, _ = proj.shape
    has_init = s0 is not None
    col = lambda off: pl.BlockSpec((None, t, HEAD_DIM), lambda i, h: (i, 0, off + h))
    in_specs = [col(j * N_HEADS) for j in range(6)]
    args = [proj] * 6
    if has_init:
        in_specs.append(pl.BlockSpec((None, None, 2, None, HEAD_DIM, HEAD_DIM),
                                     lambda i, h: (i, layer_idx, 0, h, 0, 0)))
        args.append(s0)
    out_specs = [pl.BlockSpec((None, t, HEAD_DIM), lambda i, h: (i, 0, h))]
    out_shape = [jax.ShapeDtypeStruct((b, t, KEY_DIM), F32)]
    if emit_state:
        out_specs.append(pl.BlockSpec((None, 2, None, HEAD_DIM, HEAD_DIM), lambda i, h: (i, 0, h, 0, 0)))
        out_shape.append(jax.ShapeDtypeStruct((b, 2, N_HEADS, HEAD_DIM, HEAD_DIM), F32))
    res = pl.pallas_call(
        functools.partial(_hgrn_kernel, has_init=has_init, emit_state=emit_state),
        grid=(b, N_HEADS),
        in_specs=in_specs,
        out_specs=out_specs,
        out_shape=out_shape,
        scratch_shapes=[pltpu.VMEM((2, t, HEAD_DIM), F32), pltpu.VMEM((2, t, HEAD_DIM), BF16),
                        pltpu.VMEM((2, t, HEAD_DIM), F32), pltpu.VMEM((2, t, HEAD_DIM), F32),
                        pltpu.VMEM((2, t, HEAD_DIM), F32), pltpu.VMEM((2, HEAD_DIM, HEAD_DIM), F32)],
        compiler_params=_params(2),
        name="hgrn_mixer",
    )(*args)
    return res[0], (res[1] if emit_state else None)


def _post_kernel(x_ref, o_ref, z_ref, on_ref, g_ref, gt1_ref, sh2_ref, sc2_ref, gt2_ref, wo_ref, w1_ref, w2_ref,
                 y_ref):
    rows = x_ref.shape[0]
    n_parts = 2
    part = rows // n_parts
    mixes = []
    for r in range(n_parts):
        rs = slice(r * part, (r + 1) * part)
        gated = [(_rms(o_ref[rs, h * HEAD_DIM:(h + 1) * HEAD_DIM], on_ref[...])
                  * z_ref[rs, h * HEAD_DIM:(h + 1) * HEAD_DIM]).astype(BF16) for h in range(N_HEADS)]
        mixes.append(jnp.dot(jnp.concatenate(gated, axis=1), wo_ref[...], preferred_element_type=F32))
    mix = jnp.concatenate(mixes, axis=0)
    x1 = x_ref[...] + gt1_ref[...] * _rms(mix, g_ref[1:2, :])
    hb = (_rms(x1, g_ref[2:3, :]) * (1.0 + sc2_ref[...]) + sh2_ref[...]).astype(BF16)
    ff = jnp.zeros(x1.shape, F32)
    for j in range(D_FF // 1024):
        cols = slice(j * 1024, (j + 1) * 1024)
        hid = jnp.maximum(jnp.dot(hb, w1_ref[:, cols], preferred_element_type=F32), 0.0)
        ff = ff + jnp.dot((hid * hid).astype(BF16), w2_ref[cols, :], preferred_element_type=F32)
    y_ref[...] = x1 + gt2_ref[...] * _rms(ff, g_ref[3:4, :])


def _post_mixer(x2d, o2d, proj2d, z_block, onorm_g, norm_g, mod, tiles_per_seq, w_out, w1, w2):
    rows = x2d.shape[0]
    tile = pl.BlockSpec((ROW_TILE, D_MODEL), lambda i: (i, 0))
    return pl.pallas_call(
        _post_kernel,
        grid=(rows // ROW_TILE,),
        in_specs=[tile, tile, pl.BlockSpec((ROW_TILE, D_MODEL), lambda i: (i, z_block)),
                  _const_spec((1, HEAD_DIM)), _const_spec((4, D_MODEL)),
                  _mod_spec(2, tiles_per_seq), _mod_spec(3, tiles_per_seq),
                  _mod_spec(4, tiles_per_seq), _mod_spec(5, tiles_per_seq),
                  _const_spec((D_MODEL, D_MODEL)), _const_spec((D_MODEL, D_FF)), _const_spec((D_FF, D_MODEL))],
        out_specs=tile,
        out_shape=jax.ShapeDtypeStruct((rows, D_MODEL), F32),
        compiler_params=_params(1),
        name="post_mixer",
    )(x2d, o2d, proj2d, onorm_g.reshape(1, HEAD_DIM), norm_g, mod, mod, mod, mod, w_out, w1, w2)


def _trunk(x, mod_rows, per_seq_mod, s_gdn, s_hgrn, grid_conv, emit_state, weights):
    (norm_g, gdn_main, gdn_gate, gdn_alog, gdn_dt, gdn_conv_w, gdn_onorm_g, gdn_w_out,
     hgrn_w_in, hgrn_lb_logits, hgrn_onorm_g, hgrn_w_out, mlp_w1, mlp_w2) = weights
    b, t, _ = x.shape
    tiles_per_seq = t // ROW_TILE if per_seq_mod else None
    x2d = x.reshape(b * t, D_MODEL)
    depth = norm_g.shape[0]
    fin_gdn, fin_hgrn = [], []
    for layer in range(depth):
        j = layer // 2
        mod = mod_rows[layer]
        if layer % 2 == 0:
            proj, gates, grow = _gdn_inproj(x2d, norm_g[layer], mod, tiles_per_seq, gdn_main[j], gdn_gate[j],
                                            gdn_alog[j], gdn_dt[j], gdn_conv_w[j], GRID_W if grid_conv else t)
            o, fin = _gdn_mixer(proj.reshape(b, t, GDN_MAIN), gates.reshape(b, t, GATE_LANES), grow, s_gdn, j,
                                emit_state)
            fin_gdn.append(fin)
            z_block, onorm_g, w_out = 3, gdn_onorm_g[j], gdn_w_out[j]
        else:
            proj = _hgrn_inproj(x2d, norm_g[layer], mod, tiles_per_seq, hgrn_w_in[j], hgrn_lb_logits, layer)
            o, fin = _hgrn_mixer(proj.reshape(b, t, HGRN_OUT), s_hgrn, j, emit_state)
            fin_hgrn.append(fin)
            z_block, onorm_g, w_out = 6, hgrn_onorm_g[j], hgrn_w_out[j]
        x2d = _post_mixer(x2d, o.reshape(b * t, KEY_DIM), proj, z_block, onorm_g, norm_g[layer], mod, tiles_per_seq,
                          w_out, mlp_w1[layer], mlp_w2[layer])
    y = x2d.reshape(b, t, D_MODEL)
    if emit_state:
        return y, jnp.stack(fin_gdn, axis=1), jnp.stack(fin_hgrn, axis=1)
    return y, None, None


def kernel(x_prompt, x_sample, state_gdn, state_hgrn, c, c_ctx, w_ada, b_ada, norm_g, gdn_w_in, gdn_conv_w,
           gdn_a_log, gdn_dt_bias, gdn_onorm_g, gdn_w_out, hgrn_w_in, hgrn_lb_logits, hgrn_onorm_g, hgrn_w_out,
           mlp_w1, mlp_w2):
    n_dec = c.shape[0]
    n_rows = 16
    cond = jnp.concatenate([c_ctx[None, :], c, jnp.zeros((n_rows - 1 - n_dec, D_MODEL), F32)], axis=0)
    mod = _ada_mod(cond, w_ada, b_ada)
    mod_ctx = mod[:, 0:1, None, :]
    mod_smp = mod[:, 1:1 + n_dec, None, :]

    n_gdn = gdn_w_in.shape[0]
    gate_pad = GATE_LANES - 4 * N_HEADS
    gdn_gate = jnp.pad(gdn_w_in[:, :, GDN_MAIN:], ((0, 0), (0, 0), (0, gate_pad))).astype(BF16)
    lead = jnp.zeros((n_gdn, 2 * N_HEADS), F32)
    tail = jnp.zeros((n_gdn, gate_pad), F32)
    gdn_alog = jnp.concatenate([lead, gdn_a_log.reshape(n_gdn, 2 * N_HEADS), tail], axis=1)[:, None, :]
    gdn_dt = jnp.concatenate([lead, gdn_dt_bias.reshape(n_gdn, 2 * N_HEADS), tail], axis=1)[:, None, :]
    weights = (norm_g, gdn_w_in[:, :, :GDN_MAIN].astype(BF16), gdn_gate, gdn_alog, gdn_dt, gdn_conv_w,
               gdn_onorm_g, gdn_w_out.astype(BF16), hgrn_w_in.astype(BF16), hgrn_lb_logits, hgrn_onorm_g,
               hgrn_w_out.astype(BF16), mlp_w1.astype(BF16), mlp_w2.astype(BF16))

    y_prompt, new_gdn, new_hgrn = _trunk(x_prompt, mod_ctx, False, None, None, False, True, weights)
    y_sample, _, _ = _trunk(x_sample, mod_smp, True, state_gdn, state_hgrn, True, False, weights)
    return (y_prompt, y_sample, new_gdn, new_hgrn)
```
